```python
import math
import jax, jax.numpy as jnp
from jax import lax
import numpy as np

D_MODEL = 1024
BATCH = 8
SEQ = 4096
DEPTH = 1

GRID_W = 64
CTX_LEN = 256
N_MOD = 9
D_FF = 2816
MLA_HEADS = 12
QK_NOPE = 64
QK_ROPE = 32
V_HEAD = 64
Q_RANK = 256
KV_RANK = 128
MLA_WIDTH = MLA_HEADS * V_HEAD
SSM_GROUPS = 16
SSM_CH = 16
SSM_WIDTH = SSM_GROUPS * SSM_CH
SSM_STATE = 64
D_MIX = MLA_WIDTH + SSM_WIDTH
D_IN = Q_RANK + KV_RANK + QK_ROPE + SSM_WIDTH
ROPE_BASE = 10000.0
ATTN_SCALE = (QK_NOPE + QK_ROPE) ** -0.5
Q_BLOCK = 128
EPS = 1e-6
DT_MIN = 1e-3
DT_MAX = 1e-1

kernel_name = "hymba_mla_s5_macaron_dit"


def rmsnorm(x, g):
    xf = x.astype(jnp.float32)
    y = xf * lax.rsqrt(jnp.mean(xf * xf, axis=-1, keepdims=True) + EPS)
    return (y * g.astype(jnp.float32)).astype(x.dtype)


def adaln(cvec, w_mod, b_mod):
    m = jax.nn.silu(cvec) @ w_mod + b_mod
    m = m.reshape(cvec.shape[0], 1, N_MOD, D_MODEL)
    return [m[:, :, i] for i in range(N_MOD)]


def modulate(x, g, shift, scale):
    return rmsnorm(x, g) * (1.0 + scale) + shift


def swiglu(h, w_gu, w_down):
    gate, up = jnp.split(h @ w_gu, 2, axis=-1)
    return (jax.nn.silu(gate) * up) @ w_down


def axial_rope_tables(n_tokens, dtype):
    rows = n_tokens // GRID_W
    row = jnp.repeat(jnp.arange(rows), GRID_W).astype(jnp.float32)
    col = jnp.tile(jnp.arange(GRID_W), rows).astype(jnp.float32)
    per_axis = QK_ROPE // 2
    inv_freq = ROPE_BASE ** (-jnp.arange(0, per_axis, 2, dtype=jnp.float32) / per_axis)
    ang = jnp.concatenate([row[:, None] * inv_freq, col[:, None] * inv_freq], axis=-1)
    return jnp.cos(ang).astype(dtype), jnp.sin(ang).astype(dtype)


def apply_rope(t, cos, sin):
    tp = t.reshape(t.shape[:-1] + (t.shape[-1] // 2, 2))
    t1, t2 = tp[..., 0], tp[..., 1]
    return jnp.stack([t1 * cos - t2 * sin, t1 * sin + t2 * cos], axis=-1).reshape(t.shape)


def project(h, w_in, g_cq, w_uq, g_ckv, w_ukv):
    bsz, n = h.shape[:2]
    p = h @ w_in
    c_q, c_kv, k_rope, u = jnp.split(p, [Q_RANK, Q_RANK + KV_RANK, Q_RANK + KV_RANK + QK_ROPE], axis=-1)
    q = (rmsnorm(c_q, g_cq) @ w_uq).reshape(bsz, n, MLA_HEADS, QK_NOPE + QK_ROPE)
    kv = (rmsnorm(c_kv, g_ckv) @ w_ukv).reshape(bsz, n, MLA_HEADS, QK_NOPE + V_HEAD)
    q_nope, q_rope = jnp.split(q, [QK_NOPE], axis=-1)
    k_nope, v = jnp.split(kv, [QK_NOPE], axis=-1)
    return q_nope, q_rope, k_nope, v, k_rope, u


def attend(q_nope, q_rope, k_nope, k_rope, v):
    s = (jnp.einsum('bqhd,bkhd->bhqk', q_nope, k_nope)
         + jnp.einsum('bqhr,bkr->bhqk', q_rope, k_rope)).astype(jnp.float32) * ATTN_SCALE
    p = jax.nn.softmax(s, axis=-1).astype(v.dtype)
    return jnp.einsum('bhqk,bkhd->bqhd', p, v)


def latent_attention(q_nope, q_rope, k_nope, k_rope, v):
    bsz, n = q_nope.shape[:2]
    nblk = n // Q_BLOCK

    def blocks(t):
        return t.reshape((bsz, nblk, Q_BLOCK) + t.shape[2:]).swapaxes(0, 1)

    o = lax.map(lambda q: attend(q[0], q[1], k_nope, k_rope, v), (blocks(q_nope), blocks(q_rope)))
    return o.swapaxes(0, 1).reshape(bsz, n, MLA_WIDTH)


def s5_discretize(lam_re, lam_im, log_dt, b_re, b_im):
    dt = jnp.exp(log_dt.astype(jnp.float32))[:, None]
    lr = jnp.minimum(lam_re.astype(jnp.float32), -1e-4)
    li = lam_im.astype(jnp.float32)
    mag = jnp.exp(lr * dt)
    ar, ai = mag * jnp.cos(li * dt), mag * jnp.sin(li * dt)
    den = lr * lr + li * li
    fr = ((ar - 1.0) * lr + ai * li) / den
    fi = (ai * lr - (ar - 1.0) * li) / den
    br, bi = b_re.astype(jnp.float32), b_im.astype(jnp.float32)
    bbr = fr[..., None] * br - fi[..., None] * bi
    bbi = fr[..., None] * bi + fi[..., None] * br
    return ar, ai, bbr, bbi


def complex_scan(ar, ai, br, bi, reverse):
    n = br.shape[1]
    a_r = jnp.broadcast_to(ar, (1, n) + ar.shape)
    a_i = jnp.broadcast_to(ai, (1, n) + ai.shape)

    def combine(e1, e2):
        a1r, a1i, b1r, b1i = e1
        a2r, a2i, b2r, b2i = e2
        return (a2r * a1r - a2i * a1i, a2r * a1i + a2i * a1r,
                a2r * b1r - a2i * b1i + b2r, a2r * b1i + a2i * b1r + b2i)

    return lax.associative_scan(combine, (a_r, a_i, br, bi), reverse=reverse, axis=1)


def s5_drive(u, bbr, bbi):
    return jnp.einsum('btgc,gpc->btgp', u, bbr), jnp.einsum('btgc,gpc->btgp', u, bbi)


def s5_readout(xr, xi, c_re, c_im):
    return (jnp.einsum('btgp,gcp->btgc', xr, c_re.astype(jnp.float32))
            - jnp.einsum('btgp,gcp->btgc', xi, c_im.astype(jnp.float32)))


def s5_glu(y, w_glu, dtype):
    bsz, n = y.shape[:2]
    z = jax.nn.gelu(y.reshape(bsz, n, SSM_WIDTH))
    a, g = jnp.split(z @ w_glu.astype(jnp.float32), 2, axis=-1)
    return (a * jax.nn.sigmoid(g)).astype(dtype)


def s5_branch(u_c, u_x, lam_re, lam_im, log_dt, b_re, b_im, c_re, c_im, d_skip, w_glu, with_ctx_out):
    bsz = u_x.shape[0]
    uc = u_c.reshape(bsz, u_c.shape[1], SSM_GROUPS, SSM_CH).astype(jnp.float32)
    ux = u_x.reshape(bsz, u_x.shape[1], SSM_GROUPS, SSM_CH).astype(jnp.float32)
    dsk = d_skip.reshape(SSM_GROUPS, SSM_CH).astype(jnp.float32)
    y_x = ux * dsk
    y_c = uc * dsk
    for d, rev in ((0, False), (1, True)):
        ar, ai, bbr, bbi = s5_discretize(lam_re[d], lam_im[d], log_dt[d], b_re[d], b_im[d])
        bcr, bci = s5_drive(uc, bbr, bbi)
        _, _, xcr, xci = complex_scan(ar, ai, bcr, bci, rev)
        end = 0 if rev else -1
        h0r, h0i = xcr[:, end][:, None], xci[:, end][:, None]
        bxr, bxi = s5_drive(ux, bbr, bbi)
        apr, api, xr, xi = complex_scan(ar, ai, bxr, bxi, rev)
        xr, xi = xr + apr * h0r - api * h0i, xi + apr * h0i + api * h0r
        y_x = y_x + s5_readout(xr, xi, c_re[d], c_im[d])
        if with_ctx_out:
            y_c = y_c + s5_readout(xcr, xci, c_re[d], c_im[d])
    out_x = s5_glu(y_x, w_glu, u_x.dtype)
    out_c = s5_glu(y_c, w_glu, u_c.dtype) if with_ctx_out else None
    return out_x, out_c


def hybrid_mixer(hx, hc, cos, sin, w_in, g_cq, w_uq, g_ckv, w_ukv, lam_re, lam_im, log_dt,
                 b_re, b_im, c_re, c_im, d_skip, w_glu, g_mla_out, g_ssm_out, w_out, with_ctx_out):
    bsz, n = hx.shape[:2]
    qn_x, qr_x, kn_x, v_x, kr_x, u_x = project(hx, w_in, g_cq, w_uq, g_ckv, w_ukv)
    qn_c, qr_c, kn_c, v_c, kr_c, u_c = project(hc, w_in, g_cq, w_uq, g_ckv, w_ukv)
    qr_x = apply_rope(qr_x, cos[:, None, :], sin[:, None, :])
    kr_x = apply_rope(kr_x, cos, sin)
    kn_all = jnp.concatenate([kn_c, kn_x], axis=1)
    kr_all = jnp.concatenate([kr_c, kr_x], axis=1)
    v_all = jnp.concatenate([v_c, v_x], axis=1)
    attn_x = latent_attention(qn_x, qr_x, kn_all, kr_all, v_all)
    ssm_x, ssm_c = s5_branch(u_c, u_x, lam_re, lam_im, log_dt, b_re, b_im, c_re, c_im,
                             d_skip, w_glu, with_ctx_out)
    out_x = jnp.concatenate([rmsnorm(attn_x, g_mla_out), rmsnorm(ssm_x, g_ssm_out)], axis=-1) @ w_out
    out_c = None
    if with_ctx_out:
        attn_c = attend(qn_c, qr_c, kn_c, kr_c, v_c).reshape(bsz, hc.shape[1], MLA_WIDTH)
        out_c = jnp.concatenate([rmsnorm(attn_c, g_mla_out), rmsnorm(ssm_c, g_ssm_out)], axis=-1) @ w_out
    return out_x, out_c


def setup_inputs(seed: int = 0) -> dict:
    key = jax.random.key(seed)
    ks = jax.random.split(key, 32)
    f32 = jnp.float32
    L = DEPTH

    def nrm(k, shape, s):
        return jax.random.normal(k, shape, f32) * s

    def gain(k, shape):
        return 1.0 + 0.01 * jax.random.normal(k, shape, f32)

    n_idx = jnp.arange(SSM_STATE, dtype=f32)
    ssm_shape = (L, 2, SSM_GROUPS, SSM_STATE)
    return {
        "x": nrm(ks[0], (BATCH, SEQ, D_MODEL), 1.0),
        "c": nrm(ks[1], (BATCH, D_MODEL), 1.0),
        "ctx": nrm(ks[2], (BATCH, CTX_LEN, D_MODEL), 1.0),
        "c_ctx": nrm(ks[3], (D_MODEL,), 1.0),
        "w_mod": nrm(ks[4], (L, D_MODEL, N_MOD * D_MODEL), 0.5 * D_MODEL ** -0.5),
        "b_mod": nrm(ks[5], (L, N_MOD * D_MODEL), 0.01),
        "g_ffn1": gain(ks[6], (L, D_MODEL)),
        "w_gu1": nrm(ks[7], (L, D_MODEL, 2 * D_FF), D_MODEL ** -0.5),
        "w_down1": nrm(ks[8], (L, D_FF, D_MODEL), D_FF ** -0.5),
        "g_mix": gain(ks[9], (L, D_MODEL)),
        "w_in": nrm(ks[10], (L, D_MODEL, D_IN), D_MODEL ** -0.5),
        "g_cq": gain(ks[11], (L, Q_RANK)),
        "w_uq": nrm(ks[12], (L, Q_RANK, MLA_HEADS * (QK_NOPE + QK_ROPE)), Q_RANK ** -0.5),
        "g_ckv": gain(ks[13], (L, KV_RANK)),
        "w_ukv": nrm(ks[14], (L, KV_RANK, MLA_HEADS * (QK_NOPE + V_HEAD)), KV_RANK ** -0.5),
        "lam_re": -0.5 + nrm(ks[15], ssm_shape, 0.01),
        "lam_im": math.pi * n_idx + nrm(ks[16], ssm_shape, 0.01),
        "log_dt": jax.random.uniform(ks[17], (L, 2, SSM_GROUPS), f32, math.log(DT_MIN), math.log(DT_MAX)),
        "b_re": nrm(ks[18], (L, 2, SSM_GROUPS, SSM_STATE, SSM_CH), (2 * SSM_CH) ** -0.5),
        "b_im": nrm(ks[19], (L, 2, SSM_GROUPS, SSM_STATE, SSM_CH), (2 * SSM_CH) ** -0.5),
        "c_re": nrm(ks[20], (L, 2, SSM_GROUPS, SSM_CH, SSM_STATE), SSM_STATE ** -0.5),
        "c_im": nrm(ks[21], (L, 2, SSM_GROUPS, SSM_CH, SSM_STATE), SSM_STATE ** -0.5),
        "d_skip": nrm(ks[22], (L, SSM_WIDTH), 1.0),
        "w_glu": nrm(ks[23], (L, SSM_WIDTH, 2 * SSM_WIDTH), SSM_WIDTH ** -0.5),
        "g_mla_out": gain(ks[24], (L, MLA_WIDTH)),
        "g_ssm_out": gain(ks[25], (L, SSM_WIDTH)),
        "w_out": nrm(ks[26], (L, D_MIX, D_MODEL), D_MIX ** -0.5),
        "g_ffn2": gain(ks[27], (L, D_MODEL)),
        "w_gu2": nrm(ks[28], (L, D_MODEL, 2 * D_FF), D_MODEL ** -0.5),
        "w_down2": nrm(ks[29], (L, D_FF, D_MODEL), D_FF ** -0.5),
        "g_final": gain(ks[30], (D_MODEL,)),
    }


def reference(x, c, ctx, c_ctx, w_mod, b_mod, g_ffn1, w_gu1, w_down1, g_mix, w_in, g_cq, w_uq,
              g_ckv, w_ukv, lam_re, lam_im, log_dt, b_re, b_im, c_re, c_im, d_skip, w_glu,
              g_mla_out, g_ssm_out, w_out, g_ffn2, w_gu2, w_down2, g_final):
    cos, sin = axial_rope_tables(x.shape[1], x.dtype)
    for l in range(DEPTH):
        last = l == DEPTH - 1
        mx = adaln(c, w_mod[l], b_mod[l])
        mc = adaln(c_ctx[None], w_mod[l], b_mod[l])
        x = x + 0.5 * mx[2] * swiglu(modulate(x, g_ffn1[l], mx[0], mx[1]), w_gu1[l], w_down1[l])
        ctx = ctx + 0.5 * mc[2] * swiglu(modulate(ctx, g_ffn1[l], mc[0], mc[1]), w_gu1[l], w_down1[l])
        hx = modulate(x, g_mix[l], mx[3], mx[4])
        hc = modulate(ctx, g_mix[l], mc[3], mc[4])
        mix_x, mix_c = hybrid_mixer(hx, hc, cos, sin, w_in[l], g_cq[l], w_uq[l], g_ckv[l], w_ukv[l],
                                    lam_re[l], lam_im[l], log_dt[l], b_re[l], b_im[l], c_re[l], c_im[l],
                                    d_skip[l], w_glu[l], g_mla_out[l], g_ssm_out[l], w_out[l],
                                    not last)
        x = x + mx[5] * mix_x
        x = x + 0.5 * mx[8] * swiglu(modulate(x, g_ffn2[l], mx[6], mx[7]), w_gu2[l], w_down2[l])
        if not last:
            ctx = ctx + mc[5] * mix_c
            ctx = ctx + 0.5 * mc[8] * swiglu(modulate(ctx, g_ffn2[l], mc[6], mc[7]), w_gu2[l], w_down2[l])
    return rmsnorm(x, g_final)
```

```python
import functools
import math

import jax
import jax.numpy as jnp
from jax import lax
from jax.experimental import pallas as pl
from jax.experimental.pallas import tpu as pltpu

D_MODEL = 1024
GRID_W = 64
N_MOD = 9
D_FF = 2816
MLA_HEADS = 12
QK_NOPE = 64
QK_ROPE = 32
V_HEAD = 64
Q_RANK = 256
KV_RANK = 128
MLA_WIDTH = MLA_HEADS * V_HEAD
SSM_GROUPS = 16
SSM_CH = 16
SSM_WIDTH = SSM_GROUPS * SSM_CH
SSM_STATE = 64
N_STATE = SSM_GROUPS * SSM_STATE
ROPE_BASE = 10000.0
ATTN_SCALE = (QK_NOPE + QK_ROPE) ** -0.5
EPS = 1e-6

LANES = 128
SUBLANES = 8
HEAD_PAD = 128
VMEM_LIMIT = 56 * 1024 * 1024

F32 = jnp.float32
BF16 = jnp.bfloat16


def _rms(x, g):
    ms = jnp.mean(x * x, axis=-1, keepdims=True)
    return x * lax.rsqrt(ms + EPS) * g


def _dot(a, b):
    return jnp.dot(a, b, preferred_element_type=F32)


def _const_spec(shape):
    zeros = (0,) * len(shape)
    return pl.BlockSpec(shape, lambda *_: zeros, pipeline_mode=pl.Buffered(1))


def _params(*sem):
    return pltpu.CompilerParams(dimension_semantics=sem, vmem_limit_bytes=VMEM_LIMIT)


def _adaln_kernel(c_ref, w_ref, b_ref, o_ref):
    c = c_ref[...]
    s = (c * jax.nn.sigmoid(c)).astype(BF16)
    o_ref[...] = _dot(s, w_ref[...].astype(BF16)) + b_ref[...]


def _adaln(cvec, w_mod, b_mod):
    rows = cvec.shape[0]
    n = w_mod.shape[1]
    bn = D_MODEL
    return pl.pallas_call(
        _adaln_kernel,
        grid=(n // bn,),
        in_specs=[pl.BlockSpec((rows, D_MODEL), lambda i: (0, 0)),
                  pl.BlockSpec((D_MODEL, bn), lambda i: (0, i)),
                  pl.BlockSpec((1, bn), lambda i: (0, i))],
        out_specs=pl.BlockSpec((rows, bn), lambda i: (0, i)),
        out_shape=jax.ShapeDtypeStruct((rows, n), F32),
        compiler_params=_params("parallel"),
        name="adaln",
    )(cvec, w_mod, b_mod)


FF_CHUNKS = 2
FF_CHUNK = D_FF // FF_CHUNKS


def _swiglu(h, wgu_ref, wd_ref):
    acc = None
    for c in range(FF_CHUNKS):
        lo = c * FF_CHUNK
        gate = _dot(h, wgu_ref[:, lo:lo + FF_CHUNK])
        up = _dot(h, wgu_ref[:, D_FF + lo:D_FF + lo + FF_CHUNK])
        act = (gate * jax.nn.sigmoid(gate) * up).astype(BF16)
        part = _dot(act, wd_ref[lo:lo + FF_CHUNK, :])
        acc = part if acc is None else acc + part
    return acc


def _ffn_kernel(x_ref, mod_ref, g_ref, wgu_ref, wd_ref, o_ref):
    x = x_ref[0]
    m = mod_ref[0]
    h = (_rms(x, g_ref[...]) * (1.0 + m[1:2]) + m[0:1]).astype(BF16)
    o_ref[0] = x + 0.5 * m[2:3] * _swiglu(h, wgu_ref, wd_ref)


def _ffn(x, mod, g, wgu, wd, mod_row, tm):
    nb, n, _ = x.shape
    mod_map = (lambda b, j: (b, 0, 0)) if mod_row is None else (lambda b, j: (mod_row, 0, 0))
    return pl.pallas_call(
        _ffn_kernel,
        grid=(nb, n // tm),
        in_specs=[pl.BlockSpec((1, tm, D_MODEL), lambda b, j: (b, j, 0)),
                  pl.BlockSpec((1, N_MOD, D_MODEL), mod_map),
                  _const_spec((1, D_MODEL)),
                  _const_spec((D_MODEL, 2 * D_FF)),
                  _const_spec((D_FF, D_MODEL))],
        out_specs=pl.BlockSpec((1, tm, D_MODEL), lambda b, j: (b, j, 0)),
        out_shape=jax.ShapeDtypeStruct(x.shape, F32),
        compiler_params=_params("parallel", "parallel"),
        name="ffn",
    )(x, mod, g, wgu, wd)


W_A_COLS = Q_RANK + KV_RANK + 2 * HEAD_PAD + SSM_WIDTH
QK_WIDTH = MLA_HEADS * HEAD_PAD


def _proj_kernel(x_ref, mod_ref, g_ref, wa_ref, gcq_ref, wq_ref, gckv_ref, wkv_ref,
                 cq_ref, sq_ref, ck_ref, sk_ref, *out_refs, with_q):
    if with_q:
        q_ref, k_ref, v_ref, u_ref = out_refs
    else:
        k_ref, v_ref, u_ref = out_refs
    x = x_ref[0]
    m = mod_ref[0]
    h = (_rms(x, g_ref[...]) * (1.0 + m[4:5]) + m[3:4]).astype(BF16)
    p = _dot(h, wa_ref[...])
    o_kv = Q_RANK
    o_kr = o_kv + KV_RANK
    o_rot = o_kr + HEAD_PAD
    o_u = o_rot + HEAD_PAD
    u_ref[...] = p[:, o_u:o_u + SSM_WIDTH]
    kr = p[:, o_kr:o_kr + HEAD_PAD] * ck_ref[...] + p[:, o_rot:o_rot + HEAD_PAD] * sk_ref[...]
    ckv = _rms(p[:, o_kv:o_kv + KV_RANK], gckv_ref[...]).astype(BF16)
    kv = _dot(ckv, wkv_ref[...])
    for hd in range(MLA_HEADS):
        lo = hd * HEAD_PAD
        k_ref[0, :, lo:lo + HEAD_PAD] = (kv[:, lo:lo + HEAD_PAD] + kr).astype(BF16)
    v_ref[0] = kv[:, QK_WIDTH:QK_WIDTH + MLA_WIDTH].astype(BF16)
    if with_q:
        cq = _rms(p[:, 0:Q_RANK], gcq_ref[...]).astype(BF16)
        qa = _dot(cq, wq_ref[:, 0:QK_WIDTH])
        qb = _dot(cq, wq_ref[:, QK_WIDTH:2 * QK_WIDTH])
        cq_t = cq_ref[...]
        sq_t = sq_ref[...]
        for hd in range(MLA_HEADS):
            lo = hd * HEAD_PAD
            q_ref[0, :, lo:lo + HEAD_PAD] = (qa[:, lo:lo + HEAD_PAD] * cq_t
                                             + qb[:, lo:lo + HEAD_PAD] * sq_t).astype(BF16)


def _proj(x, mod, g_mix, wa, gcq, wq, gckv, wkv, tabs, mod_row, tm, with_q):
    nb, n, _ = x.shape
    mod_map = (lambda b, j: (b, 0, 0)) if mod_row is None else (lambda b, j: (mod_row, 0, 0))
    tab_spec = pl.BlockSpec((tm, HEAD_PAD), lambda b, j: (j, 0))
    out_shape = [jax.ShapeDtypeStruct((nb, n, QK_WIDTH), BF16),
                 jax.ShapeDtypeStruct((nb, n, MLA_WIDTH), BF16),
                 jax.ShapeDtypeStruct((n, nb * SSM_WIDTH), F32)]
    out_specs = [pl.BlockSpec((1, tm, QK_WIDTH), lambda b, j: (b, j, 0)),
                 pl.BlockSpec((1, tm, MLA_WIDTH), lambda b, j: (b, j, 0)),
                 pl.BlockSpec((tm, SSM_WIDTH), lambda b, j: (j, b))]
    if with_q:
        out_shape = [jax.ShapeDtypeStruct((nb, n, QK_WIDTH), BF16)] + out_shape
        out_specs = [pl.BlockSpec((1, tm, QK_WIDTH), lambda b, j: (b, j, 0))] + out_specs
    return pl.pallas_call(
        functools.partial(_proj_kernel, with_q=with_q),
        grid=(nb, n // tm),
        in_specs=[pl.BlockSpec((1, tm, D_MODEL), lambda b, j: (b, j, 0)),
                  pl.BlockSpec((1, N_MOD, D_MODEL), mod_map),
                  _const_spec((1, D_MODEL)),
                  _const_spec(wa.shape),
                  _const_spec((1, Q_RANK)),
                  _const_spec(wq.shape),
                  _const_spec((1, KV_RANK)),
                  _const_spec(wkv.shape),
                  tab_spec, tab_spec, tab_spec, tab_spec],
        out_specs=out_specs,
        out_shape=out_shape,
        compiler_params=_params("parallel", "parallel"),
        name="proj_q" if with_q else "proj_ctx",
    )(x, mod, g_mix, wa, gcq, wq, gckv, wkv, *tabs)


HEADS_PER_STEP = 2


def _attn_kernel(q_ref, kx_ref, kc_ref, vx_ref, vc_ref, o_ref):
    nt = (((1,), (1,)), ((), ()))
    vx = vx_ref[0]
    vc = vc_ref[0]
    outs = []
    for hh in range(HEADS_PER_STEP):
        lo = hh * HEAD_PAD
        q = q_ref[0, :, lo:lo + HEAD_PAD]
        sx = lax.dot_general(q, kx_ref[0, :, lo:lo + HEAD_PAD], nt, preferred_element_type=F32)
        sc = lax.dot_general(q, kc_ref[0, :, lo:lo + HEAD_PAD], nt, preferred_element_type=F32)
        mx = jnp.maximum(jnp.max(sx, axis=-1, keepdims=True), jnp.max(sc, axis=-1, keepdims=True))
        px = jnp.exp(sx - mx)
        pc = jnp.exp(sc - mx)
        den = jnp.sum(px, axis=-1, keepdims=True) + jnp.sum(pc, axis=-1, keepdims=True)
        o = _dot(px.astype(BF16), vx) + _dot(pc.astype(BF16), vc)
        outs.append(o / den)
    lane = lax.broadcasted_iota(jnp.int32, outs[0].shape, 1)
    o_ref[0] = jnp.where(lane < V_HEAD, outs[0], outs[1])


def _attention(q, kx, kc, vx, vc, tq):
    nb, n, _ = q.shape
    nk = kx.shape[1]
    nc = kc.shape[1]
    pairs = MLA_HEADS // HEADS_PER_STEP
    qk_blk = HEADS_PER_STEP * HEAD_PAD
    v_blk = HEADS_PER_STEP * V_HEAD
    return pl.pallas_call(
        _attn_kernel,
        grid=(nb, pairs, n // tq),
        in_specs=[pl.BlockSpec((1, tq, qk_blk), lambda b, p, j: (b, j, p)),
                  pl.BlockSpec((1, nk, qk_blk), lambda b, p, j: (b, 0, p)),
                  pl.BlockSpec((1, nc, qk_blk), lambda b, p, j: (b, 0, p)),
                  pl.BlockSpec((1, nk, v_blk), lambda b, p, j: (b, 0, p)),
                  pl.BlockSpec((1, nc, v_blk), lambda b, p, j: (b, 0, p))],
        out_specs=pl.BlockSpec((1, tq, v_blk), lambda b, p, j: (b, j, p)),
        out_shape=jax.ShapeDtypeStruct((nb, n, MLA_WIDTH), F32),
        compiler_params=_params("parallel", "parallel", "parallel"),
        name="attention",
    )(q, kx, kc, vx, vc)


def _s5_kernel(uc_ref, ux_ref, lre_ref, lim_ref, ldt_ref, bre_ref, bim_ref, cre_ref, cim_ref,
               y_ref, bb_ref, cm_ref, a_ref, h_ref, bd_ref, *, t_chunk):
    d = pl.program_id(0)
    j = pl.program_id(1)

    @pl.when(j == 0)
    def _discretize():
        dt = jnp.exp(ldt_ref[0])
        lr = jnp.minimum(lre_ref[0], -1e-4)
        li = lim_ref[0]
        mag = jnp.exp(lr * dt)
        ar = mag * jnp.cos(li * dt)
        ai = mag * jnp.sin(li * dt)
        den = lr * lr + li * li
        fr = ((ar - 1.0) * lr + ai * li) / den
        fi = (ai * lr - (ar - 1.0) * li) / den
        bre = bre_ref[0]
        bim = bim_ref[0]
        bb_ref[:, 0:N_STATE] = (fr * bre - fi * bim).astype(BF16)
        bb_ref[:, N_STATE:2 * N_STATE] = (fr * bim + fi * bre).astype(BF16)
        cm_ref[0:N_STATE, :] = cre_ref[0].astype(BF16)
        cm_ref[N_STATE:2 * N_STATE, :] = (-cim_ref[0]).astype(BF16)
        a_ref[0:SUBLANES, :] = jnp.broadcast_to(ar, (SUBLANES, N_STATE))
        a_ref[SUBLANES:2 * SUBLANES, :] = jnp.broadcast_to(ai, (SUBLANES, N_STATE))
        h_ref[...] = jnp.zeros_like(h_ref)

    def run(u_ref, readout):
        bd_ref[...] = _dot(u_ref[...].astype(BF16), bb_ref[...])
        ar = a_ref[0:SUBLANES, :]
        ai = a_ref[SUBLANES:2 * SUBLANES, :]

        def step(i, carry):
            hr, hi = carry
            tt = jnp.where(d == 0, i, t_chunk - 1 - i)
            r = pl.multiple_of(tt * SUBLANES, SUBLANES)
            br = bd_ref[pl.ds(r, SUBLANES), 0:N_STATE]
            bi = bd_ref[pl.ds(r, SUBLANES), N_STATE:2 * N_STATE]
            nr = ar * hr - ai * hi + br
            ni = ar * hi + ai * hr + bi
            bd_ref[pl.ds(r, SUBLANES), 0:N_STATE] = nr
            bd_ref[pl.ds(r, SUBLANES), N_STATE:2 * N_STATE] = ni
            return nr, ni

        hr, hi = lax.fori_loop(0, t_chunk, step,
                               (h_ref[0:SUBLANES, :], h_ref[SUBLANES:2 * SUBLANES, :]), unroll=8)
        h_ref[0:SUBLANES, :] = hr
        h_ref[SUBLANES:2 * SUBLANES, :] = hi
        if readout:
            y_ref[0] = _dot(bd_ref[...].astype(BF16), cm_ref[...])

    @pl.when(j == 0)
    def _context():
        run(uc_ref, False)

    @pl.when(j > 0)
    def _latent():
        run(ux_ref, True)


def _s5(u_c, u_x, lre, lim, ldt, bre, bim, cre, cim, t_chunk):
    rows = t_chunk * SUBLANES
    assert u_c.shape[0] == rows
    nch = u_x.shape[0] // rows

    def chunk(d, j):
        jj = jnp.maximum(j - 1, 0)
        return jnp.where(d == 0, jj, nch - 1 - jj)

    vec_spec = pl.BlockSpec((1, 1, N_STATE), lambda d, j: (d, 0, 0))
    return pl.pallas_call(
        functools.partial(_s5_kernel, t_chunk=t_chunk),
        grid=(2, nch + 1),
        in_specs=[pl.BlockSpec((rows, SSM_WIDTH), lambda d, j: (0, 0)),
                  pl.BlockSpec((rows, SSM_WIDTH), lambda d, j: (chunk(d, j), 0)),
                  vec_spec, vec_spec, vec_spec,
                  pl.BlockSpec((1, SSM_WIDTH, N_STATE), lambda d, j: (d, 0, 0)),
                  pl.BlockSpec((1, SSM_WIDTH, N_STATE), lambda d, j: (d, 0, 0)),
                  pl.BlockSpec((1, N_STATE, SSM_WIDTH), lambda d, j: (d, 0, 0)),
                  pl.BlockSpec((1, N_STATE, SSM_WIDTH), lambda d, j: (d, 0, 0))],
        out_specs=pl.BlockSpec((1, rows, SSM_WIDTH), lambda d, j: (d, chunk(d, j), 0)),
        out_shape=jax.ShapeDtypeStruct((2, u_x.shape[0], SSM_WIDTH), F32),
        scratch_shapes=[pltpu.VMEM((SSM_WIDTH, 2 * N_STATE), BF16),
                        pltpu.VMEM((2 * N_STATE, SSM_WIDTH), BF16),
                        pltpu.VMEM((2 * SUBLANES, N_STATE), F32),
                        pltpu.VMEM((2 * SUBLANES, N_STATE), F32),
                        pltpu.VMEM((rows, 2 * N_STATE), F32)],
        compiler_params=_params("arbitrary", "arbitrary"),
        name="s5_scan",
    )(u_c, u_x, lre, lim, ldt, bre, bim, cre, cim)


def _gelu_tanh(x):
    return 0.5 * x * (1.0 + jnp.tanh(math.sqrt(2.0 / math.pi) * (x + 0.044715 * (x * x * x))))


def _final_kernel(x_ref, attn_ref, u_ref, y_ref, mod_ref, dsk_ref, wglu_ref, gmla_ref, gssm_ref,
                  woa_ref, wos_ref, g2_ref, wgu_ref, wd_ref, gf_ref, o_ref):
    x = x_ref[0]
    m = mod_ref[0]
    y = u_ref[...] * dsk_ref[...] + y_ref[0] + y_ref[1]
    z = _gelu_tanh(y).astype(BF16)
    ag = _dot(z, wglu_ref[...])
    ssm = ag[:, 0:SSM_WIDTH] * jax.nn.sigmoid(ag[:, SSM_WIDTH:2 * SSM_WIDTH])
    a_n = _rms(attn_ref[0], gmla_ref[...]).astype(BF16)
    s_n = _rms(ssm, gssm_ref[...]).astype(BF16)
    mix = _dot(a_n, woa_ref[...]) + _dot(s_n, wos_ref[...])
    x = x + m[5:6] * mix
    h = (_rms(x, g2_ref[...]) * (1.0 + m[7:8]) + m[6:7]).astype(BF16)
    x = x + 0.5 * m[8:9] * _swiglu(h, wgu_ref, wd_ref)
    o_ref[0] = _rms(x, gf_ref[...])


def _final(x, attn, u, y, mod, dsk, wglu, gmla, gssm, woa, wos, g2, wgu, wd, gf, tm):
    nb, n, _ = x.shape
    return pl.pallas_call(
        _final_kernel,
        grid=(nb, n // tm),
        in_specs=[pl.BlockSpec((1, tm, D_MODEL), lambda b, j: (b, j, 0)),
                  pl.BlockSpec((1, tm, MLA_WIDTH), lambda b, j: (b, j, 0)),
                  pl.BlockSpec((tm, SSM_WIDTH), lambda b, j: (j, b)),
                  pl.BlockSpec((2, tm, SSM_WIDTH), lambda b, j: (0, j, b)),
                  pl.BlockSpec((1, N_MOD, D_MODEL), lambda b, j: (b, 0, 0)),
                  _const_spec((1, SSM_WIDTH)),
                  _const_spec((SSM_WIDTH, 2 * SSM_WIDTH)),
                  _const_spec((1, MLA_WIDTH)),
                  _const_spec((1, SSM_WIDTH)),
                  _const_spec((MLA_WIDTH, D_MODEL)),
                  _const_spec((SSM_WIDTH, D_MODEL)),
                  _const_spec((1, D_MODEL)),
                  _const_spec((D_MODEL, 2 * D_FF)),
                  _const_spec((D_FF, D_MODEL)),
                  _const_spec((1, D_MODEL))],
        out_specs=pl.BlockSpec((1, tm, D_MODEL), lambda b, j: (b, j, 0)),
        out_shape=jax.ShapeDtypeStruct(x.shape, F32),
        compiler_params=_params("parallel", "parallel"),
        name="final",
    )(x, attn, u, y, mod, dsk, wglu, gmla, gssm, woa, wos, g2, wgu, wd, gf)


def _rot_cols(w):
    wp = w.reshape(w.shape[0], -1, 2)
    return jnp.stack([-wp[..., 1], wp[..., 0]], axis=-1).reshape(w.shape)


def _rope_tables(n_tokens):
    rows = n_tokens // GRID_W
    row = jnp.repeat(jnp.arange(rows), GRID_W).astype(F32)
    col = jnp.tile(jnp.arange(GRID_W), rows).astype(F32)
    per_axis = QK_ROPE // 2
    inv_freq = ROPE_BASE ** (-jnp.arange(0, per_axis, 2, dtype=F32) / per_axis)
    ang = jnp.concatenate([row[:, None] * inv_freq, col[:, None] * inv_freq], axis=-1)
    cos = jnp.repeat(jnp.cos(ang), 2, axis=-1)
    sin = jnp.repeat(jnp.sin(ang), 2, axis=-1)
    return cos, sin


def _head_tables(cos, sin, nope_fill, scale):
    n = cos.shape[0]
    pad = HEAD_PAD - QK_NOPE - QK_ROPE
    c = jnp.concatenate([jnp.full((n, QK_NOPE), nope_fill, F32), cos, jnp.zeros((n, pad), F32)], axis=-1)
    s = jnp.concatenate([jnp.zeros((n, QK_NOPE), F32), sin, jnp.zeros((n, pad), F32)], axis=-1)
    return c * scale, s * scale


def _pad_head_cols(w_nope, w_rope):
    k, h = w_nope.shape[:2]
    pad = HEAD_PAD - QK_NOPE - QK_ROPE
    return jnp.concatenate([w_nope, w_rope, jnp.zeros((k, h, pad), w_nope.dtype)], axis=-1).reshape(k, h * HEAD_PAD)


def kernel(x, c, ctx, c_ctx, w_mod, b_mod, g_ffn1, w_gu1, w_down1, g_mix, w_in, g_cq, w_uq, g_ckv, w_ukv,
           lam_re, lam_im, log_dt, b_re, b_im, c_re, c_im, d_skip, w_glu, g_mla_out, g_ssm_out, w_out,
           g_ffn2, w_gu2, w_down2, g_final):
    bsz, seq, _ = x.shape
    n_ctx = ctx.shape[1]
    assert w_mod.shape[0] == 1, "single-layer block"
    l = 0

    rows = 2 * SUBLANES
    cvec = jnp.concatenate([c, c_ctx[None], jnp.zeros((rows - bsz - 1, D_MODEL), F32)], axis=0)
    mod = _adaln(cvec, w_mod[l], b_mod[l][None]).reshape(rows, N_MOD, D_MODEL)
    ctx_row = bsz

    wgu1 = w_gu1[l].astype(BF16)
    wd1 = w_down1[l].astype(BF16)
    x1 = _ffn(x, mod, g_ffn1[l][None], wgu1, wd1, None, 512)
    ctx1 = _ffn(ctx.reshape(1, bsz * n_ctx, D_MODEL), mod, g_ffn1[l][None], wgu1, wd1, ctx_row, 512)
    ctx1 = ctx1.reshape(bsz, n_ctx, D_MODEL)

    win = w_in[l]
    o_kv = Q_RANK
    o_kr = o_kv + KV_RANK
    o_u = o_kr + QK_ROPE
    w_kr = win[:, o_kr:o_kr + QK_ROPE]
    zk = jnp.zeros((D_MODEL, QK_NOPE), F32)
    zp = jnp.zeros((D_MODEL, HEAD_PAD - QK_NOPE - QK_ROPE), F32)
    wa = jnp.concatenate([win[:, 0:o_kr], zk, w_kr, zp, zk, _rot_cols(w_kr), zp, win[:, o_u:]], axis=-1).astype(BF16)
    wuq = w_uq[l].reshape(Q_RANK, MLA_HEADS, QK_NOPE + QK_ROPE)
    wq_nope, wq_rope = wuq[..., :QK_NOPE], wuq[..., QK_NOPE:]
    wq_rot = _rot_cols(wq_rope.reshape(Q_RANK, -1)).reshape(wq_rope.shape)
    wq = jnp.concatenate([_pad_head_cols(wq_nope, wq_rope),
                          _pad_head_cols(jnp.zeros_like(wq_nope), wq_rot)], axis=-1).astype(BF16)
    wukv = w_ukv[l].reshape(KV_RANK, MLA_HEADS, QK_NOPE + V_HEAD)
    wk = _pad_head_cols(wukv[..., :QK_NOPE], jnp.zeros((KV_RANK, MLA_HEADS, QK_ROPE), F32))
    wv = wukv[..., QK_NOPE:].reshape(KV_RANK, MLA_WIDTH)
    wkv = jnp.concatenate([wk, wv], axis=-1).astype(BF16)

    cos, sin = _rope_tables(seq)
    cq_t, sq_t = _head_tables(cos, sin, 1.0, ATTN_SCALE)
    ck_t, sk_t = _head_tables(cos, sin, 0.0, 1.0)
    one = jnp.ones((n_ctx, QK_ROPE), F32)
    ckc_t, skc_t = _head_tables(one, jnp.zeros_like(one), 0.0, 1.0)

    gmix = g_mix[l][None]
    q, kx, vx, ux = _proj(x1, mod, gmix, wa, g_cq[l][None], wq, g_ckv[l][None], wkv,
                          (cq_t, sq_t, ck_t, sk_t), None, 512, True)
    kc, vc, uc = _proj(ctx1, mod, gmix, wa, g_cq[l][None], wq, g_ckv[l][None], wkv,
                       (ckc_t, skc_t, ckc_t, skc_t), ctx_row, n_ctx, False)

    attn = _attention(q, kx, kc, vx, vc, 512)

    eye = jnp.eye(SSM_GROUPS, dtype=F32)
    lre = lam_re[l].reshape(2, 1, N_STATE)
    lim = lam_im[l].reshape(2, 1, N_STATE)
    ldt = jnp.repeat(log_dt[l], SSM_STATE, axis=-1).reshape(2, 1, N_STATE)
    bre = jnp.einsum('dgpc,gh->dgchp', b_re[l], eye).reshape(2, SSM_WIDTH, N_STATE)
    bim = jnp.einsum('dgpc,gh->dgchp', b_im[l], eye).reshape(2, SSM_WIDTH, N_STATE)
    cre = jnp.einsum('dgcp,gh->dgphc', c_re[l], eye).reshape(2, N_STATE, SSM_WIDTH)
    cim = jnp.einsum('dgcp,gh->dgphc', c_im[l], eye).reshape(2, N_STATE, SSM_WIDTH)
    y = _s5(uc.reshape(n_ctx * bsz, SSM_WIDTH), ux.reshape(seq * bsz, SSM_WIDTH),
            lre, lim, ldt, bre, bim, cre, cim, n_ctx)
    y = y.reshape(2, seq, bsz * SSM_WIDTH)

    wo = w_out[l].astype(BF16)
    return _final(x1, attn, ux, y, mod, d_skip[l][None], w_glu[l].astype(BF16), g_mla_out[l][None],
                  g_ssm_out[l][None], wo[:MLA_WIDTH], wo[MLA_WIDTH:], g_ffn2[l][None],
                  w_gu2[l].astype(BF16), w_down2[l].astype(BF16), g_final[None], 512)
```

```python
import functools
import math

import jax
import jax.numpy as jnp
from jax import lax
from jax.experimental import pallas as pl
from jax.experimental.pallas import tpu as pltpu

D_MODEL = 1024
GRID_W = 64
N_MOD = 9
D_FF = 2816
MLA_HEADS = 12
QK_NOPE = 64
QK_ROPE = 32
V_HEAD = 64
Q_RANK = 256
KV_RANK = 128
MLA_WIDTH = MLA_HEADS * V_HEAD
SSM_GROUPS = 16
SSM_CH = 16
SSM_WIDTH = SSM_GROUPS * SSM_CH
SSM_STATE = 64
N_STATE = SSM_GROUPS * SSM_STATE
ROPE_BASE = 10000.0
ATTN_SCALE = (QK_NOPE + QK_ROPE) ** -0.5
EPS = 1e-6

LANES = 128
SUBLANES = 8
HEAD_PAD = 128
VMEM_LIMIT = 56 * 1024 * 1024

F32 = jnp.float32
BF16 = jnp.bfloat16


def _rms(x, g):
    ms = jnp.mean(x * x, axis=-1, keepdims=True)
    return x * lax.rsqrt(ms + EPS) * g


def _dot(a, b):
    return jnp.dot(a, b, preferred_element_type=F32)


def _const_spec(shape):
    zeros = (0,) * len(shape)
    return pl.BlockSpec(shape, lambda *_: zeros, pipeline_mode=pl.Buffered(1))


def _params(*sem):
    return pltpu.CompilerParams(dimension_semantics=sem, vmem_limit_bytes=VMEM_LIMIT)


def _adaln_kernel(c_ref, w_ref, b_ref, o_ref):
    c = c_ref[...]
    s = (c * jax.nn.sigmoid(c)).astype(BF16)
    o_ref[...] = _dot(s, w_ref[...].astype(BF16)) + b_ref[...]


def _adaln(cvec, w_mod, b_mod):
    rows = cvec.shape[0]
    n = w_mod.shape[1]
    bn = D_MODEL
    return pl.pallas_call(
        _adaln_kernel,
        grid=(n // bn,),
        in_specs=[pl.BlockSpec((rows, D_MODEL), lambda i: (0, 0)),
                  pl.BlockSpec((D_MODEL, bn), lambda i: (0, i)),
                  pl.BlockSpec((1, bn), lambda i: (0, i))],
        out_specs=pl.BlockSpec((rows, bn), lambda i: (0, i)),
        out_shape=jax.ShapeDtypeStruct((rows, n), F32),
        compiler_params=_params("parallel"),
        name="adaln",
    )(cvec, w_mod, b_mod)


FF_CHUNKS = 2
FF_CHUNK = D_FF // FF_CHUNKS


def _swiglu(h, wgu_ref, wd_ref):
    acc = None
    for c in range(FF_CHUNKS):
        lo = c * FF_CHUNK
        gate = _dot(h, wgu_ref[:, lo:lo + FF_CHUNK])
        up = _dot(h, wgu_ref[:, D_FF + lo:D_FF + lo + FF_CHUNK])
        act = (gate * jax.nn.sigmoid(gate) * up).astype(BF16)
        part = _dot(act, wd_ref[lo:lo + FF_CHUNK, :])
        acc = part if acc is None else acc + part
    return acc


def _ffn_kernel(x_ref, mod_ref, g_ref, wgu_ref, wd_ref, o_ref):
    x = x_ref[0]
    m = mod_ref[0]
    h = (_rms(x, g_ref[...]) * (1.0 + m[1:2]) + m[0:1]).astype(BF16)
    o_ref[0] = x + 0.5 * m[2:3] * _swiglu(h, wgu_ref, wd_ref)


def _ffn(x, mod, g, wgu, wd, mod_row, tm):
    nb, n, _ = x.shape
    mod_map = (lambda b, j: (b, 0, 0)) if mod_row is None else (lambda b, j: (mod_row, 0, 0))
    return pl.pallas_call(
        _ffn_kernel,
        grid=(nb, n // tm),
        in_specs=[pl.BlockSpec((1, tm, D_MODEL), lambda b, j: (b, j, 0)),
                  pl.BlockSpec((1, N_MOD, D_MODEL), mod_map),
                  _const_spec((1, D_MODEL)),
                  _const_spec((D_MODEL, 2 * D_FF)),
                  _const_spec((D_FF, D_MODEL))],
        out_specs=pl.BlockSpec((1, tm, D_MODEL), lambda b, j: (b, j, 0)),
        out_shape=jax.ShapeDtypeStruct(x.shape, F32),
        compiler_params=_params("parallel", "parallel"),
        name="ffn",
    )(x, mod, g, wgu, wd)


W_A_COLS = Q_RANK + KV_RANK + 2 * HEAD_PAD + SSM_WIDTH
QK_WIDTH = MLA_HEADS * HEAD_PAD


def _proj_kernel(x_ref, mod_ref, g_ref, wa_ref, gcq_ref, wq_ref, gckv_ref, wkv_ref,
                 cq_ref, sq_ref, ck_ref, sk_ref, *out_refs, with_q):
    if with_q:
        q_ref, k_ref, v_ref, u_ref = out_refs
    else:
        k_ref, v_ref, u_ref = out_refs
    x = x_ref[0]
    m = mod_ref[0]
    h = (_rms(x, g_ref[...]) * (1.0 + m[4:5]) + m[3:4]).astype(BF16)
    p = _dot(h, wa_ref[...])
    o_kv = Q_RANK
    o_kr = o_kv + KV_RANK
    o_rot = o_kr + HEAD_PAD
    o_u = o_rot + HEAD_PAD
    u_ref[...] = p[:, o_u:o_u + SSM_WIDTH]
    kr = p[:, o_kr:o_kr + HEAD_PAD] * ck_ref[...] + p[:, o_rot:o_rot + HEAD_PAD] * sk_ref[...]
    ckv = _rms(p[:, o_kv:o_kv + KV_RANK], gckv_ref[...]).astype(BF16)
    kv = _dot(ckv, wkv_ref[...])
    for hd in range(MLA_HEADS):
        lo = hd * HEAD_PAD
        k_ref[0, :, lo:lo + HEAD_PAD] = (kv[:, lo:lo + HEAD_PAD] + kr).astype(BF16)
    v_ref[0] = kv[:, QK_WIDTH:QK_WIDTH + MLA_WIDTH].astype(BF16)
    if with_q:
        cq = _rms(p[:, 0:Q_RANK], gcq_ref[...]).astype(BF16)
        qa = _dot(cq, wq_ref[:, 0:QK_WIDTH])
        qb = _dot(cq, wq_ref[:, QK_WIDTH:2 * QK_WIDTH])
        cq_t = cq_ref[...]
        sq_t = sq_ref[...]
        for hd in range(MLA_HEADS):
            lo = hd * HEAD_PAD
            q_ref[0, :, lo:lo + HEAD_PAD] = (qa[:, lo:lo + HEAD_PAD] * cq_t
                                             + qb[:, lo:lo + HEAD_PAD] * sq_t).astype(BF16)


def _proj(x, mod, g_mix, wa, gcq, wq, gckv, wkv, tabs, mod_row, tm, with_q):
    nb, n, _ = x.shape
    mod_map = (lambda b, j: (b, 0, 0)) if mod_row is None else (lambda b, j: (mod_row, 0, 0))
    tab_spec = pl.BlockSpec((tm, HEAD_PAD), lambda b, j: (j, 0))
    out_shape = [jax.ShapeDtypeStruct((nb, n, QK_WIDTH), BF16),
                 jax.ShapeDtypeStruct((nb, n, MLA_WIDTH), BF16),
                 jax.ShapeDtypeStruct((n, nb * SSM_WIDTH), F32)]
    out_specs = [pl.BlockSpec((1, tm, QK_WIDTH), lambda b, j: (b, j, 0)),
                 pl.BlockSpec((1, tm, MLA_WIDTH), lambda b, j: (b, j, 0)),
                 pl.BlockSpec((tm, SSM_WIDTH), lambda b, j: (j, b))]
    if with_q:
        out_shape = [jax.ShapeDtypeStruct((nb, n, QK_WIDTH), BF16)] + out_shape
        out_specs = [pl.BlockSpec((1, tm, QK_WIDTH), lambda b, j: (b, j, 0))] + out_specs
    return pl.pallas_call(
        functools.partial(_proj_kernel, with_q=with_q),
        grid=(nb, n // tm),
        in_specs=[pl.BlockSpec((1, tm, D_MODEL), lambda b, j: (b, j, 0)),
                  pl.BlockSpec((1, N_MOD, D_MODEL), mod_map),
                  _const_spec((1, D_MODEL)),
                  _const_spec(wa.shape),
                  _const_spec((1, Q_RANK)),
                  _const_spec(wq.shape),
                  _const_spec((1, KV_RANK)),
                  _const_spec(wkv.shape),
                  tab_spec, tab_spec, tab_spec, tab_spec],
        out_specs=out_specs,
        out_shape=out_shape,
        compiler_params=_params("parallel", "parallel"),
        name="proj_q" if with_q else "proj_ctx",
    )(x, mod, g_mix, wa, gcq, wq, gckv, wkv, *tabs)


HEADS_PER_STEP = 2


KEY_CHUNK = 2048


def _attn_kernel(q_ref, kx_ref, kc_ref, vx_ref, vc_ref, o_ref):
    nt = (((1,), (1,)), ((), ()))
    tq = q_ref.shape[1]
    chunks = [(kc_ref, vc_ref, 0, kc_ref.shape[1])]
    chunks += [(kx_ref, vx_ref, c * KEY_CHUNK, KEY_CHUNK) for c in range(kx_ref.shape[1] // KEY_CHUNK)]
    outs = []
    for hh in range(HEADS_PER_STEP):
        lo = hh * HEAD_PAD
        q = q_ref[0, :, lo:lo + HEAD_PAD]
        m_b = l_vec = acc = None
        for k_ref, v_ref, start, size in chunks:
            s = lax.dot_general(q, k_ref[0, start:start + size, lo:lo + HEAD_PAD], nt,
                                preferred_element_type=F32)
            cols = [s[:, i * LANES:(i + 1) * LANES] for i in range(size // LANES)]
            c_vec = functools.reduce(jnp.maximum, cols)
            c_max = jnp.broadcast_to(jnp.max(c_vec, axis=-1, keepdims=True), (tq, LANES))
            m_new = c_max if m_b is None else jnp.maximum(m_b, c_max)
            ps = [jnp.exp2(col - m_new) for col in cols]
            p_sum = functools.reduce(jnp.add, ps)
            pv = _dot(jnp.concatenate([p.astype(BF16) for p in ps], axis=-1), v_ref[0, start:start + size, :])
            if m_b is None:
                l_vec, acc = p_sum, pv
            else:
                alpha = jnp.exp2(m_b - m_new)
                l_vec = alpha * l_vec + p_sum
                acc = alpha * acc + pv
            m_b = m_new
        outs.append(acc / jnp.sum(l_vec, axis=-1, keepdims=True))
    lane = lax.broadcasted_iota(jnp.int32, outs[0].shape, 1)
    o_ref[0] = jnp.where(lane < V_HEAD, outs[0], outs[1])


def _attention(q, kx, kc, vx, vc, tq):
    nb, n, _ = q.shape
    nk = kx.shape[1]
    nc = kc.shape[1]
    pairs = MLA_HEADS // HEADS_PER_STEP
    qk_blk = HEADS_PER_STEP * HEAD_PAD
    v_blk = HEADS_PER_STEP * V_HEAD
    return pl.pallas_call(
        _attn_kernel,
        grid=(nb, pairs, n // tq),
        in_specs=[pl.BlockSpec((1, tq, qk_blk), lambda b, p, j: (b, j, p)),
                  pl.BlockSpec((1, nk, qk_blk), lambda b, p, j: (b, 0, p)),
                  pl.BlockSpec((1, nc, qk_blk), lambda b, p, j: (b, 0, p)),
                  pl.BlockSpec((1, nk, v_blk), lambda b, p, j: (b, 0, p)),
                  pl.BlockSpec((1, nc, v_blk), lambda b, p, j: (b, 0, p))],
        out_specs=pl.BlockSpec((1, tq, v_blk), lambda b, p, j: (b, j, p)),
        out_shape=jax.ShapeDtypeStruct((nb, n, MLA_WIDTH), F32),
        compiler_params=_params("parallel", "parallel", "parallel"),
        name="attention",
    )(q, kx, kc, vx, vc)


def _s5_kernel(uc_ref, ux_ref, lre_ref, lim_ref, ldt_ref, bre_ref, bim_ref, cre_ref, cim_ref,
               y_ref, bb_ref, cm_ref, a_ref, h_ref, bd_ref, *, t_chunk):
    d = pl.program_id(0)
    j = pl.program_id(1)

    @pl.when(j == 0)
    def _discretize():
        dt = jnp.exp(ldt_ref[0])
        lr = jnp.minimum(lre_ref[0], -1e-4)
        li = lim_ref[0]
        mag = jnp.exp(lr * dt)
        ar = mag * jnp.cos(li * dt)
        ai = mag * jnp.sin(li * dt)
        den = lr * lr + li * li
        fr = ((ar - 1.0) * lr + ai * li) / den
        fi = (ai * lr - (ar - 1.0) * li) / den
        bre = bre_ref[0]
        bim = bim_ref[0]
        bb_ref[:, 0:N_STATE] = (fr * bre - fi * bim).astype(BF16)
        bb_ref[:, N_STATE:2 * N_STATE] = (fr * bim + fi * bre).astype(BF16)
        cm_ref[0:N_STATE, :] = cre_ref[0].astype(BF16)
        cm_ref[N_STATE:2 * N_STATE, :] = (-cim_ref[0]).astype(BF16)
        a_ref[0:SUBLANES, :] = jnp.broadcast_to(ar, (SUBLANES, N_STATE))
        a_ref[SUBLANES:2 * SUBLANES, :] = jnp.broadcast_to(ai, (SUBLANES, N_STATE))
        h_ref[...] = jnp.zeros_like(h_ref)

    def run(u_ref, readout):
        bd_ref[...] = _dot(u_ref[...].astype(BF16), bb_ref[...])
        ar = a_ref[0:SUBLANES, :]
        ai = a_ref[SUBLANES:2 * SUBLANES, :]

        def step(i, carry):
            hr, hi = carry
            tt = jnp.where(d == 0, i, t_chunk - 1 - i)
            r = pl.multiple_of(tt * SUBLANES, SUBLANES)
            br = bd_ref[pl.ds(r, SUBLANES), 0:N_STATE]
            bi = bd_ref[pl.ds(r, SUBLANES), N_STATE:2 * N_STATE]
            nr = ar * hr - ai * hi + br
            ni = ar * hi + ai * hr + bi
            bd_ref[pl.ds(r, SUBLANES), 0:N_STATE] = nr
            bd_ref[pl.ds(r, SUBLANES), N_STATE:2 * N_STATE] = ni
            return nr, ni

        hr, hi = lax.fori_loop(0, t_chunk, step,
                               (h_ref[0:SUBLANES, :], h_ref[SUBLANES:2 * SUBLANES, :]), unroll=8)
        h_ref[0:SUBLANES, :] = hr
        h_ref[SUBLANES:2 * SUBLANES, :] = hi
        if readout:
            y_ref[0] = _dot(bd_ref[...].astype(BF16), cm_ref[...])

    @pl.when(j == 0)
    def _context():
        run(uc_ref, False)

    @pl.when(j > 0)
    def _latent():
        run(ux_ref, True)


def _s5(u_c, u_x, lre, lim, ldt, bre, bim, cre, cim, t_chunk):
    rows = t_chunk * SUBLANES
    assert u_c.shape[0] == rows
    nch = u_x.shape[0] // rows

    def chunk(d, j):
        jj = jnp.maximum(j - 1, 0)
        return jnp.where(d == 0, jj, nch - 1 - jj)

    vec_spec = pl.BlockSpec((1, 1, N_STATE), lambda d, j: (d, 0, 0))
    return pl.pallas_call(
        functools.partial(_s5_kernel, t_chunk=t_chunk),
        grid=(2, nch + 1),
        in_specs=[pl.BlockSpec((rows, SSM_WIDTH), lambda d, j: (0, 0)),
                  pl.BlockSpec((rows, SSM_WIDTH), lambda d, j: (chunk(d, j), 0)),
                  vec_spec, vec_spec, vec_spec,
                  pl.BlockSpec((1, SSM_WIDTH, N_STATE), lambda d, j: (d, 0, 0)),
                  pl.BlockSpec((1, SSM_WIDTH, N_STATE), lambda d, j: (d, 0, 0)),
                  pl.BlockSpec((1, N_STATE, SSM_WIDTH), lambda d, j: (d, 0, 0)),
                  pl.BlockSpec((1, N_STATE, SSM_WIDTH), lambda d, j: (d, 0, 0))],
        out_specs=pl.BlockSpec((1, rows, SSM_WIDTH), lambda d, j: (d, chunk(d, j), 0)),
        out_shape=jax.ShapeDtypeStruct((2, u_x.shape[0], SSM_WIDTH), F32),
        scratch_shapes=[pltpu.VMEM((SSM_WIDTH, 2 * N_STATE), BF16),
                        pltpu.VMEM((2 * N_STATE, SSM_WIDTH), BF16),
                        pltpu.VMEM((2 * SUBLANES, N_STATE), F32),
                        pltpu.VMEM((2 * SUBLANES, N_STATE), F32),
                        pltpu.VMEM((rows, 2 * N_STATE), F32)],
        compiler_params=_params("arbitrary", "arbitrary"),
        name="s5_scan",
    )(u_c, u_x, lre, lim, ldt, bre, bim, cre, cim)


def _gelu_tanh(x):
    return 0.5 * x * (1.0 + jnp.tanh(math.sqrt(2.0 / math.pi) * (x + 0.044715 * (x * x * x))))


def _final_kernel(x_ref, attn_ref, u_ref, y_ref, mod_ref, dsk_ref, wglu_ref, gmla_ref, gssm_ref,
                  woa_ref, wos_ref, g2_ref, wgu_ref, wd_ref, gf_ref, o_ref):
    x = x_ref[0]
    m = mod_ref[0]
    y = u_ref[...] * dsk_ref[...] + y_ref[0] + y_ref[1]
    z = _gelu_tanh(y).astype(BF16)
    ag = _dot(z, wglu_ref[...])
    ssm = ag[:, 0:SSM_WIDTH] * jax.nn.sigmoid(ag[:, SSM_WIDTH:2 * SSM_WIDTH])
    a_n = _rms(attn_ref[0], gmla_ref[...]).astype(BF16)
    s_n = _rms(ssm, gssm_ref[...]).astype(BF16)
    mix = _dot(a_n, woa_ref[...]) + _dot(s_n, wos_ref[...])
    x = x + m[5:6] * mix
    h = (_rms(x, g2_ref[...]) * (1.0 + m[7:8]) + m[6:7]).astype(BF16)
    x = x + 0.5 * m[8:9] * _swiglu(h, wgu_ref, wd_ref)
    o_ref[0] = _rms(x, gf_ref[...])


def _final(x, attn, u, y, mod, dsk, wglu, gmla, gssm, woa, wos, g2, wgu, wd, gf, tm):
    nb, n, _ = x.shape
    return pl.pallas_call(
        _final_kernel,
        grid=(nb, n // tm),
        in_specs=[pl.BlockSpec((1, tm, D_MODEL), lambda b, j: (b, j, 0)),
                  pl.BlockSpec((1, tm, MLA_WIDTH), lambda b, j: (b, j, 0)),
                  pl.BlockSpec((tm, SSM_WIDTH), lambda b, j: (j, b)),
                  pl.BlockSpec((2, tm, SSM_WIDTH), lambda b, j: (0, j, b)),
                  pl.BlockSpec((1, N_MOD, D_MODEL), lambda b, j: (b, 0, 0)),
                  _const_spec((1, SSM_WIDTH)),
                  _const_spec((SSM_WIDTH, 2 * SSM_WIDTH)),
                  _const_spec((1, MLA_WIDTH)),
                  _const_spec((1, SSM_WIDTH)),
                  _const_spec((MLA_WIDTH, D_MODEL)),
                  _const_spec((SSM_WIDTH, D_MODEL)),
                  _const_spec((1, D_MODEL)),
                  _const_spec((D_MODEL, 2 * D_FF)),
                  _const_spec((D_FF, D_MODEL)),
                  _const_spec((1, D_MODEL))],
        out_specs=pl.BlockSpec((1, tm, D_MODEL), lambda b, j: (b, j, 0)),
        out_shape=jax.ShapeDtypeStruct(x.shape, F32),
        compiler_params=_params("parallel", "parallel"),
        name="final",
    )(x, attn, u, y, mod, dsk, wglu, gmla, gssm, woa, wos, g2, wgu, wd, gf)


def _rot_cols(w):
    wp = w.reshape(w.shape[0], -1, 2)
    return jnp.stack([-wp[..., 1], wp[..., 0]], axis=-1).reshape(w.shape)


def _rope_tables(n_tokens):
    rows = n_tokens // GRID_W
    row = jnp.repeat(jnp.arange(rows), GRID_W).astype(F32)
    col = jnp.tile(jnp.arange(GRID_W), rows).astype(F32)
    per_axis = QK_ROPE // 2
    inv_freq = ROPE_BASE ** (-jnp.arange(0, per_axis, 2, dtype=F32) / per_axis)
    ang = jnp.concatenate([row[:, None] * inv_freq, col[:, None] * inv_freq], axis=-1)
    cos = jnp.repeat(jnp.cos(ang), 2, axis=-1)
    sin = jnp.repeat(jnp.sin(ang), 2, axis=-1)
    return cos, sin


def _head_tables(cos, sin, nope_fill, scale):
    n = cos.shape[0]
    pad = HEAD_PAD - QK_NOPE - QK_ROPE
    c = jnp.concatenate([jnp.full((n, QK_NOPE), nope_fill, F32), cos, jnp.zeros((n, pad), F32)], axis=-1)
    s = jnp.concatenate([jnp.zeros((n, QK_NOPE), F32), sin, jnp.zeros((n, pad), F32)], axis=-1)
    return c * scale, s * scale


def _pad_head_cols(w_nope, w_rope):
    k, h = w_nope.shape[:2]
    pad = HEAD_PAD - QK_NOPE - QK_ROPE
    return jnp.concatenate([w_nope, w_rope, jnp.zeros((k, h, pad), w_nope.dtype)], axis=-1).reshape(k, h * HEAD_PAD)


def kernel(x, c, ctx, c_ctx, w_mod, b_mod, g_ffn1, w_gu1, w_down1, g_mix, w_in, g_cq, w_uq, g_ckv, w_ukv,
           lam_re, lam_im, log_dt, b_re, b_im, c_re, c_im, d_skip, w_glu, g_mla_out, g_ssm_out, w_out,
           g_ffn2, w_gu2, w_down2, g_final):
    bsz, seq, _ = x.shape
    n_ctx = ctx.shape[1]
    assert w_mod.shape[0] == 1, "single-layer block"
    l = 0

    rows = 2 * SUBLANES
    cvec = jnp.concatenate([c, c_ctx[None], jnp.zeros((rows - bsz - 1, D_MODEL), F32)], axis=0)
    mod = _adaln(cvec, w_mod[l], b_mod[l][None]).reshape(rows, N_MOD, D_MODEL)
    ctx_row = bsz

    wgu1 = w_gu1[l].astype(BF16)
    wd1 = w_down1[l].astype(BF16)
    x1 = _ffn(x, mod, g_ffn1[l][None], wgu1, wd1, None, 512)
    ctx1 = _ffn(ctx.reshape(1, bsz * n_ctx, D_MODEL), mod, g_ffn1[l][None], wgu1, wd1, ctx_row, 512)
    ctx1 = ctx1.reshape(bsz, n_ctx, D_MODEL)

    win = w_in[l]
    o_kv = Q_RANK
    o_kr = o_kv + KV_RANK
    o_u = o_kr + QK_ROPE
    w_kr = win[:, o_kr:o_kr + QK_ROPE]
    zk = jnp.zeros((D_MODEL, QK_NOPE), F32)
    zp = jnp.zeros((D_MODEL, HEAD_PAD - QK_NOPE - QK_ROPE), F32)
    wa = jnp.concatenate([win[:, 0:o_kr], zk, w_kr, zp, zk, _rot_cols(w_kr), zp, win[:, o_u:]], axis=-1).astype(BF16)
    wuq = w_uq[l].reshape(Q_RANK, MLA_HEADS, QK_NOPE + QK_ROPE)
    wq_nope, wq_rope = wuq[..., :QK_NOPE], wuq[..., QK_NOPE:]
    wq_rot = _rot_cols(wq_rope.reshape(Q_RANK, -1)).reshape(wq_rope.shape)
    wq = jnp.concatenate([_pad_head_cols(wq_nope, wq_rope),
                          _pad_head_cols(jnp.zeros_like(wq_nope), wq_rot)], axis=-1).astype(BF16)
    wukv = w_ukv[l].reshape(KV_RANK, MLA_HEADS, QK_NOPE + V_HEAD)
    wk = _pad_head_cols(wukv[..., :QK_NOPE], jnp.zeros((KV_RANK, MLA_HEADS, QK_ROPE), F32))
    wv = wukv[..., QK_NOPE:].reshape(KV_RANK, MLA_WIDTH)
    wkv = jnp.concatenate([wk, wv], axis=-1).astype(BF16)

    cos, sin = _rope_tables(seq)
    cq_t, sq_t = _head_tables(cos, sin, 1.0, ATTN_SCALE * math.log2(math.e))
    ck_t, sk_t = _head_tables(cos, sin, 0.0, 1.0)
    one = jnp.ones((n_ctx, QK_ROPE), F32)
    ckc_t, skc_t = _head_tables(one, jnp.zeros_like(one), 0.0, 1.0)

    gmix = g_mix[l][None]
    q, kx, vx, ux = _proj(x1, mod, gmix, wa, g_cq[l][None], wq, g_ckv[l][None], wkv,
                          (cq_t, sq_t, ck_t, sk_t), None, 512, True)
    kc, vc, uc = _proj(ctx1, mod, gmix, wa, g_cq[l][None], wq, g_ckv[l][None], wkv,
                       (ckc_t, skc_t, ckc_t, skc_t), ctx_row, n_ctx, False)

    attn = _attention(q, kx, kc, vx, vc, 512)

    eye = jnp.eye(SSM_GROUPS, dtype=F32)
    lre = lam_re[l].reshape(2, 1, N_STATE)
    lim = lam_im[l].reshape(2, 1, N_STATE)
    ldt = jnp.repeat(log_dt[l], SSM_STATE, axis=-1).reshape(2, 1, N_STATE)
    bre = jnp.einsum('dgpc,gh->dgchp', b_re[l], eye).reshape(2, SSM_WIDTH, N_STATE)
    bim = jnp.einsum('dgpc,gh->dgchp', b_im[l], eye).reshape(2, SSM_WIDTH, N_STATE)
    cre = jnp.einsum('dgcp,gh->dgphc', c_re[l], eye).reshape(2, N_STATE, SSM_WIDTH)
    cim = jnp.einsum('dgcp,gh->dgphc', c_im[l], eye).reshape(2, N_STATE, SSM_WIDTH)
    y = _s5(uc.reshape(n_ctx * bsz, SSM_WIDTH), ux.reshape(seq * bsz, SSM_WIDTH),
            lre, lim, ldt, bre, bim, cre, cim, n_ctx)
    y = y.reshape(2, seq, bsz * SSM_WIDTH)

    wo = w_out[l].astype(BF16)
    return _final(x1, attn, ux, y, mod, d_skip[l][None], w_glu[l].astype(BF16), g_mla_out[l][None],
                  g_ssm_out[l][None], wo[:MLA_WIDTH], wo[MLA_WIDTH:], g_ffn2[l][None],
                  w_gu2[l].astype(BF16), w_down2[l].astype(BF16), g_final[None], 512)
```

```python
import functools
import math

import jax
import jax.numpy as jnp
from jax import lax
from jax.experimental import pallas as pl
from jax.experimental.pallas import tpu as pltpu

D_MODEL = 1024
GRID_W = 64
N_MOD = 9
D_FF = 2816
MLA_HEADS = 12
QK_NOPE = 64
QK_ROPE = 32
V_HEAD = 64
Q_RANK = 256
KV_RANK = 128
MLA_WIDTH = MLA_HEADS * V_HEAD
SSM_GROUPS = 16
SSM_CH = 16
SSM_WIDTH = SSM_GROUPS * SSM_CH
SSM_STATE = 64
N_STATE = SSM_GROUPS * SSM_STATE
ROPE_BASE = 10000.0
ATTN_SCALE = (QK_NOPE + QK_ROPE) ** -0.5
EPS = 1e-6

LANES = 128
SUBLANES = 8
HEAD_PAD = 128
VMEM_LIMIT = 56 * 1024 * 1024

F32 = jnp.float32
BF16 = jnp.bfloat16


def _rms(x, g):
    ms = jnp.mean(x * x, axis=-1, keepdims=True)
    return x * lax.rsqrt(ms + EPS) * g


def _dot(a, b):
    return jnp.dot(a, b, preferred_element_type=F32)


def _const_spec(shape):
    zeros = (0,) * len(shape)
    return pl.BlockSpec(shape, lambda *_: zeros, pipeline_mode=pl.Buffered(1))


def _params(*sem):
    return pltpu.CompilerParams(dimension_semantics=sem, vmem_limit_bytes=VMEM_LIMIT)


def _adaln_kernel(c_ref, w_ref, b_ref, o_ref):
    c = c_ref[...]
    s = (c * jax.nn.sigmoid(c)).astype(BF16)
    o_ref[...] = _dot(s, w_ref[...].astype(BF16)) + b_ref[...]


def _adaln(cvec, w_mod, b_mod):
    rows = cvec.shape[0]
    n = w_mod.shape[1]
    bn = D_MODEL
    return pl.pallas_call(
        _adaln_kernel,
        grid=(n // bn,),
        in_specs=[pl.BlockSpec((rows, D_MODEL), lambda i: (0, 0)),
                  pl.BlockSpec((D_MODEL, bn), lambda i: (0, i)),
                  pl.BlockSpec((1, bn), lambda i: (0, i))],
        out_specs=pl.BlockSpec((rows, bn), lambda i: (0, i)),
        out_shape=jax.ShapeDtypeStruct((rows, n), F32),
        compiler_params=_params("parallel"),
        name="adaln",
    )(cvec, w_mod, b_mod)


MXU_DIM = 256
FF_TILES = D_FF // MXU_DIM
FF_SPLITS = (0, (FF_TILES + 1) // 2 * MXU_DIM, D_FF)


def _swiglu(h, wgu_ref, wd_ref):
    acc = None
    for lo, hi in zip(FF_SPLITS[:-1], FF_SPLITS[1:]):
        gate = _dot(h, wgu_ref[:, lo:hi])
        up = _dot(h, wgu_ref[:, D_FF + lo:D_FF + hi])
        act = (gate * jax.nn.sigmoid(gate) * up).astype(BF16)
        part = _dot(act, wd_ref[lo:hi, :])
        acc = part if acc is None else acc + part
    return acc


def _ffn_kernel(x_ref, mod_ref, g_ref, wgu_ref, wd_ref, o_ref):
    x = x_ref[0]
    m = mod_ref[0]
    h = (_rms(x, g_ref[...]) * (1.0 + m[1:2]) + m[0:1]).astype(BF16)
    o_ref[0] = x + 0.5 * m[2:3] * _swiglu(h, wgu_ref, wd_ref)


def _ffn(x, mod, g, wgu, wd, mod_row, tm):
    nb, n, _ = x.shape
    mod_map = (lambda b, j: (b, 0, 0)) if mod_row is None else (lambda b, j: (mod_row, 0, 0))
    return pl.pallas_call(
        _ffn_kernel,
        grid=(nb, n // tm),
        in_specs=[pl.BlockSpec((1, tm, D_MODEL), lambda b, j: (b, j, 0)),
                  pl.BlockSpec((1, N_MOD, D_MODEL), mod_map),
                  _const_spec((1, D_MODEL)),
                  _const_spec((D_MODEL, 2 * D_FF)),
                  _const_spec((D_FF, D_MODEL))],
        out_specs=pl.BlockSpec((1, tm, D_MODEL), lambda b, j: (b, j, 0)),
        out_shape=jax.ShapeDtypeStruct(x.shape, F32),
        compiler_params=_params("parallel", "parallel"),
        name="ffn",
    )(x, mod, g, wgu, wd)


W_A_COLS = Q_RANK + KV_RANK + 2 * HEAD_PAD + SSM_WIDTH
QK_WIDTH = MLA_HEADS * HEAD_PAD


def _proj_kernel(x_ref, mod_ref, g_ref, wa_ref, gcq_ref, wq_ref, gckv_ref, wkv_ref,
                 cq_ref, sq_ref, ck_ref, sk_ref, *out_refs, with_q):
    if with_q:
        q_ref, k_ref, v_ref, u_ref = out_refs
    else:
        k_ref, v_ref, u_ref = out_refs
    x = x_ref[0]
    m = mod_ref[0]
    h = (_rms(x, g_ref[...]) * (1.0 + m[4:5]) + m[3:4]).astype(BF16)
    p = _dot(h, wa_ref[...])
    o_kv = Q_RANK
    o_kr = o_kv + KV_RANK
    o_rot = o_kr + HEAD_PAD
    o_u = o_rot + HEAD_PAD
    u_ref[...] = p[:, o_u:o_u + SSM_WIDTH]
    kr = p[:, o_kr:o_kr + HEAD_PAD] * ck_ref[...] + p[:, o_rot:o_rot + HEAD_PAD] * sk_ref[...]
    ckv = _rms(p[:, o_kv:o_kv + KV_RANK], gckv_ref[...]).astype(BF16)
    kv = _dot(ckv, wkv_ref[...])
    for hd in range(MLA_HEADS):
        lo = hd * HEAD_PAD
        k_ref[0, :, lo:lo + HEAD_PAD] = (kv[:, lo:lo + HEAD_PAD] + kr).astype(BF16)
    v_ref[0] = kv[:, QK_WIDTH:QK_WIDTH + MLA_WIDTH].astype(BF16)
    if with_q:
        cq = _rms(p[:, 0:Q_RANK], gcq_ref[...]).astype(BF16)
        qa = _dot(cq, wq_ref[:, 0:QK_WIDTH])
        qb = _dot(cq, wq_ref[:, QK_WIDTH:2 * QK_WIDTH])
        cq_t = cq_ref[...]
        sq_t = sq_ref[...]
        for hd in range(MLA_HEADS):
            lo = hd * HEAD_PAD
            q_ref[0, :, lo:lo + HEAD_PAD] = (qa[:, lo:lo + HEAD_PAD] * cq_t
                                             + qb[:, lo:lo + HEAD_PAD] * sq_t).astype(BF16)


def _proj(x, mod, g_mix, wa, gcq, wq, gckv, wkv, tabs, mod_row, tm, with_q):
    nb, n, _ = x.shape
    mod_map = (lambda b, j: (b, 0, 0)) if mod_row is None else (lambda b, j: (mod_row, 0, 0))
    tab_spec = pl.BlockSpec((tm, HEAD_PAD), lambda b, j: (j, 0))
    out_shape = [jax.ShapeDtypeStruct((nb, n, QK_WIDTH), BF16),
                 jax.ShapeDtypeStruct((nb, n, MLA_WIDTH), BF16),
                 jax.ShapeDtypeStruct((n, nb * SSM_WIDTH), F32)]
    out_specs = [pl.BlockSpec((1, tm, QK_WIDTH), lambda b, j: (b, j, 0)),
                 pl.BlockSpec((1, tm, MLA_WIDTH), lambda b, j: (b, j, 0)),
                 pl.BlockSpec((tm, SSM_WIDTH), lambda b, j: (j, b))]
    if with_q:
        out_shape = [jax.ShapeDtypeStruct((nb, n, QK_WIDTH), BF16)] + out_shape
        out_specs = [pl.BlockSpec((1, tm, QK_WIDTH), lambda b, j: (b, j, 0))] + out_specs
    return pl.pallas_call(
        functools.partial(_proj_kernel, with_q=with_q),
        grid=(nb, n // tm),
        in_specs=[pl.BlockSpec((1, tm, D_MODEL), lambda b, j: (b, j, 0)),
                  pl.BlockSpec((1, N_MOD, D_MODEL), mod_map),
                  _const_spec((1, D_MODEL)),
                  _const_spec(wa.shape),
                  _const_spec((1, Q_RANK)),
                  _const_spec(wq.shape),
                  _const_spec((1, KV_RANK)),
                  _const_spec(wkv.shape),
                  tab_spec, tab_spec, tab_spec, tab_spec],
        out_specs=out_specs,
        out_shape=out_shape,
        compiler_params=_params("parallel", "parallel"),
        name="proj_q" if with_q else "proj_ctx",
    )(x, mod, g_mix, wa, gcq, wq, gckv, wkv, *tabs)


HEADS_PER_STEP = 2


KEY_CHUNK = 2048


def _attn_kernel(q_ref, kx_ref, kc_ref, vx_ref, vc_ref, o_ref):
    nt = (((1,), (1,)), ((), ()))
    tq = q_ref.shape[1]
    chunks = [(kc_ref, vc_ref, 0, kc_ref.shape[1])]
    chunks += [(kx_ref, vx_ref, c * KEY_CHUNK, KEY_CHUNK) for c in range(kx_ref.shape[1] // KEY_CHUNK)]
    outs = []
    for hh in range(HEADS_PER_STEP):
        lo = hh * HEAD_PAD
        q = q_ref[0, :, lo:lo + HEAD_PAD]
        m_b = l_vec = acc = None
        for k_ref, v_ref, start, size in chunks:
            s = lax.dot_general(q, k_ref[0, start:start + size, lo:lo + HEAD_PAD], nt,
                                preferred_element_type=F32)
            cols = [s[:, i * LANES:(i + 1) * LANES] for i in range(size // LANES)]
            c_vec = functools.reduce(jnp.maximum, cols)
            c_max = jnp.broadcast_to(jnp.max(c_vec, axis=-1, keepdims=True), (tq, LANES))
            m_new = c_max if m_b is None else jnp.maximum(m_b, c_max)
            ps = [jnp.exp2(col - m_new) for col in cols]
            p_sum = functools.reduce(jnp.add, ps)
            v_lo = (hh // 2) * 2 * V_HEAD
            pv = _dot(jnp.concatenate([p.astype(BF16) for p in ps], axis=-1),
                      v_ref[0, start:start + size, v_lo:v_lo + 2 * V_HEAD])
            if m_b is None:
                l_vec, acc = p_sum, pv
            else:
                alpha = jnp.exp2(m_b - m_new)
                l_vec = alpha * l_vec + p_sum
                acc = alpha * acc + pv
            m_b = m_new
        outs.append(acc / jnp.sum(l_vec, axis=-1, keepdims=True))
    lane = lax.broadcasted_iota(jnp.int32, outs[0].shape, 1)
    for pr in range(HEADS_PER_STEP // 2):
        o_ref[0, :, pr * 2 * V_HEAD:(pr + 1) * 2 * V_HEAD] = jnp.where(lane < V_HEAD, outs[2 * pr], outs[2 * pr + 1])


def _attention(q, kx, kc, vx, vc, tq):
    nb, n, _ = q.shape
    nk = kx.shape[1]
    nc = kc.shape[1]
    pairs = MLA_HEADS // HEADS_PER_STEP
    qk_blk = HEADS_PER_STEP * HEAD_PAD
    v_blk = HEADS_PER_STEP * V_HEAD
    return pl.pallas_call(
        _attn_kernel,
        grid=(nb, pairs, n // tq),
        in_specs=[pl.BlockSpec((1, tq, qk_blk), lambda b, p, j: (b, j, p)),
                  pl.BlockSpec((1, nk, qk_blk), lambda b, p, j: (b, 0, p)),
                  pl.BlockSpec((1, nc, qk_blk), lambda b, p, j: (b, 0, p)),
                  pl.BlockSpec((1, nk, v_blk), lambda b, p, j: (b, 0, p)),
                  pl.BlockSpec((1, nc, v_blk), lambda b, p, j: (b, 0, p))],
        out_specs=pl.BlockSpec((1, tq, v_blk), lambda b, p, j: (b, j, p)),
        out_shape=jax.ShapeDtypeStruct((nb, n, MLA_WIDTH), F32),
        compiler_params=_params("parallel", "parallel", "parallel"),
        name="attention",
    )(q, kx, kc, vx, vc)


def _s5_kernel(uc_ref, ux_ref, lre_ref, lim_ref, ldt_ref, bre_ref, bim_ref, cre_ref, cim_ref,
               y_ref, bb_ref, cm_ref, a_ref, h_ref, bd_ref, *, t_chunk):
    d = pl.program_id(0)
    j = pl.program_id(1)

    @pl.when(j == 0)
    def _discretize():
        dt = jnp.exp(ldt_ref[0])
        lr = jnp.minimum(lre_ref[0], -1e-4)
        li = lim_ref[0]
        mag = jnp.exp(lr * dt)
        ar = mag * jnp.cos(li * dt)
        ai = mag * jnp.sin(li * dt)
        den = lr * lr + li * li
        fr = ((ar - 1.0) * lr + ai * li) / den
        fi = (ai * lr - (ar - 1.0) * li) / den
        bre = bre_ref[0]
        bim = bim_ref[0]
        bb_ref[:, 0:N_STATE] = (fr * bre - fi * bim).astype(BF16)
        bb_ref[:, N_STATE:2 * N_STATE] = (fr * bim + fi * bre).astype(BF16)
        cm_ref[0:N_STATE, :] = cre_ref[0].astype(BF16)
        cm_ref[N_STATE:2 * N_STATE, :] = (-cim_ref[0]).astype(BF16)
        a_ref[0:SUBLANES, :] = jnp.broadcast_to(ar, (SUBLANES, N_STATE))
        a_ref[SUBLANES:2 * SUBLANES, :] = jnp.broadcast_to(ai, (SUBLANES, N_STATE))
        h_ref[...] = jnp.zeros_like(h_ref)

    def run(u_ref, readout):
        u = u_ref[...].reshape(t_chunk * SUBLANES, SSM_WIDTH)
        bd_ref[...] = _dot(u.astype(BF16), bb_ref[...])
        ar = a_ref[0:SUBLANES, :]
        ai = a_ref[SUBLANES:2 * SUBLANES, :]

        def step(i, carry):
            hr, hi = carry
            tt = jnp.where(d == 0, i, t_chunk - 1 - i)
            r = pl.multiple_of(tt * SUBLANES, SUBLANES)
            br = bd_ref[pl.ds(r, SUBLANES), 0:N_STATE]
            bi = bd_ref[pl.ds(r, SUBLANES), N_STATE:2 * N_STATE]
            nr = ar * hr - ai * hi + br
            ni = ar * hi + ai * hr + bi
            bd_ref[pl.ds(r, SUBLANES), 0:N_STATE] = nr
            bd_ref[pl.ds(r, SUBLANES), N_STATE:2 * N_STATE] = ni
            return nr, ni

        hr, hi = lax.fori_loop(0, t_chunk, step,
                               (h_ref[0:SUBLANES, :], h_ref[SUBLANES:2 * SUBLANES, :]), unroll=8)
        h_ref[0:SUBLANES, :] = hr
        h_ref[SUBLANES:2 * SUBLANES, :] = hi
        if readout:
            y = _dot(bd_ref[...].astype(BF16), cm_ref[...])
            y_ref[0] = y.reshape(t_chunk, SUBLANES * SSM_WIDTH)

    @pl.when(j == 0)
    def _context():
        run(uc_ref, False)

    @pl.when(j > 0)
    def _latent():
        run(ux_ref, True)


def _s5(u_c, u_x, lre, lim, ldt, bre, bim, cre, cim, t_chunk):
    rows = t_chunk * SUBLANES
    width = SUBLANES * SSM_WIDTH
    assert u_c.shape == (t_chunk, width) and u_x.shape[1] == width
    nch = u_x.shape[0] // t_chunk

    def chunk(d, j):
        jj = jnp.maximum(j - 1, 0)
        return jnp.where(d == 0, jj, nch - 1 - jj)

    vec_spec = pl.BlockSpec((1, 1, N_STATE), lambda d, j: (d, 0, 0))
    return pl.pallas_call(
        functools.partial(_s5_kernel, t_chunk=t_chunk),
        grid=(2, nch + 1),
        in_specs=[pl.BlockSpec((t_chunk, width), lambda d, j: (0, 0)),
                  pl.BlockSpec((t_chunk, width), lambda d, j: (chunk(d, j), 0)),
                  vec_spec, vec_spec, vec_spec,
                  pl.BlockSpec((1, SSM_WIDTH, N_STATE), lambda d, j: (d, 0, 0)),
                  pl.BlockSpec((1, SSM_WIDTH, N_STATE), lambda d, j: (d, 0, 0)),
                  pl.BlockSpec((1, N_STATE, SSM_WIDTH), lambda d, j: (d, 0, 0)),
                  pl.BlockSpec((1, N_STATE, SSM_WIDTH), lambda d, j: (d, 0, 0))],
        out_specs=pl.BlockSpec((1, t_chunk, width), lambda d, j: (d, chunk(d, j), 0)),
        out_shape=jax.ShapeDtypeStruct((2, u_x.shape[0], width), F32),
        scratch_shapes=[pltpu.VMEM((SSM_WIDTH, 2 * N_STATE), BF16),
                        pltpu.VMEM((2 * N_STATE, SSM_WIDTH), BF16),
                        pltpu.VMEM((2 * SUBLANES, N_STATE), F32),
                        pltpu.VMEM((2 * SUBLANES, N_STATE), F32),
                        pltpu.VMEM((rows, 2 * N_STATE), F32)],
        compiler_params=_params("arbitrary", "arbitrary"),
        name="s5_scan",
    )(u_c, u_x, lre, lim, ldt, bre, bim, cre, cim)


def _gelu_tanh(x):
    return 0.5 * x * (1.0 + jnp.tanh(math.sqrt(2.0 / math.pi) * (x + 0.044715 * (x * x * x))))


def _final_kernel(x_ref, attn_ref, u_ref, y_ref, mod_ref, dsk_ref, wglu_ref, gmla_ref, gssm_ref,
                  woa_ref, wos_ref, g2_ref, wgu_ref, wd_ref, gf_ref, o_ref):
    x = x_ref[0]
    m = mod_ref[0]
    y = u_ref[...] * dsk_ref[...] + y_ref[0] + y_ref[1]
    z = _gelu_tanh(y).astype(BF16)
    ag = _dot(z, wglu_ref[...])
    ssm = ag[:, 0:SSM_WIDTH] * jax.nn.sigmoid(ag[:, SSM_WIDTH:2 * SSM_WIDTH])
    a_n = _rms(attn_ref[0], gmla_ref[...]).astype(BF16)
    s_n = _rms(ssm, gssm_ref[...]).astype(BF16)
    mix = _dot(a_n, woa_ref[...]) + _dot(s_n, wos_ref[...])
    x = x + m[5:6] * mix
    h = (_rms(x, g2_ref[...]) * (1.0 + m[7:8]) + m[6:7]).astype(BF16)
    x = x + 0.5 * m[8:9] * _swiglu(h, wgu_ref, wd_ref)
    o_ref[0] = _rms(x, gf_ref[...])


def _final(x, attn, u, y, mod, dsk, wglu, gmla, gssm, woa, wos, g2, wgu, wd, gf, tm):
    nb, n, _ = x.shape
    return pl.pallas_call(
        _final_kernel,
        grid=(nb, n // tm),
        in_specs=[pl.BlockSpec((1, tm, D_MODEL), lambda b, j: (b, j, 0)),
                  pl.BlockSpec((1, tm, MLA_WIDTH), lambda b, j: (b, j, 0)),
                  pl.BlockSpec((tm, SSM_WIDTH), lambda b, j: (j, b)),
                  pl.BlockSpec((2, tm, SSM_WIDTH), lambda b, j: (0, j, b)),
                  pl.BlockSpec((1, N_MOD, D_MODEL), lambda b, j: (b, 0, 0)),
                  _const_spec((1, SSM_WIDTH)),
                  _const_spec((SSM_WIDTH, 2 * SSM_WIDTH)),
                  _const_spec((1, MLA_WIDTH)),
                  _const_spec((1, SSM_WIDTH)),
                  _const_spec((MLA_WIDTH, D_MODEL)),
                  _const_spec((SSM_WIDTH, D_MODEL)),
                  _const_spec((1, D_MODEL)),
                  _const_spec((D_MODEL, 2 * D_FF)),
                  _const_spec((D_FF, D_MODEL)),
                  _const_spec((1, D_MODEL))],
        out_specs=pl.BlockSpec((1, tm, D_MODEL), lambda b, j: (b, j, 0)),
        out_shape=jax.ShapeDtypeStruct(x.shape, F32),
        compiler_params=_params("parallel", "parallel"),
        name="final",
    )(x, attn, u, y, mod, dsk, wglu, gmla, gssm, woa, wos, g2, wgu, wd, gf)


def _rot_cols(w):
    wp = w.reshape(w.shape[0], -1, 2)
    return jnp.stack([-wp[..., 1], wp[..., 0]], axis=-1).reshape(w.shape)


def _rope_tables(n_tokens):
    rows = n_tokens // GRID_W
    row = jnp.repeat(jnp.arange(rows), GRID_W).astype(F32)
    col = jnp.tile(jnp.arange(GRID_W), rows).astype(F32)
    per_axis = QK_ROPE // 2
    inv_freq = ROPE_BASE ** (-jnp.arange(0, per_axis, 2, dtype=F32) / per_axis)
    ang = jnp.concatenate([row[:, None] * inv_freq, col[:, None] * inv_freq], axis=-1)
    cos = jnp.repeat(jnp.cos(ang), 2, axis=-1)
    sin = jnp.repeat(jnp.sin(ang), 2, axis=-1)
    return cos, sin


def _head_tables(cos, sin, nope_fill, scale):
    n = cos.shape[0]
    pad = HEAD_PAD - QK_NOPE - QK_ROPE
    c = jnp.concatenate([jnp.full((n, QK_NOPE), nope_fill, F32), cos, jnp.zeros((n, pad), F32)], axis=-1)
    s = jnp.concatenate([jnp.zeros((n, QK_NOPE), F32), sin, jnp.zeros((n, pad), F32)], axis=-1)
    return c * scale, s * scale


def _pad_head_cols(w_nope, w_rope):
    k, h = w_nope.shape[:2]
    pad = HEAD_PAD - QK_NOPE - QK_ROPE
    return jnp.concatenate([w_nope, w_rope, jnp.zeros((k, h, pad), w_nope.dtype)], axis=-1).reshape(k, h * HEAD_PAD)


def kernel(x, c, ctx, c_ctx, w_mod, b_mod, g_ffn1, w_gu1, w_down1, g_mix, w_in, g_cq, w_uq, g_ckv, w_ukv,
           lam_re, lam_im, log_dt, b_re, b_im, c_re, c_im, d_skip, w_glu, g_mla_out, g_ssm_out, w_out,
           g_ffn2, w_gu2, w_down2, g_final):
    bsz, seq, _ = x.shape
    n_ctx = ctx.shape[1]
    assert w_mod.shape[0] == 1, "single-layer block"
    l = 0

    rows = 2 * SUBLANES
    cvec = jnp.concatenate([c, c_ctx[None], jnp.zeros((rows - bsz - 1, D_MODEL), F32)], axis=0)
    mod = _adaln(cvec, w_mod[l], b_mod[l][None]).reshape(rows, N_MOD, D_MODEL)
    ctx_row = bsz

    wgu1 = w_gu1[l].astype(BF16)
    wd1 = w_down1[l].astype(BF16)
    x1 = _ffn(x, mod, g_ffn1[l][None], wgu1, wd1, None, 1024)
    ctx1 = _ffn(ctx.reshape(1, bsz * n_ctx, D_MODEL), mod, g_ffn1[l][None], wgu1, wd1, ctx_row, 512)
    ctx1 = ctx1.reshape(bsz, n_ctx, D_MODEL)

    win = w_in[l]
    o_kv = Q_RANK
    o_kr = o_kv + KV_RANK
    o_u = o_kr + QK_ROPE
    w_kr = win[:, o_kr:o_kr + QK_ROPE]
    zk = jnp.zeros((D_MODEL, QK_NOPE), F32)
    zp = jnp.zeros((D_MODEL, HEAD_PAD - QK_NOPE - QK_ROPE), F32)
    wa = jnp.concatenate([win[:, 0:o_kr], zk, w_kr, zp, zk, _rot_cols(w_kr), zp, win[:, o_u:]], axis=-1).astype(BF16)
    wuq = w_uq[l].reshape(Q_RANK, MLA_HEADS, QK_NOPE + QK_ROPE)
    wq_nope, wq_rope = wuq[..., :QK_NOPE], wuq[..., QK_NOPE:]
    wq_rot = _rot_cols(wq_rope.reshape(Q_RANK, -1)).reshape(wq_rope.shape)
    wq = jnp.concatenate([_pad_head_cols(wq_nope, wq_rope),
                          _pad_head_cols(jnp.zeros_like(wq_nope), wq_rot)], axis=-1).astype(BF16)
    wukv = w_ukv[l].reshape(KV_RANK, MLA_HEADS, QK_NOPE + V_HEAD)
    wk = _pad_head_cols(wukv[..., :QK_NOPE], jnp.zeros((KV_RANK, MLA_HEADS, QK_ROPE), F32))
    wv = wukv[..., QK_NOPE:].reshape(KV_RANK, MLA_WIDTH)
    wkv = jnp.concatenate([wk, wv], axis=-1).astype(BF16)

    cos, sin = _rope_tables(seq)
    cq_t, sq_t = _head_tables(cos, sin, 1.0, ATTN_SCALE * math.log2(math.e))
    ck_t, sk_t = _head_tables(cos, sin, 0.0, 1.0)
    one = jnp.ones((n_ctx, QK_ROPE), F32)
    ckc_t, skc_t = _head_tables(one, jnp.zeros_like(one), 0.0, 1.0)

    gmix = g_mix[l][None]
    q, kx, vx, ux = _proj(x1, mod, gmix, wa, g_cq[l][None], wq, g_ckv[l][None], wkv,
                          (cq_t, sq_t, ck_t, sk_t), None, 512, True)
    kc, vc, uc = _proj(ctx1, mod, gmix, wa, g_cq[l][None], wq, g_ckv[l][None], wkv,
                       (ckc_t, skc_t, ckc_t, skc_t), ctx_row, n_ctx, False)

    attn = _attention(q, kx, kc, vx, vc, 1024)

    eye = jnp.eye(SSM_GROUPS, dtype=F32)
    lre = lam_re[l].reshape(2, 1, N_STATE)
    lim = lam_im[l].reshape(2, 1, N_STATE)
    ldt = jnp.repeat(log_dt[l], SSM_STATE, axis=-1).reshape(2, 1, N_STATE)
    bre = jnp.einsum('dgpc,gh->dgchp', b_re[l], eye).reshape(2, SSM_WIDTH, N_STATE)
    bim = jnp.einsum('dgpc,gh->dgchp', b_im[l], eye).reshape(2, SSM_WIDTH, N_STATE)
    cre = jnp.einsum('dgcp,gh->dgphc', c_re[l], eye).reshape(2, N_STATE, SSM_WIDTH)
    cim = jnp.einsum('dgcp,gh->dgphc', c_im[l], eye).reshape(2, N_STATE, SSM_WIDTH)
    y = _s5(uc, ux, lre, lim, ldt, bre, bim, cre, cim, n_ctx)

    wo = w_out[l].astype(BF16)
    return _final(x1, attn, ux, y, mod, d_skip[l][None], w_glu[l].astype(BF16), g_mla_out[l][None],
                  g_ssm_out[l][None], wo[:MLA_WIDTH], wo[MLA_WIDTH:], g_ffn2[l][None],
                  w_gu2[l].astype(BF16), w_down2[l].astype(BF16), g_final[None], 512)
```

```python
import functools
import math

import jax
import jax.numpy as jnp
from jax import lax
from jax.experimental import pallas as pl
from jax.experimental.pallas import tpu as pltpu

D_MODEL = 1024
GRID_W = 64
N_MOD = 9
D_FF = 2816
MLA_HEADS = 12
QK_NOPE = 64
QK_ROPE = 32
V_HEAD = 64
Q_RANK = 256
KV_RANK = 128
MLA_WIDTH = MLA_HEADS * V_HEAD
SSM_GROUPS = 16
SSM_CH = 16
SSM_WIDTH = SSM_GROUPS * SSM_CH
SSM_STATE = 64
N_STATE = SSM_GROUPS * SSM_STATE
ROPE_BASE = 10000.0
ATTN_SCALE = (QK_NOPE + QK_ROPE) ** -0.5
EPS = 1e-6

LANES = 128
SUBLANES = 8
MXU_DIM = 256
HEAD_PAD = 128
VMEM_LIMIT = 56 * 1024 * 1024

FFN_TM = 1024
CTX_FFN_TM = 512
PROJ_TM = 512
ATTN_TQ = 1024
FINAL_TM = 512

F32 = jnp.float32
BF16 = jnp.bfloat16


def _rms(x, g):
    ms = jnp.mean(x * x, axis=-1, keepdims=True)
    return x * lax.rsqrt(ms + EPS) * g


def _dot(a, b):
    return jnp.dot(a, b, preferred_element_type=F32)


def _const_spec(shape):
    zeros = (0,) * len(shape)
    return pl.BlockSpec(shape, lambda *_: zeros, pipeline_mode=pl.Buffered(1))


def _params(*sem):
    return pltpu.CompilerParams(dimension_semantics=sem, vmem_limit_bytes=VMEM_LIMIT)


def _adaln_kernel(c_ref, w_ref, b_ref, o_ref):
    c = c_ref[...]
    s = (c * jax.nn.sigmoid(c)).astype(BF16)
    o_ref[...] = _dot(s, w_ref[...].astype(BF16)) + b_ref[...]


def _adaln(cvec, w_mod, b_mod):
    rows = cvec.shape[0]
    n = w_mod.shape[1]
    bn = D_MODEL
    return pl.pallas_call(
        _adaln_kernel,
        grid=(n // bn,),
        in_specs=[pl.BlockSpec((rows, D_MODEL), lambda i: (0, 0)),
                  pl.BlockSpec((D_MODEL, bn), lambda i: (0, i)),
                  pl.BlockSpec((1, bn), lambda i: (0, i))],
        out_specs=pl.BlockSpec((rows, bn), lambda i: (0, i)),
        out_shape=jax.ShapeDtypeStruct((rows, n), F32),
        compiler_params=_params("parallel"),
        name="adaln",
    )(cvec, w_mod, b_mod)


FF_TILES = D_FF // MXU_DIM
FF_SPLITS = (0, (FF_TILES + 1) // 2 * MXU_DIM, D_FF)


def _swiglu(h, wgu_ref, wd_ref):
    acc = None
    for lo, hi in zip(FF_SPLITS[:-1], FF_SPLITS[1:]):
        gate = _dot(h, wgu_ref[:, lo:hi])
        up = _dot(h, wgu_ref[:, D_FF + lo:D_FF + hi])
        act = (gate * jax.nn.sigmoid(gate) * up).astype(BF16)
        part = _dot(act, wd_ref[lo:hi, :])
        acc = part if acc is None else acc + part
    return acc


def _ffn_kernel(x_ref, mod_ref, g_ref, wgu_ref, wd_ref, o_ref):
    x = x_ref[0]
    m = mod_ref[0]
    h = (_rms(x, g_ref[...]) * (1.0 + m[1:2]) + m[0:1]).astype(BF16)
    o_ref[0] = x + 0.5 * m[2:3] * _swiglu(h, wgu_ref, wd_ref)


def _ffn(x, mod, g, wgu, wd, mod_row, tm):
    nb, n, _ = x.shape
    mod_map = (lambda b, j: (b, 0, 0)) if mod_row is None else (lambda b, j: (mod_row, 0, 0))
    return pl.pallas_call(
        _ffn_kernel,
        grid=(nb, n // tm),
        in_specs=[pl.BlockSpec((1, tm, D_MODEL), lambda b, j: (b, j, 0)),
                  pl.BlockSpec((1, N_MOD, D_MODEL), mod_map),
                  _const_spec((1, D_MODEL)),
                  _const_spec((D_MODEL, 2 * D_FF)),
                  _const_spec((D_FF, D_MODEL))],
        out_specs=pl.BlockSpec((1, tm, D_MODEL), lambda b, j: (b, j, 0)),
        out_shape=jax.ShapeDtypeStruct(x.shape, F32),
        compiler_params=_params("parallel", "parallel"),
        name="ffn",
    )(x, mod, g, wgu, wd)


W_A_COLS = Q_RANK + KV_RANK + 2 * HEAD_PAD + SSM_WIDTH
QK_WIDTH = MLA_HEADS * HEAD_PAD


def _proj_kernel(x_ref, mod_ref, g_ref, wa_ref, gcq_ref, wq_ref, gckv_ref, wkv_ref,
                 cq_ref, sq_ref, ck_ref, sk_ref, *out_refs, with_q):
    if with_q:
        q_ref, k_ref, v_ref, u_ref = out_refs
    else:
        k_ref, v_ref, u_ref = out_refs
    x = x_ref[0]
    m = mod_ref[0]
    h = (_rms(x, g_ref[...]) * (1.0 + m[4:5]) + m[3:4]).astype(BF16)
    p = _dot(h, wa_ref[...])
    o_kv = Q_RANK
    o_kr = o_kv + KV_RANK
    o_rot = o_kr + HEAD_PAD
    o_u = o_rot + HEAD_PAD
    u_ref[...] = p[:, o_u:o_u + SSM_WIDTH]
    kr = p[:, o_kr:o_kr + HEAD_PAD] * ck_ref[...] + p[:, o_rot:o_rot + HEAD_PAD] * sk_ref[...]
    ckv = _rms(p[:, o_kv:o_kv + KV_RANK], gckv_ref[...]).astype(BF16)
    kv = _dot(ckv, wkv_ref[...])
    for hd in range(MLA_HEADS):
        lo = hd * HEAD_PAD
        k_ref[0, :, lo:lo + HEAD_PAD] = (kv[:, lo:lo + HEAD_PAD] + kr).astype(BF16)
    v_ref[0] = kv[:, QK_WIDTH:QK_WIDTH + MLA_WIDTH].astype(BF16)
    if with_q:
        cq = _rms(p[:, 0:Q_RANK], gcq_ref[...]).astype(BF16)
        qa = _dot(cq, wq_ref[:, 0:QK_WIDTH])
        qb = _dot(cq, wq_ref[:, QK_WIDTH:2 * QK_WIDTH])
        cq_t = cq_ref[...]
        sq_t = sq_ref[...]
        for hd in range(MLA_HEADS):
            lo = hd * HEAD_PAD
            q_ref[0, :, lo:lo + HEAD_PAD] = (qa[:, lo:lo + HEAD_PAD] * cq_t
                                             + qb[:, lo:lo + HEAD_PAD] * sq_t).astype(BF16)


def _proj(x, mod, g_mix, wa, gcq, wq, gckv, wkv, tabs, mod_row, tm, with_q):
    nb, n, _ = x.shape
    mod_map = (lambda b, j: (b, 0, 0)) if mod_row is None else (lambda b, j: (mod_row, 0, 0))
    tab_spec = pl.BlockSpec((tm, HEAD_PAD), lambda b, j: (j, 0))
    out_shape = [jax.ShapeDtypeStruct((nb, n, QK_WIDTH), BF16),
                 jax.ShapeDtypeStruct((nb, n, MLA_WIDTH), BF16),
                 jax.ShapeDtypeStruct((n, nb * SSM_WIDTH), F32)]
    out_specs = [pl.BlockSpec((1, tm, QK_WIDTH), lambda b, j: (b, j, 0)),
                 pl.BlockSpec((1, tm, MLA_WIDTH), lambda b, j: (b, j, 0)),
                 pl.BlockSpec((tm, SSM_WIDTH), lambda b, j: (j, b))]
    if with_q:
        out_shape = [jax.ShapeDtypeStruct((nb, n, QK_WIDTH), BF16)] + out_shape
        out_specs = [pl.BlockSpec((1, tm, QK_WIDTH), lambda b, j: (b, j, 0))] + out_specs
    return pl.pallas_call(
        functools.partial(_proj_kernel, with_q=with_q),
        grid=(nb, n // tm),
        in_specs=[pl.BlockSpec((1, tm, D_MODEL), lambda b, j: (b, j, 0)),
                  pl.BlockSpec((1, N_MOD, D_MODEL), mod_map),
                  _const_spec((1, D_MODEL)),
                  _const_spec(wa.shape),
                  _const_spec((1, Q_RANK)),
                  _const_spec(wq.shape),
                  _const_spec((1, KV_RANK)),
                  _const_spec(wkv.shape),
                  tab_spec, tab_spec, tab_spec, tab_spec],
        out_specs=out_specs,
        out_shape=out_shape,
        compiler_params=_params("parallel", "parallel"),
        name="proj_q" if with_q else "proj_ctx",
    )(x, mod, g_mix, wa, gcq, wq, gckv, wkv, *tabs)


HEADS_PER_STEP = 2
KEY_SPLITS = (0, 2048, 4096)


def _attn_kernel(q_ref, kx_ref, kc_ref, vx_ref, vc_ref, o_ref):
    nt = (((1,), (1,)), ((), ()))
    tq = q_ref.shape[1]
    assert KEY_SPLITS[-1] == kx_ref.shape[1]
    chunks = [(kx_ref, vx_ref, lo, hi - lo) for lo, hi in zip(KEY_SPLITS[:-1], KEY_SPLITS[1:])]
    chunks.append((kc_ref, vc_ref, 0, kc_ref.shape[1]))
    outs = []
    for hh in range(HEADS_PER_STEP):
        lo = hh * HEAD_PAD
        q = q_ref[0, :, lo:lo + HEAD_PAD]
        m_b = l_vec = acc = None
        for k_ref, v_ref, start, size in chunks:
            s = lax.dot_general(q, k_ref[0, start:start + size, lo:lo + HEAD_PAD], nt,
                                preferred_element_type=F32)
            cols = [s[:, i * LANES:(i + 1) * LANES] for i in range(size // LANES)]
            c_vec = functools.reduce(jnp.maximum, cols)
            c_max = jnp.broadcast_to(jnp.max(c_vec, axis=-1, keepdims=True), (tq, LANES))
            m_new = c_max if m_b is None else jnp.maximum(m_b, c_max)
            ps = [jnp.exp2(col - m_new) for col in cols]
            p_sum = functools.reduce(jnp.add, ps)
            v_lo = (hh // 2) * 2 * V_HEAD
            pv = _dot(jnp.concatenate([p.astype(BF16) for p in ps], axis=-1),
                      v_ref[0, start:start + size, v_lo:v_lo + 2 * V_HEAD])
            if m_b is None:
                l_vec, acc = p_sum, pv
            else:
                alpha = jnp.exp2(m_b - m_new)
                l_vec = alpha * l_vec + p_sum
                acc = alpha * acc + pv
            m_b = m_new
        outs.append(acc / jnp.sum(l_vec, axis=-1, keepdims=True))
    lane = lax.broadcasted_iota(jnp.int32, outs[0].shape, 1)
    for pr in range(HEADS_PER_STEP // 2):
        o_ref[0, :, pr * 2 * V_HEAD:(pr + 1) * 2 * V_HEAD] = jnp.where(lane < V_HEAD, outs[2 * pr], outs[2 * pr + 1])


def _attention(q, kx, kc, vx, vc, tq):
    nb, n, _ = q.shape
    nk = kx.shape[1]
    nc = kc.shape[1]
    pairs = MLA_HEADS // HEADS_PER_STEP
    qk_blk = HEADS_PER_STEP * HEAD_PAD
    v_blk = HEADS_PER_STEP * V_HEAD
    return pl.pallas_call(
        _attn_kernel,
        grid=(nb, pairs, n // tq),
        in_specs=[pl.BlockSpec((1, tq, qk_blk), lambda b, p, j: (b, j, p)),
                  pl.BlockSpec((1, nk, qk_blk), lambda b, p, j: (b, 0, p)),
                  pl.BlockSpec((1, nc, qk_blk), lambda b, p, j: (b, 0, p)),
                  pl.BlockSpec((1, nk, v_blk), lambda b, p, j: (b, 0, p)),
                  pl.BlockSpec((1, nc, v_blk), lambda b, p, j: (b, 0, p))],
        out_specs=pl.BlockSpec((1, tq, v_blk), lambda b, p, j: (b, j, p)),
        out_shape=jax.ShapeDtypeStruct((nb, n, MLA_WIDTH), F32),
        compiler_params=_params("parallel", "parallel", "parallel"),
        name="attention",
    )(q, kx, kc, vx, vc)


SCAN_BLOCK = 32


def _s5_kernel(uc_ref, ux_ref, lre_ref, lim_ref, ldt_ref, bre_ref, bim_ref, cre_ref, cim_ref,
               y_ref, bb_ref, cm_ref, a_ref, h_ref, bd_ref, *, t_chunk, reverse):
    j = pl.program_id(0)

    @pl.when(j == 0)
    def _discretize():
        dt = jnp.exp(ldt_ref[...])
        lr = jnp.minimum(lre_ref[...], -1e-4)
        li = lim_ref[...]
        mag = jnp.exp(lr * dt)
        ar = mag * jnp.cos(li * dt)
        ai = mag * jnp.sin(li * dt)
        den = lr * lr + li * li
        fr = ((ar - 1.0) * lr + ai * li) / den
        fi = (ai * lr - (ar - 1.0) * li) / den
        bre = bre_ref[...]
        bim = bim_ref[...]
        bb_ref[:, 0:N_STATE] = (fr * bre - fi * bim).astype(BF16)
        bb_ref[:, N_STATE:2 * N_STATE] = (fr * bim + fi * bre).astype(BF16)
        cm_ref[0:N_STATE, :] = cre_ref[...].astype(BF16)
        cm_ref[N_STATE:2 * N_STATE, :] = (-cim_ref[...]).astype(BF16)
        a_ref[0:SUBLANES, :] = jnp.broadcast_to(ar, (SUBLANES, N_STATE))
        a_ref[SUBLANES:2 * SUBLANES, :] = jnp.broadcast_to(ai, (SUBLANES, N_STATE))
        h_ref[...] = jnp.zeros_like(h_ref)

    blk_rows = SCAN_BLOCK * SUBLANES
    order = list(range(t_chunk // SCAN_BLOCK))
    steps = list(range(SCAN_BLOCK))
    if reverse:
        order.reverse()
        steps.reverse()

    def run(u_ref, with_readout):
        ar = a_ref[0:SUBLANES, :]
        ai = a_ref[SUBLANES:2 * SUBLANES, :]

        def drive(sb):
            u = u_ref[sb * SCAN_BLOCK:(sb + 1) * SCAN_BLOCK, :].reshape(blk_rows, SSM_WIDTH)
            bd_ref[sb * blk_rows:(sb + 1) * blk_rows, :] = _dot(u.astype(BF16), bb_ref[...])

        def scan(sb, hr, hi):
            for t in steps:
                r = sb * blk_rows + t * SUBLANES
                nr = ar * hr - ai * hi + bd_ref[r:r + SUBLANES, 0:N_STATE]
                ni = ar * hi + ai * hr + bd_ref[r:r + SUBLANES, N_STATE:2 * N_STATE]
                bd_ref[r:r + SUBLANES, 0:N_STATE] = nr
                bd_ref[r:r + SUBLANES, N_STATE:2 * N_STATE] = ni
                hr, hi = nr, ni
            return hr, hi

        def readout(sb):
            y = _dot(bd_ref[sb * blk_rows:(sb + 1) * blk_rows, :].astype(BF16), cm_ref[...])
            y_ref[sb * SCAN_BLOCK:(sb + 1) * SCAN_BLOCK, :] = y.reshape(SCAN_BLOCK, SUBLANES * SSM_WIDTH)

        hr = h_ref[0:SUBLANES, :]
        hi = h_ref[SUBLANES:2 * SUBLANES, :]
        drive(order[0])
        for k, sb in enumerate(order):
            if k + 1 < len(order):
                drive(order[k + 1])
            hr, hi = scan(sb, hr, hi)
            if with_readout and k >= 1:
                readout(order[k - 1])
        if with_readout:
            readout(order[-1])
        h_ref[0:SUBLANES, :] = hr
        h_ref[SUBLANES:2 * SUBLANES, :] = hi

    @pl.when(j == 0)
    def _context():
        run(uc_ref, False)

    @pl.when(j > 0)
    def _latent():
        run(ux_ref, True)


def _s5(u_c, u_x, lre, lim, ldt, bre, bim, cre, cim, t_chunk, reverse):
    rows = t_chunk * SUBLANES
    width = SUBLANES * SSM_WIDTH
    assert u_c.shape == (t_chunk, width) and u_x.shape[1] == width
    nch = u_x.shape[0] // t_chunk

    def chunk(j):
        jj = jnp.maximum(j - 1, 0)
        return nch - 1 - jj if reverse else jj

    return pl.pallas_call(
        functools.partial(_s5_kernel, t_chunk=t_chunk, reverse=reverse),
        grid=(nch + 1,),
        in_specs=[pl.BlockSpec((t_chunk, width), lambda j: (0, 0)),
                  pl.BlockSpec((t_chunk, width), lambda j: (chunk(j), 0)),
                  _const_spec((1, N_STATE)), _const_spec((1, N_STATE)), _const_spec((1, N_STATE)),
                  _const_spec((SSM_WIDTH, N_STATE)), _const_spec((SSM_WIDTH, N_STATE)),
                  _const_spec((N_STATE, SSM_WIDTH)), _const_spec((N_STATE, SSM_WIDTH))],
        out_specs=pl.BlockSpec((t_chunk, width), lambda j: (chunk(j), 0)),
        out_shape=jax.ShapeDtypeStruct((u_x.shape[0], width), F32),
        scratch_shapes=[pltpu.VMEM((SSM_WIDTH, 2 * N_STATE), BF16),
                        pltpu.VMEM((2 * N_STATE, SSM_WIDTH), BF16),
                        pltpu.VMEM((2 * SUBLANES, N_STATE), F32),
                        pltpu.VMEM((2 * SUBLANES, N_STATE), F32),
                        pltpu.VMEM((rows, 2 * N_STATE), F32)],
        compiler_params=_params("arbitrary"),
        name="s5_rev" if reverse else "s5_fwd",
    )(u_c, u_x, lre, lim, ldt, bre, bim, cre, cim)


def _gelu_tanh(x):
    return 0.5 * x * (1.0 + jnp.tanh(math.sqrt(2.0 / math.pi) * (x + 0.044715 * (x * x * x))))


def _final_kernel(x_ref, attn_ref, u_ref, yf_ref, yb_ref, mod_ref, dsk_ref, wglu_ref, gmla_ref, gssm_ref,
                  woa_ref, wos_ref, g2_ref, wgu_ref, wd_ref, gf_ref, o_ref):
    x = x_ref[0]
    m = mod_ref[0]
    y = u_ref[...] * dsk_ref[...] + yf_ref[...] + yb_ref[...]
    z = _gelu_tanh(y).astype(BF16)
    ag = _dot(z, wglu_ref[...])
    ssm = ag[:, 0:SSM_WIDTH] * jax.nn.sigmoid(ag[:, SSM_WIDTH:2 * SSM_WIDTH])
    a_n = _rms(attn_ref[0], gmla_ref[...]).astype(BF16)
    s_n = _rms(ssm, gssm_ref[...]).astype(BF16)
    mix = _dot(a_n, woa_ref[...]) + _dot(s_n, wos_ref[...])
    x = x + m[5:6] * mix
    h = (_rms(x, g2_ref[...]) * (1.0 + m[7:8]) + m[6:7]).astype(BF16)
    x = x + 0.5 * m[8:9] * _swiglu(h, wgu_ref, wd_ref)
    o_ref[0] = _rms(x, gf_ref[...])


def _final(x, attn, u, y_fwd, y_bwd, mod, dsk, wglu, gmla, gssm, woa, wos, g2, wgu, wd, gf, tm):
    nb, n, _ = x.shape
    ssm_spec = pl.BlockSpec((tm, SSM_WIDTH), lambda b, j: (j, b))
    return pl.pallas_call(
        _final_kernel,
        grid=(nb, n // tm),
        in_specs=[pl.BlockSpec((1, tm, D_MODEL), lambda b, j: (b, j, 0)),
                  pl.BlockSpec((1, tm, MLA_WIDTH), lambda b, j: (b, j, 0)),
                  ssm_spec, ssm_spec, ssm_spec,
                  pl.BlockSpec((1, N_MOD, D_MODEL), lambda b, j: (b, 0, 0)),
                  _const_spec((1, SSM_WIDTH)),
                  _const_spec((SSM_WIDTH, 2 * SSM_WIDTH)),
                  _const_spec((1, MLA_WIDTH)),
                  _const_spec((1, SSM_WIDTH)),
                  _const_spec((MLA_WIDTH, D_MODEL)),
                  _const_spec((SSM_WIDTH, D_MODEL)),
                  _const_spec((1, D_MODEL)),
                  _const_spec((D_MODEL, 2 * D_FF)),
                  _const_spec((D_FF, D_MODEL)),
                  _const_spec((1, D_MODEL))],
        out_specs=pl.BlockSpec((1, tm, D_MODEL), lambda b, j: (b, j, 0)),
        out_shape=jax.ShapeDtypeStruct(x.shape, F32),
        compiler_params=_params("parallel", "parallel"),
        name="final",
    )(x, attn, u, y_fwd, y_bwd, mod, dsk, wglu, gmla, gssm, woa, wos, g2, wgu, wd, gf)


def _rot_cols(w):
    wp = w.reshape(w.shape[0], -1, 2)
    return jnp.stack([-wp[..., 1], wp[..., 0]], axis=-1).reshape(w.shape)


def _rope_tables(n_tokens):
    rows = n_tokens // GRID_W
    row = jnp.repeat(jnp.arange(rows), GRID_W).astype(F32)
    col = jnp.tile(jnp.arange(GRID_W), rows).astype(F32)
    per_axis = QK_ROPE // 2
    inv_freq = ROPE_BASE ** (-jnp.arange(0, per_axis, 2, dtype=F32) / per_axis)
    ang = jnp.concatenate([row[:, None] * inv_freq, col[:, None] * inv_freq], axis=-1)
    cos = jnp.repeat(jnp.cos(ang), 2, axis=-1)
    sin = jnp.repeat(jnp.sin(ang), 2, axis=-1)
    return cos, sin


def _head_tables(cos, sin, nope_fill, scale):
    n = cos.shape[0]
    pad = HEAD_PAD - QK_NOPE - QK_ROPE
    c = jnp.concatenate([jnp.full((n, QK_NOPE), nope_fill, F32), cos, jnp.zeros((n, pad), F32)], axis=-1)
    s = jnp.concatenate([jnp.zeros((n, QK_NOPE), F32), sin, jnp.zeros((n, pad), F32)], axis=-1)
    return c * scale, s * scale


def _pad_head_cols(w_nope, w_rope):
    k, h = w_nope.shape[:2]
    pad = HEAD_PAD - QK_NOPE - QK_ROPE
    return jnp.concatenate([w_nope, w_rope, jnp.zeros((k, h, pad), w_nope.dtype)], axis=-1).reshape(k, h * HEAD_PAD)


def kernel(x, c, ctx, c_ctx, w_mod, b_mod, g_ffn1, w_gu1, w_down1, g_mix, w_in, g_cq, w_uq, g_ckv, w_ukv,
           lam_re, lam_im, log_dt, b_re, b_im, c_re, c_im, d_skip, w_glu, g_mla_out, g_ssm_out, w_out,
           g_ffn2, w_gu2, w_down2, g_final):
    bsz, seq, _ = x.shape
    n_ctx = ctx.shape[1]
    assert w_mod.shape[0] == 1, "single-layer block"
    l = 0

    rows = 2 * SUBLANES
    cvec = jnp.concatenate([c, c_ctx[None], jnp.zeros((rows - bsz - 1, D_MODEL), F32)], axis=0)
    mod = _adaln(cvec, w_mod[l], b_mod[l][None]).reshape(rows, N_MOD, D_MODEL)
    ctx_row = bsz

    wgu1 = w_gu1[l].astype(BF16)
    wd1 = w_down1[l].astype(BF16)
    x1 = _ffn(x, mod, g_ffn1[l][None], wgu1, wd1, None, FFN_TM)
    ctx1 = _ffn(ctx.reshape(1, bsz * n_ctx, D_MODEL), mod, g_ffn1[l][None], wgu1, wd1, ctx_row, CTX_FFN_TM)
    ctx1 = ctx1.reshape(bsz, n_ctx, D_MODEL)

    win = w_in[l]
    o_kv = Q_RANK
    o_kr = o_kv + KV_RANK
    o_u = o_kr + QK_ROPE
    w_kr = win[:, o_kr:o_kr + QK_ROPE]
    zk = jnp.zeros((D_MODEL, QK_NOPE), F32)
    zp = jnp.zeros((D_MODEL, HEAD_PAD - QK_NOPE - QK_ROPE), F32)
    wa = jnp.concatenate([win[:, 0:o_kr], zk, w_kr, zp, zk, _rot_cols(w_kr), zp, win[:, o_u:]], axis=-1).astype(BF16)
    wuq = w_uq[l].reshape(Q_RANK, MLA_HEADS, QK_NOPE + QK_ROPE)
    wq_nope, wq_rope = wuq[..., :QK_NOPE], wuq[..., QK_NOPE:]
    wq_rot = _rot_cols(wq_rope.reshape(Q_RANK, -1)).reshape(wq_rope.shape)
    wq = jnp.concatenate([_pad_head_cols(wq_nope, wq_rope),
                          _pad_head_cols(jnp.zeros_like(wq_nope), wq_rot)], axis=-1).astype(BF16)
    wukv = w_ukv[l].reshape(KV_RANK, MLA_HEADS, QK_NOPE + V_HEAD)
    wk = _pad_head_cols(wukv[..., :QK_NOPE], jnp.zeros((KV_RANK, MLA_HEADS, QK_ROPE), F32))
    wv = wukv[..., QK_NOPE:].reshape(KV_RANK, MLA_WIDTH)
    wkv = jnp.concatenate([wk, wv], axis=-1).astype(BF16)

    cos, sin = _rope_tables(seq)
    cq_t, sq_t = _head_tables(cos, sin, 1.0, ATTN_SCALE * math.log2(math.e))
    ck_t, sk_t = _head_tables(cos, sin, 0.0, 1.0)
    one = jnp.ones((n_ctx, QK_ROPE), F32)
    ckc_t, skc_t = _head_tables(one, jnp.zeros_like(one), 0.0, 1.0)

    gmix = g_mix[l][None]
    q, kx, vx, ux = _proj(x1, mod, gmix, wa, g_cq[l][None], wq, g_ckv[l][None], wkv,
                          (cq_t, sq_t, ck_t, sk_t), None, PROJ_TM, True)
    kc, vc, uc = _proj(ctx1, mod, gmix, wa, g_cq[l][None], wq, g_ckv[l][None], wkv,
                       (ckc_t, skc_t, ckc_t, skc_t), ctx_row, n_ctx, False)

    attn = _attention(q, kx, kc, vx, vc, ATTN_TQ)

    eye = jnp.eye(SSM_GROUPS, dtype=F32)
    lre = lam_re[l].reshape(2, 1, N_STATE)
    lim = lam_im[l].reshape(2, 1, N_STATE)
    ldt = jnp.repeat(log_dt[l], SSM_STATE, axis=-1).reshape(2, 1, N_STATE)
    bre = jnp.einsum('dgpc,gh->dgchp', b_re[l], eye).reshape(2, SSM_WIDTH, N_STATE)
    bim = jnp.einsum('dgpc,gh->dgchp', b_im[l], eye).reshape(2, SSM_WIDTH, N_STATE)
    cre = jnp.einsum('dgcp,gh->dgphc', c_re[l], eye).reshape(2, N_STATE, SSM_WIDTH)
    cim = jnp.einsum('dgcp,gh->dgphc', c_im[l], eye).reshape(2, N_STATE, SSM_WIDTH)
    y_fwd, y_bwd = [_s5(uc, ux, lre[d], lim[d], ldt[d], bre[d], bim[d], cre[d], cim[d], n_ctx, bool(d))
                    for d in range(2)]

    wo = w_out[l].astype(BF16)
    return _final(x1, attn, ux, y_fwd, y_bwd, mod, d_skip[l][None], w_glu[l].astype(BF16), g_mla_out[l][None],
                  g_ssm_out[l][None], wo[:MLA_WIDTH], wo[MLA_WIDTH:], g_ffn2[l][None],
                  w_gu2[l].astype(BF16), w_down2[l].astype(BF16), g_final[None], FINAL_TM)
```

```python
import functools
import math

import jax
import jax.numpy as jnp
import numpy as np
from jax import lax
from jax.experimental import pallas as pl
from jax.experimental.pallas import tpu as pltpu

D_MODEL = 1024
GRID_W = 64
N_MOD = 9
D_FF = 2816
MLA_HEADS = 12
QK_NOPE = 64
QK_ROPE = 32
V_HEAD = 64
Q_RANK = 256
KV_RANK = 128
MLA_WIDTH = MLA_HEADS * V_HEAD
SSM_GROUPS = 16
SSM_CH = 16
SSM_WIDTH = SSM_GROUPS * SSM_CH
SSM_STATE = 64
N_STATE = SSM_GROUPS * SSM_STATE
ROPE_BASE = 10000.0
ATTN_SCALE = (QK_NOPE + QK_ROPE) ** -0.5
EPS = 1e-6

LANES = 128
SUBLANES = 8
MXU_DIM = 256
HEAD_PAD = 128
VMEM_LIMIT = 56 * 1024 * 1024

FFN_TM = 1024
CTX_FFN_TM = 512
PROJ_TM = 1024
ATTN_TQ = 1024
FINAL_TM = 512

F32 = jnp.float32
BF16 = jnp.bfloat16


def _rms(x, g):
    ms = jnp.mean(x * x, axis=-1, keepdims=True)
    return x * lax.rsqrt(ms + EPS) * g


def _dot(a, b):
    return jnp.dot(a, b, preferred_element_type=F32)


def _const_spec(shape):
    zeros = (0,) * len(shape)
    return pl.BlockSpec(shape, lambda *_: zeros, pipeline_mode=pl.Buffered(1))


def _params(*sem):
    return pltpu.CompilerParams(dimension_semantics=sem, vmem_limit_bytes=VMEM_LIMIT)


def _adaln_kernel(c_ref, w_ref, b_ref, o_ref):
    c = c_ref[...]
    s = (c * jax.nn.sigmoid(c)).astype(BF16)
    o_ref[...] = _dot(s, w_ref[...].astype(BF16)) + b_ref[...]


def _adaln(cvec, w_mod, b_mod):
    rows = cvec.shape[0]
    n = w_mod.shape[1]
    bn = D_MODEL
    return pl.pallas_call(
        _adaln_kernel,
        grid=(n // bn,),
        in_specs=[pl.BlockSpec((rows, D_MODEL), lambda i: (0, 0)),
                  pl.BlockSpec((D_MODEL, bn), lambda i: (0, i)),
                  pl.BlockSpec((1, bn), lambda i: (0, i))],
        out_specs=pl.BlockSpec((rows, bn), lambda i: (0, i)),
        out_shape=jax.ShapeDtypeStruct((rows, n), F32),
        compiler_params=_params("parallel"),
        name="adaln",
    )(cvec, w_mod, b_mod)


FF_TILES = D_FF // MXU_DIM
FF_SPLITS = (0, (FF_TILES + 1) // 2 * MXU_DIM, D_FF)


def _swiglu(h, wgu_ref, wd_ref):
    acc = None
    for lo, hi in zip(FF_SPLITS[:-1], FF_SPLITS[1:]):
        gate = _dot(h, wgu_ref[:, lo:hi])
        up = _dot(h, wgu_ref[:, D_FF + lo:D_FF + hi])
        act = (gate * jax.nn.sigmoid(gate) * up).astype(BF16)
        part = _dot(act, wd_ref[lo:hi, :])
        acc = part if acc is None else acc + part
    return acc


def _ffn_kernel(x_ref, mod_ref, g_ref, wgu_ref, wd_ref, o_ref):
    x = x_ref[0]
    m = mod_ref[0]
    h = (_rms(x, g_ref[...]) * (1.0 + m[1:2]) + m[0:1]).astype(BF16)
    o_ref[0] = x + 0.5 * m[2:3] * _swiglu(h, wgu_ref, wd_ref)


def _ffn(x, mod, g, wgu, wd, mod_row, tm):
    nb, n, _ = x.shape
    mod_map = (lambda b, j: (b, 0, 0)) if mod_row is None else (lambda b, j: (mod_row, 0, 0))
    return pl.pallas_call(
        _ffn_kernel,
        grid=(nb, n // tm),
        in_specs=[pl.BlockSpec((1, tm, D_MODEL), lambda b, j: (b, j, 0)),
                  pl.BlockSpec((1, N_MOD, D_MODEL), mod_map),
                  _const_spec((1, D_MODEL)),
                  _const_spec((D_MODEL, 2 * D_FF)),
                  _const_spec((D_FF, D_MODEL))],
        out_specs=pl.BlockSpec((1, tm, D_MODEL), lambda b, j: (b, j, 0)),
        out_shape=jax.ShapeDtypeStruct(x.shape, F32),
        compiler_params=_params("parallel", "parallel"),
        name="ffn",
    )(x, mod, g, wgu, wd)


W_A_COLS = Q_RANK + KV_RANK + 2 * HEAD_PAD + SSM_WIDTH
QK_WIDTH = MLA_HEADS * HEAD_PAD


def _proj_kernel(x_ref, mod_ref, g_ref, wa_ref, gcq_ref, wq_ref, gckv_ref, wkv_ref,
                 cq_ref, sq_ref, ck_ref, sk_ref, *out_refs, with_q):
    if with_q:
        q_ref, k_ref, v_ref, u_ref = out_refs
    else:
        k_ref, v_ref, u_ref = out_refs
    x = x_ref[0]
    m = mod_ref[0]
    h = (_rms(x, g_ref[...]) * (1.0 + m[4:5]) + m[3:4]).astype(BF16)
    p = _dot(h, wa_ref[...])
    o_kv = Q_RANK
    o_kr = o_kv + KV_RANK
    o_rot = o_kr + HEAD_PAD
    o_u = o_rot + HEAD_PAD
    u_ref[...] = p[:, o_u:o_u + SSM_WIDTH]
    kr = p[:, o_kr:o_kr + HEAD_PAD] * ck_ref[...] + p[:, o_rot:o_rot + HEAD_PAD] * sk_ref[...]
    ckv = _rms(p[:, o_kv:o_kv + KV_RANK], gckv_ref[...]).astype(BF16)
    kv = _dot(ckv, wkv_ref[...])
    for hd in range(MLA_HEADS):
        lo = hd * HEAD_PAD
        k_ref[0, :, lo:lo + HEAD_PAD] = (kv[:, lo:lo + HEAD_PAD] + kr).astype(BF16)
    v_ref[0] = kv[:, QK_WIDTH:QK_WIDTH + MLA_WIDTH].astype(BF16)
    if with_q:
        cq = _rms(p[:, 0:Q_RANK], gcq_ref[...]).astype(BF16)
        qa = _dot(cq, wq_ref[:, 0:QK_WIDTH])
        qb = _dot(cq, wq_ref[:, QK_WIDTH:2 * QK_WIDTH])
        cq_t = cq_ref[...]
        sq_t = sq_ref[...]
        for hd in range(MLA_HEADS):
            lo = hd * HEAD_PAD
            q_ref[0, :, lo:lo + HEAD_PAD] = (qa[:, lo:lo + HEAD_PAD] * cq_t
                                             + qb[:, lo:lo + HEAD_PAD] * sq_t).astype(BF16)


def _proj(x, mod, g_mix, wa, gcq, wq, gckv, wkv, tabs, mod_row, tm, with_q):
    nb, n, _ = x.shape
    mod_map = (lambda b, j: (b, 0, 0)) if mod_row is None else (lambda b, j: (mod_row, 0, 0))
    tab_spec = pl.BlockSpec((tm, HEAD_PAD), lambda b, j: (j, 0))
    out_shape = [jax.ShapeDtypeStruct((nb, n, QK_WIDTH), BF16),
                 jax.ShapeDtypeStruct((nb, n, MLA_WIDTH), BF16),
                 jax.ShapeDtypeStruct((n, nb * SSM_WIDTH), F32)]
    out_specs = [pl.BlockSpec((1, tm, QK_WIDTH), lambda b, j: (b, j, 0)),
                 pl.BlockSpec((1, tm, MLA_WIDTH), lambda b, j: (b, j, 0)),
                 pl.BlockSpec((tm, SSM_WIDTH), lambda b, j: (j, b))]
    if with_q:
        out_shape = [jax.ShapeDtypeStruct((nb, n, QK_WIDTH), BF16)] + out_shape
        out_specs = [pl.BlockSpec((1, tm, QK_WIDTH), lambda b, j: (b, j, 0))] + out_specs
    return pl.pallas_call(
        functools.partial(_proj_kernel, with_q=with_q),
        grid=(nb, n // tm),
        in_specs=[pl.BlockSpec((1, tm, D_MODEL), lambda b, j: (b, j, 0)),
                  pl.BlockSpec((1, N_MOD, D_MODEL), mod_map),
                  _const_spec((1, D_MODEL)),
                  _const_spec(wa.shape),
                  _const_spec((1, Q_RANK)),
                  _const_spec(wq.shape),
                  _const_spec((1, KV_RANK)),
                  _const_spec(wkv.shape),
                  tab_spec, tab_spec, tab_spec, tab_spec],
        out_specs=out_specs,
        out_shape=out_shape,
        compiler_params=_params("parallel", "parallel"),
        name="proj_q" if with_q else "proj_ctx",
    )(x, mod, g_mix, wa, gcq, wq, gckv, wkv, *tabs)


HEADS_PER_STEP = 2
KEY_SPLITS = (0, 1024, 2048, 3072, 4096)


def _attn_kernel(q_ref, kx_ref, kc_ref, vx_ref, vc_ref, o_ref):
    nt = (((1,), (1,)), ((), ()))
    tq = q_ref.shape[1]
    assert KEY_SPLITS[-1] == kx_ref.shape[1]
    chunks = [(kx_ref, vx_ref, lo, hi - lo) for lo, hi in zip(KEY_SPLITS[:-1], KEY_SPLITS[1:])]
    chunks.append((kc_ref, vc_ref, 0, kc_ref.shape[1]))
    outs = []
    for hh in range(HEADS_PER_STEP):
        lo = hh * HEAD_PAD
        q = q_ref[0, :, lo:lo + HEAD_PAD]
        m_b = acc = None
        for k_ref, v_ref, start, size in chunks:
            s = lax.dot_general(q, k_ref[0, start:start + size, lo:lo + HEAD_PAD], nt,
                                preferred_element_type=F32)
            cols = [s[:, i * LANES:(i + 1) * LANES] for i in range(size // LANES)]
            c_vec = functools.reduce(jnp.maximum, cols)
            c_max = jnp.broadcast_to(jnp.max(c_vec, axis=-1, keepdims=True), (tq, LANES))
            m_new = c_max if m_b is None else jnp.maximum(m_b, c_max)
            p = jnp.concatenate([jnp.exp2(col - m_new).astype(BF16) for col in cols], axis=-1)
            v_lo = (hh // 2) * 2 * V_HEAD
            v_ext = jnp.concatenate([v_ref[0, start:start + size, v_lo:v_lo + 2 * V_HEAD],
                                     jnp.ones((size, LANES), BF16)], axis=-1)
            pv = _dot(p, v_ext)
            if m_b is None:
                acc = pv
            else:
                alpha = jnp.exp2(m_b - m_new)
                acc = jnp.concatenate([alpha, alpha], axis=-1) * acc + pv
            m_b = m_new
        outs.append(acc[:, 0:LANES] / acc[:, LANES:2 * LANES])
    lane = lax.broadcasted_iota(jnp.int32, outs[0].shape, 1)
    for pr in range(HEADS_PER_STEP // 2):
        o_ref[0, :, pr * 2 * V_HEAD:(pr + 1) * 2 * V_HEAD] = jnp.where(lane < V_HEAD, outs[2 * pr], outs[2 * pr + 1])


def _attention(q, kx, kc, vx, vc, tq):
    nb, n, _ = q.shape
    nk = kx.shape[1]
    nc = kc.shape[1]
    pairs = MLA_HEADS // HEADS_PER_STEP
    qk_blk = HEADS_PER_STEP * HEAD_PAD
    v_blk = HEADS_PER_STEP * V_HEAD
    return pl.pallas_call(
        _attn_kernel,
        grid=(nb, pairs, n // tq),
        in_specs=[pl.BlockSpec((1, tq, qk_blk), lambda b, p, j: (b, j, p)),
                  pl.BlockSpec((1, nk, qk_blk), lambda b, p, j: (b, 0, p)),
                  pl.BlockSpec((1, nc, qk_blk), lambda b, p, j: (b, 0, p)),
                  pl.BlockSpec((1, nk, v_blk), lambda b, p, j: (b, 0, p)),
                  pl.BlockSpec((1, nc, v_blk), lambda b, p, j: (b, 0, p))],
        out_specs=pl.BlockSpec((1, tq, v_blk), lambda b, p, j: (b, j, p)),
        out_shape=jax.ShapeDtypeStruct((nb, n, MLA_WIDTH), F32),
        compiler_params=_params("parallel", "parallel", "parallel"),
        name="attention",
    )(q, kx, kc, vx, vc)


SCAN_BLOCK = 16


def _s5_kernel(uc_ref, ux_ref, lre_ref, lim_ref, ldt_ref, bre_ref, bim_ref, cre_ref, cim_ref,
               y_ref, bb_ref, cm_ref, a_ref, h_ref, bd_ref, *, t_chunk, reverse):
    j = pl.program_id(0)

    @pl.when(j == 0)
    def _discretize():
        dt = jnp.exp(ldt_ref[...])
        lr = jnp.minimum(lre_ref[...], -1e-4)
        li = lim_ref[...]
        mag = jnp.exp(lr * dt)
        ar = mag * jnp.cos(li * dt)
        ai = mag * jnp.sin(li * dt)
        den = lr * lr + li * li
        fr = ((ar - 1.0) * lr + ai * li) / den
        fi = (ai * lr - (ar - 1.0) * li) / den
        bre = bre_ref[...]
        bim = bim_ref[...]
        bb_ref[:, 0:N_STATE] = (fr * bre - fi * bim).astype(BF16)
        bb_ref[:, N_STATE:2 * N_STATE] = (fr * bim + fi * bre).astype(BF16)
        cm_ref[0:N_STATE, :] = cre_ref[...].astype(BF16)
        cm_ref[N_STATE:2 * N_STATE, :] = (-cim_ref[...]).astype(BF16)
        a_ref[0:SUBLANES, :] = jnp.broadcast_to(ar, (SUBLANES, N_STATE))
        a_ref[SUBLANES:2 * SUBLANES, :] = jnp.broadcast_to(ai, (SUBLANES, N_STATE))
        h_ref[...] = jnp.zeros_like(h_ref)

    blk_rows = SCAN_BLOCK * SUBLANES
    order = list(range(t_chunk // SCAN_BLOCK))
    steps = list(range(SCAN_BLOCK))
    if reverse:
        order.reverse()
        steps.reverse()

    def run(u_ref, with_readout):
        ar = a_ref[0:SUBLANES, :]
        ai = a_ref[SUBLANES:2 * SUBLANES, :]

        def drive(sb):
            u = u_ref[sb * SCAN_BLOCK:(sb + 1) * SCAN_BLOCK, :].reshape(blk_rows, SSM_WIDTH)
            bd_ref[sb * blk_rows:(sb + 1) * blk_rows, :] = _dot(u.astype(BF16), bb_ref[...])

        def scan(sb, hr, hi):
            for t in steps:
                r = sb * blk_rows + t * SUBLANES
                nr = ar * hr - ai * hi + bd_ref[r:r + SUBLANES, 0:N_STATE]
                ni = ar * hi + ai * hr + bd_ref[r:r + SUBLANES, N_STATE:2 * N_STATE]
                bd_ref[r:r + SUBLANES, 0:N_STATE] = nr
                bd_ref[r:r + SUBLANES, N_STATE:2 * N_STATE] = ni
                hr, hi = nr, ni
            return hr, hi

        def readout(sb):
            y = _dot(bd_ref[sb * blk_rows:(sb + 1) * blk_rows, :].astype(BF16), cm_ref[...])
            y_ref[sb * SCAN_BLOCK:(sb + 1) * SCAN_BLOCK, :] = y.reshape(SCAN_BLOCK, SUBLANES * SSM_WIDTH)

        hr = h_ref[0:SUBLANES, :]
        hi = h_ref[SUBLANES:2 * SUBLANES, :]
        drive(order[0])
        for k, sb in enumerate(order):
            if k + 1 < len(order):
                drive(order[k + 1])
            hr, hi = scan(sb, hr, hi)
            if with_readout and k >= 1:
                readout(order[k - 1])
        if with_readout:
            readout(order[-1])
        h_ref[0:SUBLANES, :] = hr
        h_ref[SUBLANES:2 * SUBLANES, :] = hi

    @pl.when(j == 0)
    def _context():
        run(uc_ref, False)

    @pl.when(j > 0)
    def _latent():
        run(ux_ref, True)


def _s5(u_c, u_x, lre, lim, ldt, bre, bim, cre, cim, t_chunk, reverse):
    rows = t_chunk * SUBLANES
    width = SUBLANES * SSM_WIDTH
    assert u_c.shape == (t_chunk, width) and u_x.shape[1] == width
    nch = u_x.shape[0] // t_chunk

    def chunk(j):
        jj = jnp.maximum(j - 1, 0)
        return nch - 1 - jj if reverse else jj

    return pl.pallas_call(
        functools.partial(_s5_kernel, t_chunk=t_chunk, reverse=reverse),
        grid=(nch + 1,),
        in_specs=[pl.BlockSpec((t_chunk, width), lambda j: (0, 0)),
                  pl.BlockSpec((t_chunk, width), lambda j: (chunk(j), 0)),
                  _const_spec((1, N_STATE)), _const_spec((1, N_STATE)), _const_spec((1, N_STATE)),
                  _const_spec((SSM_WIDTH, N_STATE)), _const_spec((SSM_WIDTH, N_STATE)),
                  _const_spec((N_STATE, SSM_WIDTH)), _const_spec((N_STATE, SSM_WIDTH))],
        out_specs=pl.BlockSpec((t_chunk, width), lambda j: (chunk(j), 0)),
        out_shape=jax.ShapeDtypeStruct((u_x.shape[0], width), F32),
        scratch_shapes=[pltpu.VMEM((SSM_WIDTH, 2 * N_STATE), BF16),
                        pltpu.VMEM((2 * N_STATE, SSM_WIDTH), BF16),
                        pltpu.VMEM((2 * SUBLANES, N_STATE), F32),
                        pltpu.VMEM((2 * SUBLANES, N_STATE), F32),
                        pltpu.VMEM((rows, 2 * N_STATE), F32)],
        compiler_params=_params("arbitrary"),
        name="s5_rev" if reverse else "s5_fwd",
    )(u_c, u_x, lre, lim, ldt, bre, bim, cre, cim)


def _gelu_tanh(x):
    return 0.5 * x * (1.0 + jnp.tanh(math.sqrt(2.0 / math.pi) * (x + 0.044715 * (x * x * x))))


def _final_kernel(x_ref, attn_ref, u_ref, yf_ref, yb_ref, mod_ref, dsk_ref, wglu_ref, gmla_ref, gssm_ref,
                  woa_ref, wos_ref, g2_ref, wgu_ref, wd_ref, gf_ref, o_ref):
    x = x_ref[0]
    m = mod_ref[0]
    y = u_ref[...] * dsk_ref[...] + yf_ref[...] + yb_ref[...]
    z = _gelu_tanh(y).astype(BF16)
    ag = _dot(z, wglu_ref[...])
    ssm = ag[:, 0:SSM_WIDTH] * jax.nn.sigmoid(ag[:, SSM_WIDTH:2 * SSM_WIDTH])
    a_n = _rms(attn_ref[0], gmla_ref[...]).astype(BF16)
    s_n = _rms(ssm, gssm_ref[...]).astype(BF16)
    mix = _dot(a_n, woa_ref[...]) + _dot(s_n, wos_ref[...])
    x = x + m[5:6] * mix
    h = (_rms(x, g2_ref[...]) * (1.0 + m[7:8]) + m[6:7]).astype(BF16)
    x = x + 0.5 * m[8:9] * _swiglu(h, wgu_ref, wd_ref)
    o_ref[0] = _rms(x, gf_ref[...])


def _final(x, attn, u, y_fwd, y_bwd, mod, dsk, wglu, gmla, gssm, woa, wos, g2, wgu, wd, gf, tm):
    nb, n, _ = x.shape
    ssm_spec = pl.BlockSpec((tm, SSM_WIDTH), lambda b, j: (j, b))
    return pl.pallas_call(
        _final_kernel,
        grid=(nb, n // tm),
        in_specs=[pl.BlockSpec((1, tm, D_MODEL), lambda b, j: (b, j, 0)),
                  pl.BlockSpec((1, tm, MLA_WIDTH), lambda b, j: (b, j, 0)),
                  ssm_spec, ssm_spec, ssm_spec,
                  pl.BlockSpec((1, N_MOD, D_MODEL), lambda b, j: (b, 0, 0)),
                  _const_spec((1, SSM_WIDTH)),
                  _const_spec((SSM_WIDTH, 2 * SSM_WIDTH)),
                  _const_spec((1, MLA_WIDTH)),
                  _const_spec((1, SSM_WIDTH)),
                  _const_spec((MLA_WIDTH, D_MODEL)),
                  _const_spec((SSM_WIDTH, D_MODEL)),
                  _const_spec((1, D_MODEL)),
                  _const_spec((D_MODEL, 2 * D_FF)),
                  _const_spec((D_FF, D_MODEL)),
                  _const_spec((1, D_MODEL))],
        out_specs=pl.BlockSpec((1, tm, D_MODEL), lambda b, j: (b, j, 0)),
        out_shape=jax.ShapeDtypeStruct(x.shape, F32),
        compiler_params=_params("parallel", "parallel"),
        name="final",
    )(x, attn, u, y_fwd, y_bwd, mod, dsk, wglu, gmla, gssm, woa, wos, g2, wgu, wd, gf)


def _rot_cols(w):
    wp = w.reshape(w.shape[0], -1, 2)
    return jnp.stack([-wp[..., 1], wp[..., 0]], axis=-1).reshape(w.shape)


def _rope_tables(n_tokens):
    rows = n_tokens // GRID_W
    row = np.repeat(np.arange(rows), GRID_W).astype(np.float32)
    col = np.tile(np.arange(GRID_W), rows).astype(np.float32)
    per_axis = QK_ROPE // 2
    inv_freq = (ROPE_BASE ** (-np.arange(0, per_axis, 2, dtype=np.float32) / per_axis)).astype(np.float32)
    ang = np.concatenate([row[:, None] * inv_freq, col[:, None] * inv_freq], axis=-1)
    cos = np.repeat(np.cos(ang), 2, axis=-1).astype(np.float32)
    sin = np.repeat(np.sin(ang), 2, axis=-1).astype(np.float32)
    return cos, sin


def _head_tables(cos, sin, nope_fill, scale):
    n = cos.shape[0]
    pad = HEAD_PAD - QK_NOPE - QK_ROPE
    c = np.concatenate([np.full((n, QK_NOPE), nope_fill, np.float32), cos, np.zeros((n, pad), np.float32)], axis=-1)
    s = np.concatenate([np.zeros((n, QK_NOPE), np.float32), sin, np.zeros((n, pad), np.float32)], axis=-1)
    return (c * np.float32(scale)).astype(np.float32), (s * np.float32(scale)).astype(np.float32)


def _pad_head_cols(w_nope, w_rope):
    k, h = w_nope.shape[:2]
    pad = HEAD_PAD - QK_NOPE - QK_ROPE
    return jnp.concatenate([w_nope, w_rope, jnp.zeros((k, h, pad), w_nope.dtype)], axis=-1).reshape(k, h * HEAD_PAD)


def kernel(x, c, ctx, c_ctx, w_mod, b_mod, g_ffn1, w_gu1, w_down1, g_mix, w_in, g_cq, w_uq, g_ckv, w_ukv,
           lam_re, lam_im, log_dt, b_re, b_im, c_re, c_im, d_skip, w_glu, g_mla_out, g_ssm_out, w_out,
           g_ffn2, w_gu2, w_down2, g_final):
    bsz, seq, _ = x.shape
    n_ctx = ctx.shape[1]
    assert w_mod.shape[0] == 1, "single-layer block"
    l = 0

    rows = 2 * SUBLANES
    cvec = jnp.concatenate([c, c_ctx[None], jnp.zeros((rows - bsz - 1, D_MODEL), F32)], axis=0)
    mod = _adaln(cvec, w_mod[l], b_mod[l][None]).reshape(rows, N_MOD, D_MODEL)
    ctx_row = bsz

    wgu1 = w_gu1[l].astype(BF16)
    wd1 = w_down1[l].astype(BF16)
    x1 = _ffn(x, mod, g_ffn1[l][None], wgu1, wd1, None, FFN_TM)
    ctx1 = _ffn(ctx.reshape(1, bsz * n_ctx, D_MODEL), mod, g_ffn1[l][None], wgu1, wd1, ctx_row, CTX_FFN_TM)
    ctx1 = ctx1.reshape(bsz, n_ctx, D_MODEL)

    win = w_in[l]
    o_kv = Q_RANK
    o_kr = o_kv + KV_RANK
    o_u = o_kr + QK_ROPE
    w_kr = win[:, o_kr:o_kr + QK_ROPE]
    zk = jnp.zeros((D_MODEL, QK_NOPE), F32)
    zp = jnp.zeros((D_MODEL, HEAD_PAD - QK_NOPE - QK_ROPE), F32)
    wa = jnp.concatenate([win[:, 0:o_kr], zk, w_kr, zp, zk, _rot_cols(w_kr), zp, win[:, o_u:]], axis=-1).astype(BF16)
    wuq = w_uq[l].reshape(Q_RANK, MLA_HEADS, QK_NOPE + QK_ROPE)
    wq_nope, wq_rope = wuq[..., :QK_NOPE], wuq[..., QK_NOPE:]
    wq_rot = _rot_cols(wq_rope.reshape(Q_RANK, -1)).reshape(wq_rope.shape)
    wq = jnp.concatenate([_pad_head_cols(wq_nope, wq_rope),
                          _pad_head_cols(jnp.zeros_like(wq_nope), wq_rot)], axis=-1).astype(BF16)
    wukv = w_ukv[l].reshape(KV_RANK, MLA_HEADS, QK_NOPE + V_HEAD)
    wk = _pad_head_cols(wukv[..., :QK_NOPE], jnp.zeros((KV_RANK, MLA_HEADS, QK_ROPE), F32))
    wv = wukv[..., QK_NOPE:].reshape(KV_RANK, MLA_WIDTH)
    wkv = jnp.concatenate([wk, wv], axis=-1).astype(BF16)

    cos, sin = _rope_tables(seq)
    cq_t, sq_t = _head_tables(cos, sin, 1.0, ATTN_SCALE * math.log2(math.e))
    ck_t, sk_t = _head_tables(cos, sin, 0.0, 1.0)
    one = np.ones((n_ctx, QK_ROPE), np.float32)
    ckc_t, skc_t = _head_tables(one, np.zeros_like(one), 0.0, 1.0)

    gmix = g_mix[l][None]
    q, kx, vx, ux = _proj(x1, mod, gmix, wa, g_cq[l][None], wq, g_ckv[l][None], wkv,
                          (cq_t, sq_t, ck_t, sk_t), None, PROJ_TM, True)
    kc, vc, uc = _proj(ctx1, mod, gmix, wa, g_cq[l][None], wq, g_ckv[l][None], wkv,
                       (ckc_t, skc_t, ckc_t, skc_t), ctx_row, n_ctx, False)

    attn = _attention(q, kx, kc, vx, vc, ATTN_TQ)

    eye = jnp.eye(SSM_GROUPS, dtype=F32)
    lre = lam_re[l].reshape(2, 1, N_STATE)
    lim = lam_im[l].reshape(2, 1, N_STATE)
    ldt = jnp.repeat(log_dt[l], SSM_STATE, axis=-1).reshape(2, 1, N_STATE)
    bre = jnp.einsum('dgpc,gh->dgchp', b_re[l], eye).reshape(2, SSM_WIDTH, N_STATE)
    bim = jnp.einsum('dgpc,gh->dgchp', b_im[l], eye).reshape(2, SSM_WIDTH, N_STATE)
    cre = jnp.einsum('dgcp,gh->dgphc', c_re[l], eye).reshape(2, N_STATE, SSM_WIDTH)
    cim = jnp.einsum('dgcp,gh->dgphc', c_im[l], eye).reshape(2, N_STATE, SSM_WIDTH)
    y_fwd, y_bwd = [_s5(uc, ux, lre[d], lim[d], ldt[d], bre[d], bim[d], cre[d], cim[d], n_ctx, bool(d))
                    for d in range(2)]

    wo = w_out[l].astype(BF16)
    return _final(x1, attn, ux, y_fwd, y_bwd, mod, d_skip[l][None], w_glu[l].astype(BF16), g_mla_out[l][None],
                  g_ssm_out[l][None], wo[:MLA_WIDTH], wo[MLA_WIDTH:], g_ffn2[l][None],
                  w_gu2[l].astype(BF16), w_down2[l].astype(BF16), g_final[None], FINAL_TM)
```

```python
import functools
import math

import jax
import jax.numpy as jnp
import numpy as np
from jax import lax
from jax.experimental import pallas as pl
from jax.experimental.pallas import tpu as pltpu

D_MODEL = 1024
GRID_W = 64
N_MOD = 9
D_FF = 2816
MLA_HEADS = 12
QK_NOPE = 64
QK_ROPE = 32
V_HEAD = 64
Q_RANK = 256
KV_RANK = 128
MLA_WIDTH = MLA_HEADS * V_HEAD
SSM_GROUPS = 16
SSM_CH = 16
SSM_WIDTH = SSM_GROUPS * SSM_CH
SSM_STATE = 64
N_STATE = SSM_GROUPS * SSM_STATE
ROPE_BASE = 10000.0
ATTN_SCALE = (QK_NOPE + QK_ROPE) ** -0.5
EPS = 1e-6

LANES = 128
SUBLANES = 8
MXU_DIM = 256
HEAD_PAD = 128
VMEM_LIMIT = 56 * 1024 * 1024

FFN_TM = 1024
CTX_FFN_TM = 512
PROJ_TM = 1024
ATTN_TQ = 1024
FINAL_TM = 512

F32 = jnp.float32
BF16 = jnp.bfloat16


def _rms(x, g):
    ms = jnp.mean(x * x, axis=-1, keepdims=True)
    return x * lax.rsqrt(ms + EPS) * g


def _dot(a, b):
    return jnp.dot(a, b, preferred_element_type=F32)


def _const_spec(shape):
    zeros = (0,) * len(shape)
    return pl.BlockSpec(shape, lambda *_: zeros, pipeline_mode=pl.Buffered(1))


def _params(*sem):
    return pltpu.CompilerParams(dimension_semantics=sem, vmem_limit_bytes=VMEM_LIMIT)


def _adaln_kernel(c_ref, w_ref, b_ref, o_ref):
    c = c_ref[...]
    s = (c * jax.nn.sigmoid(c)).astype(BF16)
    o_ref[...] = _dot(s, w_ref[...].astype(BF16)) + b_ref[...]


def _adaln(cvec, w_mod, b_mod):
    rows = cvec.shape[0]
    n = w_mod.shape[1]
    bn = D_MODEL
    return pl.pallas_call(
        _adaln_kernel,
        grid=(n // bn,),
        in_specs=[pl.BlockSpec((rows, D_MODEL), lambda i: (0, 0)),
                  pl.BlockSpec((D_MODEL, bn), lambda i: (0, i)),
                  pl.BlockSpec((1, bn), lambda i: (0, i))],
        out_specs=pl.BlockSpec((rows, bn), lambda i: (0, i)),
        out_shape=jax.ShapeDtypeStruct((rows, n), F32),
        compiler_params=_params("parallel"),
        name="adaln",
    )(cvec, w_mod, b_mod)


FF_TILES = D_FF // MXU_DIM
FF_SPLITS = (0, (FF_TILES + 1) // 2 * MXU_DIM, D_FF)


def _swiglu(h, wgu_ref, wd_ref):
    acc = None
    for lo, hi in zip(FF_SPLITS[:-1], FF_SPLITS[1:]):
        gate = _dot(h, wgu_ref[:, lo:hi])
        up = _dot(h, wgu_ref[:, D_FF + lo:D_FF + hi])
        act = (gate * jax.nn.sigmoid(gate) * up).astype(BF16)
        part = _dot(act, wd_ref[lo:hi, :])
        acc = part if acc is None else acc + part
    return acc


def _ffn_kernel(x_ref, mod_ref, g_ref, wgu_ref, wd_ref, o_ref):
    x = x_ref[0]
    m = mod_ref[0]
    h = (_rms(x, g_ref[...]) * (1.0 + m[1:2]) + m[0:1]).astype(BF16)
    o_ref[0] = x + 0.5 * m[2:3] * _swiglu(h, wgu_ref, wd_ref)


def _ffn(x, mod, g, wgu, wd, mod_row, tm):
    nb, n, _ = x.shape
    mod_map = (lambda b, j: (b, 0, 0)) if mod_row is None else (lambda b, j: (mod_row, 0, 0))
    return pl.pallas_call(
        _ffn_kernel,
        grid=(nb, n // tm),
        in_specs=[pl.BlockSpec((1, tm, D_MODEL), lambda b, j: (b, j, 0)),
                  pl.BlockSpec((1, N_MOD, D_MODEL), mod_map),
                  _const_spec((1, D_MODEL)),
                  _const_spec((D_MODEL, 2 * D_FF)),
                  _const_spec((D_FF, D_MODEL))],
        out_specs=pl.BlockSpec((1, tm, D_MODEL), lambda b, j: (b, j, 0)),
        out_shape=jax.ShapeDtypeStruct(x.shape, F32),
        compiler_params=_params("parallel", "parallel"),
        name="ffn",
    )(x, mod, g, wgu, wd)


W_A_COLS = Q_RANK + KV_RANK + 2 * HEAD_PAD + SSM_WIDTH
QK_WIDTH = MLA_HEADS * HEAD_PAD


def _proj_kernel(x_ref, mod_ref, g_ref, wa_ref, gcq_ref, wq_ref, gckv_ref, wkv_ref,
                 cq_ref, sq_ref, ck_ref, sk_ref, *out_refs, with_q):
    if with_q:
        q_ref, k_ref, v_ref, u_ref = out_refs
    else:
        k_ref, v_ref, u_ref = out_refs
    x = x_ref[0]
    m = mod_ref[0]
    h = (_rms(x, g_ref[...]) * (1.0 + m[4:5]) + m[3:4]).astype(BF16)
    p = _dot(h, wa_ref[...])
    o_kv = Q_RANK
    o_kr = o_kv + KV_RANK
    o_rot = o_kr + HEAD_PAD
    o_u = o_rot + HEAD_PAD
    u_ref[...] = p[:, o_u:o_u + SSM_WIDTH]
    kr = p[:, o_kr:o_kr + HEAD_PAD] * ck_ref[...] + p[:, o_rot:o_rot + HEAD_PAD] * sk_ref[...]
    ckv = _rms(p[:, o_kv:o_kv + KV_RANK], gckv_ref[...]).astype(BF16)
    kv = _dot(ckv, wkv_ref[...])
    for hd in range(MLA_HEADS):
        lo = hd * HEAD_PAD
        k_ref[0, :, lo:lo + HEAD_PAD] = (kv[:, lo:lo + HEAD_PAD] + kr).astype(BF16)
    v_ref[0] = kv[:, QK_WIDTH:QK_WIDTH + MLA_WIDTH].astype(BF16)
    if with_q:
        cq = _rms(p[:, 0:Q_RANK], gcq_ref[...]).astype(BF16)
        qa = _dot(cq, wq_ref[:, 0:QK_WIDTH])
        qb = _dot(cq, wq_ref[:, QK_WIDTH:2 * QK_WIDTH])
        cq_t = cq_ref[...]
        sq_t = sq_ref[...]
        for hd in range(MLA_HEADS):
            lo = hd * HEAD_PAD
            q_ref[0, :, lo:lo + HEAD_PAD] = (qa[:, lo:lo + HEAD_PAD] * cq_t
                                             + qb[:, lo:lo + HEAD_PAD] * sq_t).astype(BF16)


def _proj(x, mod, g_mix, wa, gcq, wq, gckv, wkv, tabs, mod_row, tm, with_q):
    nb, n, _ = x.shape
    mod_map = (lambda b, j: (b, 0, 0)) if mod_row is None else (lambda b, j: (mod_row, 0, 0))
    tab_spec = pl.BlockSpec((tm, HEAD_PAD), lambda b, j: (j, 0))
    out_shape = [jax.ShapeDtypeStruct((nb, n, QK_WIDTH), BF16),
                 jax.ShapeDtypeStruct((nb, n, MLA_WIDTH), BF16),
                 jax.ShapeDtypeStruct((n, nb * SSM_WIDTH), F32)]
    out_specs = [pl.BlockSpec((1, tm, QK_WIDTH), lambda b, j: (b, j, 0)),
                 pl.BlockSpec((1, tm, MLA_WIDTH), lambda b, j: (b, j, 0)),
                 pl.BlockSpec((tm, SSM_WIDTH), lambda b, j: (j, b))]
    if with_q:
        out_shape = [jax.ShapeDtypeStruct((nb, n, QK_WIDTH), BF16)] + out_shape
        out_specs = [pl.BlockSpec((1, tm, QK_WIDTH), lambda b, j: (b, j, 0))] + out_specs
    return pl.pallas_call(
        functools.partial(_proj_kernel, with_q=with_q),
        grid=(nb, n // tm),
        in_specs=[pl.BlockSpec((1, tm, D_MODEL), lambda b, j: (b, j, 0)),
                  pl.BlockSpec((1, N_MOD, D_MODEL), mod_map),
                  _const_spec((1, D_MODEL)),
                  _const_spec(wa.shape),
                  _const_spec((1, Q_RANK)),
                  _const_spec(wq.shape),
                  _const_spec((1, KV_RANK)),
                  _const_spec(wkv.shape),
                  tab_spec, tab_spec, tab_spec, tab_spec],
        out_specs=out_specs,
        out_shape=out_shape,
        compiler_params=_params("parallel", "parallel"),
        name="proj_q" if with_q else "proj_ctx",
    )(x, mod, g_mix, wa, gcq, wq, gckv, wkv, *tabs)


HEADS_PER_STEP = 4
KEY_SPLITS = (0, 1024, 2048, 3072, 4096)


def _attn_kernel(q_ref, kx_ref, kc_ref, vx_ref, vc_ref, o_ref):
    nt = (((1,), (1,)), ((), ()))
    tq = q_ref.shape[1]
    assert KEY_SPLITS[-1] == kx_ref.shape[1]
    chunks = [(kx_ref, vx_ref, lo, hi - lo) for lo, hi in zip(KEY_SPLITS[:-1], KEY_SPLITS[1:])]
    chunks.append((kc_ref, vc_ref, 0, kc_ref.shape[1]))
    outs = []
    for hh in range(HEADS_PER_STEP):
        lo = hh * HEAD_PAD
        q = q_ref[0, :, lo:lo + HEAD_PAD]
        m_b = acc = None
        for k_ref, v_ref, start, size in chunks:
            s = lax.dot_general(q, k_ref[0, start:start + size, lo:lo + HEAD_PAD], nt,
                                preferred_element_type=F32)
            cols = [s[:, i * LANES:(i + 1) * LANES] for i in range(size // LANES)]
            c_vec = functools.reduce(jnp.maximum, cols)
            c_max = jnp.broadcast_to(jnp.max(c_vec, axis=-1, keepdims=True), (tq, LANES))
            m_new = c_max if m_b is None else jnp.maximum(m_b, c_max)
            p = jnp.concatenate([jnp.exp2(col - m_new).astype(BF16) for col in cols], axis=-1)
            v_lo = (hh // 2) * 2 * V_HEAD
            v_ext = jnp.concatenate([v_ref[0, start:start + size, v_lo:v_lo + 2 * V_HEAD],
                                     jnp.ones((size, LANES), BF16)], axis=-1)
            pv = _dot(p, v_ext)
            if m_b is None:
                acc = pv
            else:
                alpha = jnp.exp2(m_b - m_new)
                acc = jnp.concatenate([alpha, alpha], axis=-1) * acc + pv
            m_b = m_new
        outs.append(acc[:, 0:LANES] / acc[:, LANES:2 * LANES])
    lane = lax.broadcasted_iota(jnp.int32, outs[0].shape, 1)
    for pr in range(HEADS_PER_STEP // 2):
        o_ref[0, :, pr * 2 * V_HEAD:(pr + 1) * 2 * V_HEAD] = jnp.where(lane < V_HEAD, outs[2 * pr], outs[2 * pr + 1])


def _attention(q, kx, kc, vx, vc, tq):
    nb, n, _ = q.shape
    nk = kx.shape[1]
    nc = kc.shape[1]
    pairs = MLA_HEADS // HEADS_PER_STEP
    qk_blk = HEADS_PER_STEP * HEAD_PAD
    v_blk = HEADS_PER_STEP * V_HEAD
    return pl.pallas_call(
        _attn_kernel,
        grid=(nb, pairs, n // tq),
        in_specs=[pl.BlockSpec((1, tq, qk_blk), lambda b, p, j: (b, j, p)),
                  pl.BlockSpec((1, nk, qk_blk), lambda b, p, j: (b, 0, p)),
                  pl.BlockSpec((1, nc, qk_blk), lambda b, p, j: (b, 0, p)),
                  pl.BlockSpec((1, nk, v_blk), lambda b, p, j: (b, 0, p)),
                  pl.BlockSpec((1, nc, v_blk), lambda b, p, j: (b, 0, p))],
        out_specs=pl.BlockSpec((1, tq, v_blk), lambda b, p, j: (b, j, p)),
        out_shape=jax.ShapeDtypeStruct((nb, n, MLA_WIDTH), F32),
        compiler_params=_params("parallel", "parallel", "parallel"),
        name="attention",
    )(q, kx, kc, vx, vc)


SCAN_BLOCK = 32


def _s5_kernel(uc_ref, ux_ref, lre_ref, lim_ref, ldt_ref, bre_ref, bim_ref, cre_ref, cim_ref,
               y_ref, bb_ref, cm_ref, a_ref, h_ref, bd_ref, *, t_chunk, reverse):
    j = pl.program_id(0)

    @pl.when(j == 0)
    def _discretize():
        dt = jnp.exp(ldt_ref[...])
        lr = jnp.minimum(lre_ref[...], -1e-4)
        li = lim_ref[...]
        mag = jnp.exp(lr * dt)
        ar = mag * jnp.cos(li * dt)
        ai = mag * jnp.sin(li * dt)
        den = lr * lr + li * li
        fr = ((ar - 1.0) * lr + ai * li) / den
        fi = (ai * lr - (ar - 1.0) * li) / den
        same_group = (lax.broadcasted_iota(jnp.int32, (SSM_WIDTH, N_STATE), 0) // SSM_CH
                      == lax.broadcasted_iota(jnp.int32, (SSM_WIDTH, N_STATE), 1) // SSM_STATE)

        def block_diag(ref):
            return jnp.where(same_group, jnp.concatenate([ref[...]] * SSM_GROUPS, axis=-1), 0.0)

        bre = block_diag(bre_ref)
        bim = block_diag(bim_ref)
        bb_ref[:, 0:N_STATE] = (fr * bre - fi * bim).astype(BF16)
        bb_ref[:, N_STATE:2 * N_STATE] = (fr * bim + fi * bre).astype(BF16)
        cm_ref[0:N_STATE, :] = block_diag(cre_ref).T.astype(BF16)
        cm_ref[N_STATE:2 * N_STATE, :] = (-block_diag(cim_ref)).T.astype(BF16)
        a_ref[0:SUBLANES, :] = jnp.broadcast_to(ar, (SUBLANES, N_STATE))
        a_ref[SUBLANES:2 * SUBLANES, :] = jnp.broadcast_to(ai, (SUBLANES, N_STATE))
        h_ref[...] = jnp.zeros_like(h_ref)

    blk_rows = SCAN_BLOCK * SUBLANES
    order = list(range(t_chunk // SCAN_BLOCK))
    steps = list(range(SCAN_BLOCK))
    if reverse:
        order.reverse()
        steps.reverse()

    def run(u_ref, with_readout):
        ar = a_ref[0:SUBLANES, :]
        ai = a_ref[SUBLANES:2 * SUBLANES, :]

        def drive(sb):
            u = u_ref[sb * SCAN_BLOCK:(sb + 1) * SCAN_BLOCK, :].reshape(blk_rows, SSM_WIDTH)
            bd_ref[sb * blk_rows:(sb + 1) * blk_rows, :] = _dot(u.astype(BF16), bb_ref[...])

        def scan(sb, hr, hi):
            for t in steps:
                r = sb * blk_rows + t * SUBLANES
                nr = ar * hr - ai * hi + bd_ref[r:r + SUBLANES, 0:N_STATE]
                ni = ar * hi + ai * hr + bd_ref[r:r + SUBLANES, N_STATE:2 * N_STATE]
                bd_ref[r:r + SUBLANES, 0:N_STATE] = nr
                bd_ref[r:r + SUBLANES, N_STATE:2 * N_STATE] = ni
                hr, hi = nr, ni
            return hr, hi

        def readout(sb):
            y = _dot(bd_ref[sb * blk_rows:(sb + 1) * blk_rows, :].astype(BF16), cm_ref[...])
            y_ref[sb * SCAN_BLOCK:(sb + 1) * SCAN_BLOCK, :] = y.reshape(SCAN_BLOCK, SUBLANES * SSM_WIDTH)

        hr = h_ref[0:SUBLANES, :]
        hi = h_ref[SUBLANES:2 * SUBLANES, :]
        drive(order[0])
        for k, sb in enumerate(order):
            if k + 1 < len(order):
                drive(order[k + 1])
            hr, hi = scan(sb, hr, hi)
            if with_readout and k >= 1:
                readout(order[k - 1])
        if with_readout:
            readout(order[-1])
        h_ref[0:SUBLANES, :] = hr
        h_ref[SUBLANES:2 * SUBLANES, :] = hi

    @pl.when(j == 0)
    def _context():
        run(uc_ref, False)

    @pl.when(j > 0)
    def _latent():
        run(ux_ref, True)


def _s5(u_c, u_x, lre, lim, ldt, bre, bim, cre, cim, t_chunk, reverse):
    rows = t_chunk * SUBLANES
    width = SUBLANES * SSM_WIDTH
    assert u_c.shape == (t_chunk, width) and u_x.shape[1] == width
    nch = u_x.shape[0] // t_chunk

    def chunk(j):
        jj = jnp.maximum(j - 1, 0)
        return nch - 1 - jj if reverse else jj

    d = int(reverse)
    vec_spec = pl.BlockSpec((None, 1, N_STATE), lambda j: (d, 0, 0), pipeline_mode=pl.Buffered(1))
    mat_spec = pl.BlockSpec((None, SSM_WIDTH, SSM_STATE), lambda j: (d, 0, 0), pipeline_mode=pl.Buffered(1))
    return pl.pallas_call(
        functools.partial(_s5_kernel, t_chunk=t_chunk, reverse=reverse),
        grid=(nch + 1,),
        in_specs=[pl.BlockSpec((t_chunk, width), lambda j: (0, 0)),
                  pl.BlockSpec((t_chunk, width), lambda j: (chunk(j), 0)),
                  vec_spec, vec_spec, vec_spec, mat_spec, mat_spec, mat_spec, mat_spec],
        out_specs=pl.BlockSpec((t_chunk, width), lambda j: (chunk(j), 0)),
        out_shape=jax.ShapeDtypeStruct((u_x.shape[0], width), F32),
        scratch_shapes=[pltpu.VMEM((SSM_WIDTH, 2 * N_STATE), BF16),
                        pltpu.VMEM((2 * N_STATE, SSM_WIDTH), BF16),
                        pltpu.VMEM((2 * SUBLANES, N_STATE), F32),
                        pltpu.VMEM((2 * SUBLANES, N_STATE), F32),
                        pltpu.VMEM((rows, 2 * N_STATE), F32)],
        compiler_params=_params("arbitrary"),
        name="s5_rev" if reverse else "s5_fwd",
    )(u_c, u_x, lre, lim, ldt, bre, bim, cre, cim)


def _gelu_tanh(x):
    return 0.5 * x * (1.0 + jnp.tanh(math.sqrt(2.0 / math.pi) * (x + 0.044715 * (x * x * x))))


def _final_kernel(x_ref, attn_ref, u_ref, yf_ref, yb_ref, mod_ref, dsk_ref, wglu_ref, gmla_ref, gssm_ref,
                  wo_ref, g2_ref, wgu_ref, wd_ref, gf_ref, o_ref):
    x = x_ref[0]
    m = mod_ref[0]
    y = u_ref[...] * dsk_ref[...] + yf_ref[...] + yb_ref[...]
    z = _gelu_tanh(y).astype(BF16)
    ag = _dot(z, wglu_ref[...])
    ssm = ag[:, 0:SSM_WIDTH] * jax.nn.sigmoid(ag[:, SSM_WIDTH:2 * SSM_WIDTH])
    a_n = _rms(attn_ref[0], gmla_ref[...]).astype(BF16)
    s_n = _rms(ssm, gssm_ref[...]).astype(BF16)
    mix = _dot(a_n, wo_ref[0:MLA_WIDTH, :]) + _dot(s_n, wo_ref[MLA_WIDTH:MLA_WIDTH + SSM_WIDTH, :])
    x = x + m[5:6] * mix
    h = (_rms(x, g2_ref[...]) * (1.0 + m[7:8]) + m[6:7]).astype(BF16)
    x = x + 0.5 * m[8:9] * _swiglu(h, wgu_ref, wd_ref)
    o_ref[0] = _rms(x, gf_ref[...])


def _final(x, attn, u, y_fwd, y_bwd, mod, dsk, wglu, gmla, gssm, wo, g2, wgu, wd, gf, tm):
    nb, n, _ = x.shape
    ssm_spec = pl.BlockSpec((tm, SSM_WIDTH), lambda b, j: (j, b))
    return pl.pallas_call(
        _final_kernel,
        grid=(nb, n // tm),
        in_specs=[pl.BlockSpec((1, tm, D_MODEL), lambda b, j: (b, j, 0)),
                  pl.BlockSpec((1, tm, MLA_WIDTH), lambda b, j: (b, j, 0)),
                  ssm_spec, ssm_spec, ssm_spec,
                  pl.BlockSpec((1, N_MOD, D_MODEL), lambda b, j: (b, 0, 0)),
                  _const_spec((1, SSM_WIDTH)),
                  _const_spec((SSM_WIDTH, 2 * SSM_WIDTH)),
                  _const_spec((1, MLA_WIDTH)),
                  _const_spec((1, SSM_WIDTH)),
                  _const_spec((MLA_WIDTH + SSM_WIDTH, D_MODEL)),
                  _const_spec((1, D_MODEL)),
                  _const_spec((D_MODEL, 2 * D_FF)),
                  _const_spec((D_FF, D_MODEL)),
                  _const_spec((1, D_MODEL))],
        out_specs=pl.BlockSpec((1, tm, D_MODEL), lambda b, j: (b, j, 0)),
        out_shape=jax.ShapeDtypeStruct(x.shape, F32),
        compiler_params=_params("parallel", "parallel"),
        name="final",
    )(x, attn, u, y_fwd, y_bwd, mod, dsk, wglu, gmla, gssm, wo, g2, wgu, wd, gf)


def _rot_cols(w):
    wp = w.reshape(w.shape[0], -1, 2)
    return jnp.stack([-wp[..., 1], wp[..., 0]], axis=-1).reshape(w.shape)


def _rope_tables(n_tokens):
    rows = n_tokens // GRID_W
    row = np.repeat(np.arange(rows), GRID_W).astype(np.float32)
    col = np.tile(np.arange(GRID_W), rows).astype(np.float32)
    per_axis = QK_ROPE // 2
    inv_freq = (ROPE_BASE ** (-np.arange(0, per_axis, 2, dtype=np.float32) / per_axis)).astype(np.float32)
    ang = np.concatenate([row[:, None] * inv_freq, col[:, None] * inv_freq], axis=-1)
    cos = np.repeat(np.cos(ang), 2, axis=-1).astype(np.float32)
    sin = np.repeat(np.sin(ang), 2, axis=-1).astype(np.float32)
    return cos, sin


def _head_tables(cos, sin, nope_fill, scale):
    n = cos.shape[0]
    pad = HEAD_PAD - QK_NOPE - QK_ROPE
    c = np.concatenate([np.full((n, QK_NOPE), nope_fill, np.float32), cos, np.zeros((n, pad), np.float32)], axis=-1)
    s = np.concatenate([np.zeros((n, QK_NOPE), np.float32), sin, np.zeros((n, pad), np.float32)], axis=-1)
    return (c * np.float32(scale)).astype(np.float32), (s * np.float32(scale)).astype(np.float32)


def _pad_head_cols(w_nope, w_rope):
    k, h = w_nope.shape[:2]
    pad = HEAD_PAD - QK_NOPE - QK_ROPE
    return jnp.concatenate([w_nope, w_rope, jnp.zeros((k, h, pad), w_nope.dtype)], axis=-1).reshape(k, h * HEAD_PAD)


def kernel(x, c, ctx, c_ctx, w_mod, b_mod, g_ffn1, w_gu1, w_down1, g_mix, w_in, g_cq, w_uq, g_ckv, w_ukv,
           lam_re, lam_im, log_dt, b_re, b_im, c_re, c_im, d_skip, w_glu, g_mla_out, g_ssm_out, w_out,
           g_ffn2, w_gu2, w_down2, g_final):
    bsz, seq, _ = x.shape
    n_ctx = ctx.shape[1]
    assert w_mod.shape[0] == 1, "single-layer block"

    def layer(a):
        return a.reshape(a.shape[1:])

    rows = 2 * SUBLANES
    cvec = jnp.concatenate([c, c_ctx[None], jnp.zeros((rows - bsz - 1, D_MODEL), F32)], axis=0)
    mod = _adaln(cvec, layer(w_mod), b_mod).reshape(rows, N_MOD, D_MODEL)
    ctx_row = bsz

    wgu1 = layer(w_gu1).astype(BF16)
    wd1 = layer(w_down1).astype(BF16)
    x1 = _ffn(x, mod, g_ffn1, wgu1, wd1, None, FFN_TM)
    ctx1 = _ffn(ctx.reshape(1, bsz * n_ctx, D_MODEL), mod, g_ffn1, wgu1, wd1, ctx_row, CTX_FFN_TM)
    ctx1 = ctx1.reshape(bsz, n_ctx, D_MODEL)

    win = layer(w_in)
    o_kv = Q_RANK
    o_kr = o_kv + KV_RANK
    o_u = o_kr + QK_ROPE
    w_kr = win[:, o_kr:o_kr + QK_ROPE]
    zk = jnp.zeros((D_MODEL, QK_NOPE), F32)
    zp = jnp.zeros((D_MODEL, HEAD_PAD - QK_NOPE - QK_ROPE), F32)
    wa = jnp.concatenate([win[:, 0:o_kr], zk, w_kr, zp, zk, _rot_cols(w_kr), zp, win[:, o_u:]], axis=-1).astype(BF16)
    wuq = w_uq.reshape(Q_RANK, MLA_HEADS, QK_NOPE + QK_ROPE)
    wq_nope, wq_rope = wuq[..., :QK_NOPE], wuq[..., QK_NOPE:]
    wq_rot = _rot_cols(wq_rope.reshape(Q_RANK, -1)).reshape(wq_rope.shape)
    wq = jnp.concatenate([_pad_head_cols(wq_nope, wq_rope),
                          _pad_head_cols(jnp.zeros_like(wq_nope), wq_rot)], axis=-1).astype(BF16)
    wukv = w_ukv.reshape(KV_RANK, MLA_HEADS, QK_NOPE + V_HEAD)
    wk = _pad_head_cols(wukv[..., :QK_NOPE], jnp.zeros((KV_RANK, MLA_HEADS, QK_ROPE), F32))
    wv = wukv[..., QK_NOPE:].reshape(KV_RANK, MLA_WIDTH)
    wkv = jnp.concatenate([wk, wv], axis=-1).astype(BF16)

    cos, sin = _rope_tables(seq)
    cq_t, sq_t = _head_tables(cos, sin, 1.0, ATTN_SCALE * math.log2(math.e))
    ck_t, sk_t = _head_tables(cos, sin, 0.0, 1.0)
    one = np.ones((n_ctx, QK_ROPE), np.float32)
    ckc_t, skc_t = _head_tables(one, np.zeros_like(one), 0.0, 1.0)

    q, kx, vx, ux = _proj(x1, mod, g_mix, wa, g_cq, wq, g_ckv, wkv,
                          (cq_t, sq_t, ck_t, sk_t), None, PROJ_TM, True)
    kc, vc, uc = _proj(ctx1, mod, g_mix, wa, g_cq, wq, g_ckv, wkv,
                       (ckc_t, skc_t, ckc_t, skc_t), ctx_row, n_ctx, False)

    attn = _attention(q, kx, kc, vx, vc, ATTN_TQ)

    lre = lam_re.reshape(2, 1, N_STATE)
    lim = lam_im.reshape(2, 1, N_STATE)
    ldt = jnp.repeat(log_dt.reshape(2, SSM_GROUPS), SSM_STATE, axis=-1).reshape(2, 1, N_STATE)
    bre = jnp.swapaxes(b_re.reshape(2, SSM_GROUPS, SSM_STATE, SSM_CH), -1, -2).reshape(2, SSM_WIDTH, SSM_STATE)
    bim = jnp.swapaxes(b_im.reshape(2, SSM_GROUPS, SSM_STATE, SSM_CH), -1, -2).reshape(2, SSM_WIDTH, SSM_STATE)
    cre = c_re.reshape(2, SSM_WIDTH, SSM_STATE)
    cim = c_im.reshape(2, SSM_WIDTH, SSM_STATE)
    y_fwd, y_bwd = [_s5(uc, ux, lre, lim, ldt, bre, bim, cre, cim, n_ctx, rev) for rev in (False, True)]

    return _final(x1, attn, ux, y_fwd, y_bwd, mod, d_skip, layer(w_glu).astype(BF16), g_mla_out, g_ssm_out,
                  layer(w_out).astype(BF16), g_ffn2, layer(w_gu2).astype(BF16), layer(w_down2).astype(BF16),
                  g_final[None], FINAL_TM)
```

```python
import functools
import math

import jax
import jax.numpy as jnp
import numpy as np
from jax import lax
from jax.experimental import pallas as pl
from jax.experimental.pallas import tpu as pltpu

D_MODEL = 1024
GRID_W = 64
N_MOD = 9
D_FF = 2816
MLA_HEADS = 12
QK_NOPE = 64
QK_ROPE = 32
V_HEAD = 64
Q_RANK = 256
KV_RANK = 128
MLA_WIDTH = MLA_HEADS * V_HEAD
SSM_GROUPS = 16
SSM_CH = 16
SSM_WIDTH = SSM_GROUPS * SSM_CH
SSM_STATE = 64
N_STATE = SSM_GROUPS * SSM_STATE
ROPE_BASE = 10000.0
ATTN_SCALE = (QK_NOPE + QK_ROPE) ** -0.5
EPS = 1e-6

LANES = 128
SUBLANES = 8
MXU_DIM = 256
HEAD_PAD = 128
VMEM_LIMIT = 56 * 1024 * 1024

FFN_TM = 1024
CTX_FFN_TM = 512
PROJ_TM = 1024
ATTN_TQ = 1024
FINAL_TM = 512

F32 = jnp.float32
BF16 = jnp.bfloat16


def _rms(x, g):
    ms = jnp.mean(x * x, axis=-1, keepdims=True)
    return x * lax.rsqrt(ms + EPS) * g


def _dot(a, b):
    return jnp.dot(a, b, preferred_element_type=F32)


def _const_spec(shape):
    zeros = (0,) * len(shape)
    return pl.BlockSpec(shape, lambda *_: zeros, pipeline_mode=pl.Buffered(1))


def _params(*sem):
    return pltpu.CompilerParams(dimension_semantics=sem, vmem_limit_bytes=VMEM_LIMIT)


def _adaln_kernel(c_ref, w_ref, b_ref, o_ref):
    c = c_ref[...]
    s = (c * jax.nn.sigmoid(c)).astype(BF16)
    o_ref[...] = _dot(s, w_ref[...].astype(BF16)) + b_ref[...]


def _adaln(cvec, w_mod, b_mod):
    rows = cvec.shape[0]
    n = w_mod.shape[1]
    bn = D_MODEL
    return pl.pallas_call(
        _adaln_kernel,
        grid=(n // bn,),
        in_specs=[pl.BlockSpec((rows, D_MODEL), lambda i: (0, 0)),
                  pl.BlockSpec((D_MODEL, bn), lambda i: (0, i)),
                  pl.BlockSpec((1, bn), lambda i: (0, i))],
        out_specs=pl.BlockSpec((rows, bn), lambda i: (0, i)),
        out_shape=jax.ShapeDtypeStruct((rows, n), F32),
        compiler_params=_params("parallel"),
        name="adaln",
    )(cvec, w_mod, b_mod)


FF_TILES = D_FF // MXU_DIM
FF_SPLITS = (0, (FF_TILES + 1) // 2 * MXU_DIM, D_FF)


def _swiglu(h, wgu_ref, wd_ref):
    acc = None
    for lo, hi in zip(FF_SPLITS[:-1], FF_SPLITS[1:]):
        gate = _dot(h, wgu_ref[:, lo:hi])
        up = _dot(h, wgu_ref[:, D_FF + lo:D_FF + hi])
        act = (gate * jax.nn.sigmoid(gate) * up).astype(BF16)
        part = _dot(act, wd_ref[lo:hi, :])
        acc = part if acc is None else acc + part
    return acc


def _ffn_kernel(x_ref, mod_ref, g_ref, wgu_ref, wd_ref, o_ref):
    x = x_ref[0]
    m = mod_ref[0]
    h = (_rms(x, g_ref[...]) * (1.0 + m[1:2]) + m[0:1]).astype(BF16)
    o_ref[0] = x + 0.5 * m[2:3] * _swiglu(h, wgu_ref, wd_ref)


def _ffn(x, mod, g, wgu, wd, mod_row, tm):
    nb, n, _ = x.shape
    mod_map = (lambda b, j: (b, 0, 0)) if mod_row is None else (lambda b, j: (mod_row, 0, 0))
    return pl.pallas_call(
        _ffn_kernel,
        grid=(nb, n // tm),
        in_specs=[pl.BlockSpec((1, tm, D_MODEL), lambda b, j: (b, j, 0)),
                  pl.BlockSpec((1, N_MOD, D_MODEL), mod_map),
                  _const_spec((1, D_MODEL)),
                  _const_spec((D_MODEL, 2 * D_FF)),
                  _const_spec((D_FF, D_MODEL))],
        out_specs=pl.BlockSpec((1, tm, D_MODEL), lambda b, j: (b, j, 0)),
        out_shape=jax.ShapeDtypeStruct(x.shape, F32),
        compiler_params=_params("parallel", "parallel"),
        name="ffn",
    )(x, mod, g, wgu, wd)


W_A_COLS = Q_RANK + KV_RANK + 2 * HEAD_PAD + SSM_WIDTH
QK_WIDTH = MLA_HEADS * HEAD_PAD


def _proj_kernel(x_ref, mod_ref, g_ref, wa_ref, gcq_ref, wq_ref, gckv_ref, wkv_ref,
                 cq_ref, sq_ref, ck_ref, sk_ref, *out_refs, with_q):
    if with_q:
        q_ref, k_ref, v_ref, u_ref = out_refs
    else:
        k_ref, v_ref, u_ref = out_refs
    x = x_ref[0]
    m = mod_ref[0]
    h = (_rms(x, g_ref[...]) * (1.0 + m[4:5]) + m[3:4]).astype(BF16)
    p = _dot(h, wa_ref[...])
    o_kv = Q_RANK
    o_kr = o_kv + KV_RANK
    o_rot = o_kr + HEAD_PAD
    o_u = o_rot + HEAD_PAD
    u_ref[...] = p[:, o_u:o_u + SSM_WIDTH]
    kr = p[:, o_kr:o_kr + HEAD_PAD] * ck_ref[...] + p[:, o_rot:o_rot + HEAD_PAD] * sk_ref[...]
    ckv = _rms(p[:, o_kv:o_kv + KV_RANK], gckv_ref[...]).astype(BF16)
    kv = _dot(ckv, wkv_ref[...])
    for hd in range(MLA_HEADS):
        lo = hd * HEAD_PAD
        k_ref[0, :, lo:lo + HEAD_PAD] = (kv[:, lo:lo + HEAD_PAD] + kr).astype(BF16)
    v_ref[0] = kv[:, QK_WIDTH:QK_WIDTH + MLA_WIDTH].astype(BF16)
    if with_q:
        cq = _rms(p[:, 0:Q_RANK], gcq_ref[...]).astype(BF16)
        qa = _dot(cq, wq_ref[:, 0:QK_WIDTH])
        qb = _dot(cq, wq_ref[:, QK_WIDTH:2 * QK_WIDTH])
        cq_t = cq_ref[...]
        sq_t = sq_ref[...]
        for hd in range(MLA_HEADS):
            lo = hd * HEAD_PAD
            q_ref[0, :, lo:lo + HEAD_PAD] = (qa[:, lo:lo + HEAD_PAD] * cq_t
                                             + qb[:, lo:lo + HEAD_PAD] * sq_t).astype(BF16)


def _proj(x, mod, g_mix, wa, gcq, wq, gckv, wkv, tabs, mod_row, tm, with_q):
    nb, n, _ = x.shape
    mod_map = (lambda b, j: (b, 0, 0)) if mod_row is None else (lambda b, j: (mod_row, 0, 0))
    tab_spec = pl.BlockSpec((tm, HEAD_PAD), lambda b, j: (j, 0))
    out_shape = [jax.ShapeDtypeStruct((nb, n, QK_WIDTH), BF16),
                 jax.ShapeDtypeStruct((nb, n, MLA_WIDTH), BF16),
                 jax.ShapeDtypeStruct((n, nb * SSM_WIDTH), F32)]
    out_specs = [pl.BlockSpec((1, tm, QK_WIDTH), lambda b, j: (b, j, 0)),
                 pl.BlockSpec((1, tm, MLA_WIDTH), lambda b, j: (b, j, 0)),
                 pl.BlockSpec((tm, SSM_WIDTH), lambda b, j: (j, b))]
    if with_q:
        out_shape = [jax.ShapeDtypeStruct((nb, n, QK_WIDTH), BF16)] + out_shape
        out_specs = [pl.BlockSpec((1, tm, QK_WIDTH), lambda b, j: (b, j, 0))] + out_specs
    return pl.pallas_call(
        functools.partial(_proj_kernel, with_q=with_q),
        grid=(nb, n // tm),
        in_specs=[pl.BlockSpec((1, tm, D_MODEL), lambda b, j: (b, j, 0)),
                  pl.BlockSpec((1, N_MOD, D_MODEL), mod_map),
                  _const_spec((1, D_MODEL)),
                  _const_spec(wa.shape),
                  _const_spec((1, Q_RANK)),
                  _const_spec(wq.shape),
                  _const_spec((1, KV_RANK)),
                  _const_spec(wkv.shape),
                  tab_spec, tab_spec, tab_spec, tab_spec],
        out_specs=out_specs,
        out_shape=out_shape,
        compiler_params=_params("parallel", "parallel"),
        name="proj_q" if with_q else "proj_ctx",
    )(x, mod, g_mix, wa, gcq, wq, gckv, wkv, *tabs)


HEADS_PER_STEP = 6
KEY_SPLITS = (0, 1024, 2048, 3072, 4096)


def _attn_kernel(q_ref, kx_ref, kc_ref, vx_ref, vc_ref, o_ref):
    nt = (((1,), (1,)), ((), ()))
    tq = q_ref.shape[1]
    assert KEY_SPLITS[-1] == kx_ref.shape[1]
    chunks = [(kx_ref, vx_ref, lo, hi - lo) for lo, hi in zip(KEY_SPLITS[:-1], KEY_SPLITS[1:])]
    chunks.append((kc_ref, vc_ref, 0, kc_ref.shape[1]))
    outs = []
    for hh in range(HEADS_PER_STEP):
        lo = hh * HEAD_PAD
        q = q_ref[0, :, lo:lo + HEAD_PAD]
        m_b = acc = None
        for k_ref, v_ref, start, size in chunks:
            s = lax.dot_general(q, k_ref[0, start:start + size, lo:lo + HEAD_PAD], nt,
                                preferred_element_type=F32)
            cols = [s[:, i * LANES:(i + 1) * LANES] for i in range(size // LANES)]
            c_vec = functools.reduce(jnp.maximum, cols)
            c_max = jnp.broadcast_to(jnp.max(c_vec, axis=-1, keepdims=True), (tq, LANES))
            m_new = c_max if m_b is None else jnp.maximum(m_b, c_max)
            p = jnp.concatenate([jnp.exp2(col - m_new).astype(BF16) for col in cols], axis=-1)
            v_lo = (hh // 2) * 2 * V_HEAD
            v_ext = jnp.concatenate([v_ref[0, start:start + size, v_lo:v_lo + 2 * V_HEAD],
                                     jnp.ones((size, LANES), BF16)], axis=-1)
            pv = _dot(p, v_ext)
            if m_b is None:
                acc = pv
            else:
                alpha = jnp.exp2(m_b - m_new)
                acc = jnp.concatenate([alpha, alpha], axis=-1) * acc + pv
            m_b = m_new
        outs.append(acc[:, 0:LANES] / acc[:, LANES:2 * LANES])
    lane = lax.broadcasted_iota(jnp.int32, outs[0].shape, 1)
    for pr in range(HEADS_PER_STEP // 2):
        o_ref[0, :, pr * 2 * V_HEAD:(pr + 1) * 2 * V_HEAD] = jnp.where(lane < V_HEAD, outs[2 * pr], outs[2 * pr + 1])


def _attention(q, kx, kc, vx, vc, tq):
    nb, n, _ = q.shape
    nk = kx.shape[1]
    nc = kc.shape[1]
    pairs = MLA_HEADS // HEADS_PER_STEP
    qk_blk = HEADS_PER_STEP * HEAD_PAD
    v_blk = HEADS_PER_STEP * V_HEAD
    return pl.pallas_call(
        _attn_kernel,
        grid=(nb, pairs, n // tq),
        in_specs=[pl.BlockSpec((1, tq, qk_blk), lambda b, p, j: (b, j, p)),
                  pl.BlockSpec((1, nk, qk_blk), lambda b, p, j: (b, 0, p)),
                  pl.BlockSpec((1, nc, qk_blk), lambda b, p, j: (b, 0, p)),
                  pl.BlockSpec((1, nk, v_blk), lambda b, p, j: (b, 0, p)),
                  pl.BlockSpec((1, nc, v_blk), lambda b, p, j: (b, 0, p))],
        out_specs=pl.BlockSpec((1, tq, v_blk), lambda b, p, j: (b, j, p)),
        out_shape=jax.ShapeDtypeStruct((nb, n, MLA_WIDTH), F32),
        compiler_params=_params("parallel", "parallel", "parallel"),
        name="attention",
    )(q, kx, kc, vx, vc)


SCAN_BLOCK = 32


def _s5_kernel(uc_ref, ux_ref, lre_ref, lim_ref, ldt_ref, bre_ref, bim_ref, cre_ref, cim_ref,
               y_ref, bb_ref, cm_ref, a_ref, h_ref, bd_ref, *, t_chunk, reverse):
    j = pl.program_id(0)

    @pl.when(j == 0)
    def _discretize():
        dt = jnp.exp(ldt_ref[...])
        lr = jnp.minimum(lre_ref[...], -1e-4)
        li = lim_ref[...]
        mag = jnp.exp(lr * dt)
        ar = mag * jnp.cos(li * dt)
        ai = mag * jnp.sin(li * dt)
        den = lr * lr + li * li
        fr = ((ar - 1.0) * lr + ai * li) / den
        fi = (ai * lr - (ar - 1.0) * li) / den
        same_group = (lax.broadcasted_iota(jnp.int32, (SSM_WIDTH, N_STATE), 0) // SSM_CH
                      == lax.broadcasted_iota(jnp.int32, (SSM_WIDTH, N_STATE), 1) // SSM_STATE)

        def block_diag(ref):
            return jnp.where(same_group, jnp.concatenate([ref[...]] * SSM_GROUPS, axis=-1), 0.0)

        bre = block_diag(bre_ref)
        bim = block_diag(bim_ref)
        bb_ref[:, 0:N_STATE] = (fr * bre - fi * bim).astype(BF16)
        bb_ref[:, N_STATE:2 * N_STATE] = (fr * bim + fi * bre).astype(BF16)
        cm_ref[0:N_STATE, :] = block_diag(cre_ref).T.astype(BF16)
        cm_ref[N_STATE:2 * N_STATE, :] = (-block_diag(cim_ref)).T.astype(BF16)
        a_ref[0:SUBLANES, :] = jnp.broadcast_to(ar, (SUBLANES, N_STATE))
        a_ref[SUBLANES:2 * SUBLANES, :] = jnp.broadcast_to(ai, (SUBLANES, N_STATE))
        h_ref[...] = jnp.zeros_like(h_ref)

    blk_rows = SCAN_BLOCK * SUBLANES
    order = list(range(t_chunk // SCAN_BLOCK))
    steps = list(range(SCAN_BLOCK))
    if reverse:
        order.reverse()
        steps.reverse()

    def run(u_ref, with_readout):
        ar = a_ref[0:SUBLANES, :]
        ai = a_ref[SUBLANES:2 * SUBLANES, :]

        def drive(sb):
            u = u_ref[sb * SCAN_BLOCK:(sb + 1) * SCAN_BLOCK, :].reshape(blk_rows, SSM_WIDTH)
            bd_ref[sb * blk_rows:(sb + 1) * blk_rows, :] = _dot(u.astype(BF16), bb_ref[...])

        def scan(sb, hr, hi):
            for t in steps:
                r = sb * blk_rows + t * SUBLANES
                nr = ar * hr - ai * hi + bd_ref[r:r + SUBLANES, 0:N_STATE]
                ni = ar * hi + ai * hr + bd_ref[r:r + SUBLANES, N_STATE:2 * N_STATE]
                bd_ref[r:r + SUBLANES, 0:N_STATE] = nr
                bd_ref[r:r + SUBLANES, N_STATE:2 * N_STATE] = ni
                hr, hi = nr, ni
            return hr, hi

        def readout(sb):
            y = _dot(bd_ref[sb * blk_rows:(sb + 1) * blk_rows, :].astype(BF16), cm_ref[...])
            y_ref[sb * SCAN_BLOCK:(sb + 1) * SCAN_BLOCK, :] = y.reshape(SCAN_BLOCK, SUBLANES * SSM_WIDTH)

        hr = h_ref[0:SUBLANES, :]
        hi = h_ref[SUBLANES:2 * SUBLANES, :]
        drive(order[0])
        for k, sb in enumerate(order):
            if k + 1 < len(order):
                drive(order[k + 1])
            hr, hi = scan(sb, hr, hi)
            if with_readout and k >= 1:
                readout(order[k - 1])
        if with_readout:
            readout(order[-1])
        h_ref[0:SUBLANES, :] = hr
        h_ref[SUBLANES:2 * SUBLANES, :] = hi

    @pl.when(j == 0)
    def _context():
        run(uc_ref, False)

    @pl.when(j > 0)
    def _latent():
        run(ux_ref, True)


def _s5(u_c, u_x, lre, lim, ldt, bre, bim, cre, cim, t_chunk, reverse):
    rows = t_chunk * SUBLANES
    width = SUBLANES * SSM_WIDTH
    assert u_c.shape == (t_chunk, width) and u_x.shape[1] == width
    nch = u_x.shape[0] // t_chunk

    def chunk(j):
        jj = jnp.maximum(j - 1, 0)
        return nch - 1 - jj if reverse else jj

    d = int(reverse)
    vec_spec = pl.BlockSpec((None, 1, N_STATE), lambda j: (d, 0, 0), pipeline_mode=pl.Buffered(1))
    mat_spec = pl.BlockSpec((None, SSM_WIDTH, SSM_STATE), lambda j: (d, 0, 0), pipeline_mode=pl.Buffered(1))
    return pl.pallas_call(
        functools.partial(_s5_kernel, t_chunk=t_chunk, reverse=reverse),
        grid=(nch + 1,),
        in_specs=[pl.BlockSpec((t_chunk, width), lambda j: (0, 0)),
                  pl.BlockSpec((t_chunk, width), lambda j: (chunk(j), 0)),
                  vec_spec, vec_spec, vec_spec, mat_spec, mat_spec, mat_spec, mat_spec],
        out_specs=pl.BlockSpec((t_chunk, width), lambda j: (chunk(j), 0)),
        out_shape=jax.ShapeDtypeStruct((u_x.shape[0], width), F32),
        scratch_shapes=[pltpu.VMEM((SSM_WIDTH, 2 * N_STATE), BF16),
                        pltpu.VMEM((2 * N_STATE, SSM_WIDTH), BF16),
                        pltpu.VMEM((2 * SUBLANES, N_STATE), F32),
                        pltpu.VMEM((2 * SUBLANES, N_STATE), F32),
                        pltpu.VMEM((rows, 2 * N_STATE), F32)],
        compiler_params=_params("arbitrary"),
        name="s5_rev" if reverse else "s5_fwd",
    )(u_c, u_x, lre, lim, ldt, bre, bim, cre, cim)


def _gelu_tanh(x):
    return 0.5 * x * (1.0 + jnp.tanh(math.sqrt(2.0 / math.pi) * (x + 0.044715 * (x * x * x))))


def _final_kernel(x_ref, attn_ref, u_ref, yf_ref, yb_ref, mod_ref, dsk_ref, wglu_ref, gmla_ref, gssm_ref,
                  wo_ref, g2_ref, wgu_ref, wd_ref, gf_ref, o_ref):
    x = x_ref[0]
    m = mod_ref[0]
    y = u_ref[...] * dsk_ref[...] + yf_ref[...] + yb_ref[...]
    z = _gelu_tanh(y).astype(BF16)
    ag = _dot(z, wglu_ref[...])
    ssm = ag[:, 0:SSM_WIDTH] * jax.nn.sigmoid(ag[:, SSM_WIDTH:2 * SSM_WIDTH])
    a_n = _rms(attn_ref[0], gmla_ref[...]).astype(BF16)
    s_n = _rms(ssm, gssm_ref[...]).astype(BF16)
    mix = _dot(a_n, wo_ref[0:MLA_WIDTH, :]) + _dot(s_n, wo_ref[MLA_WIDTH:MLA_WIDTH + SSM_WIDTH, :])
    x = x + m[5:6] * mix
    h = (_rms(x, g2_ref[...]) * (1.0 + m[7:8]) + m[6:7]).astype(BF16)
    x = x + 0.5 * m[8:9] * _swiglu(h, wgu_ref, wd_ref)
    o_ref[0] = _rms(x, gf_ref[...])


def _final(x, attn, u, y_fwd, y_bwd, mod, dsk, wglu, gmla, gssm, wo, g2, wgu, wd, gf, tm):
    nb, n, _ = x.shape
    ssm_spec = pl.BlockSpec((tm, SSM_WIDTH), lambda b, j: (j, b))
    return pl.pallas_call(
        _final_kernel,
        grid=(nb, n // tm),
        in_specs=[pl.BlockSpec((1, tm, D_MODEL), lambda b, j: (b, j, 0)),
                  pl.BlockSpec((1, tm, MLA_WIDTH), lambda b, j: (b, j, 0)),
                  ssm_spec, ssm_spec, ssm_spec,
                  pl.BlockSpec((1, N_MOD, D_MODEL), lambda b, j: (b, 0, 0)),
                  _const_spec((1, SSM_WIDTH)),
                  _const_spec((SSM_WIDTH, 2 * SSM_WIDTH)),
                  _const_spec((1, MLA_WIDTH)),
                  _const_spec((1, SSM_WIDTH)),
                  _const_spec((MLA_WIDTH + SSM_WIDTH, D_MODEL)),
                  _const_spec((1, D_MODEL)),
                  _const_spec((D_MODEL, 2 * D_FF)),
                  _const_spec((D_FF, D_MODEL)),
                  _const_spec((1, D_MODEL))],
        out_specs=pl.BlockSpec((1, tm, D_MODEL), lambda b, j: (b, j, 0)),
        out_shape=jax.ShapeDtypeStruct(x.shape, F32),
        compiler_params=_params("parallel", "parallel"),
        name="final",
    )(x, attn, u, y_fwd, y_bwd, mod, dsk, wglu, gmla, gssm, wo, g2, wgu, wd, gf)


def _rot_cols(w):
    wp = w.reshape(w.shape[0], -1, 2)
    return jnp.stack([-wp[..., 1], wp[..., 0]], axis=-1).reshape(w.shape)


def _rope_tables(n_tokens):
    rows = n_tokens // GRID_W
    row = np.repeat(np.arange(rows), GRID_W).astype(np.float32)
    col = np.tile(np.arange(GRID_W), rows).astype(np.float32)
    per_axis = QK_ROPE // 2
    inv_freq = (ROPE_BASE ** (-np.arange(0, per_axis, 2, dtype=np.float32) / per_axis)).astype(np.float32)
    ang = np.concatenate([row[:, None] * inv_freq, col[:, None] * inv_freq], axis=-1)
    cos = np.repeat(np.cos(ang), 2, axis=-1).astype(np.float32)
    sin = np.repeat(np.sin(ang), 2, axis=-1).astype(np.float32)
    return cos, sin


def _head_tables(cos, sin, nope_fill, scale):
    n = cos.shape[0]
    pad = HEAD_PAD - QK_NOPE - QK_ROPE
    c = np.concatenate([np.full((n, QK_NOPE), nope_fill, np.float32), cos, np.zeros((n, pad), np.float32)], axis=-1)
    s = np.concatenate([np.zeros((n, QK_NOPE), np.float32), sin, np.zeros((n, pad), np.float32)], axis=-1)
    return (c * np.float32(scale)).astype(np.float32), (s * np.float32(scale)).astype(np.float32)


def _pad_head_cols(w_nope, w_rope):
    k, h = w_nope.shape[:2]
    pad = HEAD_PAD - QK_NOPE - QK_ROPE
    return jnp.concatenate([w_nope, w_rope, jnp.zeros((k, h, pad), w_nope.dtype)], axis=-1).reshape(k, h * HEAD_PAD)


def kernel(x, c, ctx, c_ctx, w_mod, b_mod, g_ffn1, w_gu1, w_down1, g_mix, w_in, g_cq, w_uq, g_ckv, w_ukv,
           lam_re, lam_im, log_dt, b_re, b_im, c_re, c_im, d_skip, w_glu, g_mla_out, g_ssm_out, w_out,
           g_ffn2, w_gu2, w_down2, g_final):
    bsz, seq, _ = x.shape
    n_ctx = ctx.shape[1]
    assert w_mod.shape[0] == 1, "single-layer block"

    def layer(a):
        return a.reshape(a.shape[1:])

    rows = 2 * SUBLANES
    cvec = jnp.concatenate([c, c_ctx[None], jnp.zeros((rows - bsz - 1, D_MODEL), F32)], axis=0)
    mod = _adaln(cvec, layer(w_mod), b_mod).reshape(rows, N_MOD, D_MODEL)
    ctx_row = bsz

    wgu1 = layer(w_gu1).astype(BF16)
    wd1 = layer(w_down1).astype(BF16)
    x1 = _ffn(x, mod, g_ffn1, wgu1, wd1, None, FFN_TM)
    ctx1 = _ffn(ctx.reshape(1, bsz * n_ctx, D_MODEL), mod, g_ffn1, wgu1, wd1, ctx_row, CTX_FFN_TM)
    ctx1 = ctx1.reshape(bsz, n_ctx, D_MODEL)

    win = layer(w_in)
    o_kv = Q_RANK
    o_kr = o_kv + KV_RANK
    o_u = o_kr + QK_ROPE
    w_kr = win[:, o_kr:o_kr + QK_ROPE]
    zk = jnp.zeros((D_MODEL, QK_NOPE), F32)
    zp = jnp.zeros((D_MODEL, HEAD_PAD - QK_NOPE - QK_ROPE), F32)
    wa = jnp.concatenate([win[:, 0:o_kr], zk, w_kr, zp, zk, _rot_cols(w_kr), zp, win[:, o_u:]], axis=-1).astype(BF16)
    wuq = w_uq.reshape(Q_RANK, MLA_HEADS, QK_NOPE + QK_ROPE)
    wq_nope, wq_rope = wuq[..., :QK_NOPE], wuq[..., QK_NOPE:]
    wq_rot = _rot_cols(wq_rope.reshape(Q_RANK, -1)).reshape(wq_rope.shape)
    wq = jnp.concatenate([_pad_head_cols(wq_nope, wq_rope),
                          _pad_head_cols(jnp.zeros_like(wq_nope), wq_rot)], axis=-1).astype(BF16)
    wukv = w_ukv.reshape(KV_RANK, MLA_HEADS, QK_NOPE + V_HEAD)
    wk = _pad_head_cols(wukv[..., :QK_NOPE], jnp.zeros((KV_RANK, MLA_HEADS, QK_ROPE), F32))
    wv = wukv[..., QK_NOPE:].reshape(KV_RANK, MLA_WIDTH)
    wkv = jnp.concatenate([wk, wv], axis=-1).astype(BF16)

    cos, sin = _rope_tables(seq)
    cq_t, sq_t = _head_tables(cos, sin, 1.0, ATTN_SCALE * math.log2(math.e))
    ck_t, sk_t = _head_tables(cos, sin, 0.0, 1.0)
    one = np.ones((n_ctx, QK_ROPE), np.float32)
    ckc_t, skc_t = _head_tables(one, np.zeros_like(one), 0.0, 1.0)

    q, kx, vx, ux = _proj(x1, mod, g_mix, wa, g_cq, wq, g_ckv, wkv,
                          (cq_t, sq_t, ck_t, sk_t), None, PROJ_TM, True)
    kc, vc, uc = _proj(ctx1, mod, g_mix, wa, g_cq, wq, g_ckv, wkv,
                       (ckc_t, skc_t, ckc_t, skc_t), ctx_row, n_ctx, False)

    attn = _attention(q, kx, kc, vx, vc, ATTN_TQ)

    lre = lam_re.reshape(2, 1, N_STATE)
    lim = lam_im.reshape(2, 1, N_STATE)
    ldt = jnp.repeat(log_dt.reshape(2, SSM_GROUPS), SSM_STATE, axis=-1).reshape(2, 1, N_STATE)
    bre = jnp.swapaxes(b_re.reshape(2, SSM_GROUPS, SSM_STATE, SSM_CH), -1, -2).reshape(2, SSM_WIDTH, SSM_STATE)
    bim = jnp.swapaxes(b_im.reshape(2, SSM_GROUPS, SSM_STATE, SSM_CH), -1, -2).reshape(2, SSM_WIDTH, SSM_STATE)
    cre = c_re.reshape(2, SSM_WIDTH, SSM_STATE)
    cim = c_im.reshape(2, SSM_WIDTH, SSM_STATE)
    y_fwd, y_bwd = [_s5(uc, ux, lre, lim, ldt, bre, bim, cre, cim, n_ctx, rev) for rev in (False, True)]

    return _final(x1, attn, ux, y_fwd, y_bwd, mod, d_skip, layer(w_glu).astype(BF16), g_mla_out, g_ssm_out,
                  layer(w_out).astype(BF16), g_ffn2, layer(w_gu2).astype(BF16), layer(w_down2).astype(BF16),
                  g_final[None], FINAL_TM)
```

```python
import functools
import math

import jax
import jax.numpy as jnp
import numpy as np
from jax import lax
from jax.experimental import pallas as pl
from jax.experimental.pallas import tpu as pltpu

D_MODEL = 1024
GRID_W = 64
N_MOD = 9
D_FF = 2816
MLA_HEADS = 12
QK_NOPE = 64
QK_ROPE = 32
V_HEAD = 64
Q_RANK = 256
KV_RANK = 128
MLA_WIDTH = MLA_HEADS * V_HEAD
SSM_GROUPS = 16
SSM_CH = 16
SSM_WIDTH = SSM_GROUPS * SSM_CH
SSM_STATE = 64
N_STATE = SSM_GROUPS * SSM_STATE
ROPE_BASE = 10000.0
ATTN_SCALE = (QK_NOPE + QK_ROPE) ** -0.5
EPS = 1e-6

LANES = 128
SUBLANES = 8
MXU_DIM = 256
HEAD_PAD = 128
VMEM_LIMIT = 56 * 1024 * 1024

FFN_TM = 1024
CTX_FFN_TM = 512
PROJ_TM = 1024
ATTN_TQ = 1024
FINAL_TM = 512

F32 = jnp.float32
BF16 = jnp.bfloat16


def _rms(x, g):
    ms = jnp.mean(x * x, axis=-1, keepdims=True)
    return x * lax.rsqrt(ms + EPS) * g


def _dot(a, b):
    return jnp.dot(a, b, preferred_element_type=F32)


def _const_spec(shape):
    zeros = (0,) * len(shape)
    return pl.BlockSpec(shape, lambda *_: zeros, pipeline_mode=pl.Buffered(1))


def _params(*sem):
    return pltpu.CompilerParams(dimension_semantics=sem, vmem_limit_bytes=VMEM_LIMIT)


def _adaln_kernel(c_ref, w_ref, b_ref, o_ref):
    c = c_ref[...]
    s = (c * jax.nn.sigmoid(c)).astype(BF16)
    o_ref[...] = _dot(s, w_ref[...].astype(BF16)) + b_ref[...]


def _adaln(cvec, w_mod, b_mod):
    rows = cvec.shape[0]
    n = w_mod.shape[1]
    bn = D_MODEL
    return pl.pallas_call(
        _adaln_kernel,
        grid=(n // bn,),
        in_specs=[pl.BlockSpec((rows, D_MODEL), lambda i: (0, 0)),
                  pl.BlockSpec((D_MODEL, bn), lambda i: (0, i)),
                  pl.BlockSpec((1, bn), lambda i: (0, i))],
        out_specs=pl.BlockSpec((rows, bn), lambda i: (0, i)),
        out_shape=jax.ShapeDtypeStruct((rows, n), F32),
        compiler_params=_params("parallel"),
        name="adaln",
    )(cvec, w_mod, b_mod)


FF_TILES = D_FF // MXU_DIM
FF_SPLITS = (0, (FF_TILES + 1) // 2 * MXU_DIM, D_FF)


def _swiglu(h, wgu_ref, wd_ref):
    acc = None
    for lo, hi in zip(FF_SPLITS[:-1], FF_SPLITS[1:]):
        gate = _dot(h, wgu_ref[:, lo:hi])
        up = _dot(h, wgu_ref[:, D_FF + lo:D_FF + hi])
        act = (gate * jax.nn.sigmoid(gate) * up).astype(BF16)
        part = _dot(act, wd_ref[lo:hi, :])
        acc = part if acc is None else acc + part
    return acc


def _ffn_kernel(x_ref, mod_ref, g_ref, wgu_ref, wd_ref, o_ref):
    x = x_ref[0]
    m = mod_ref[0]
    h = (_rms(x, g_ref[...]) * (1.0 + m[1:2]) + m[0:1]).astype(BF16)
    o_ref[0] = x + 0.5 * m[2:3] * _swiglu(h, wgu_ref, wd_ref)


def _ffn(x, mod, g, wgu, wd, mod_row, tm):
    nb, n, _ = x.shape
    mod_map = (lambda b, j: (b, 0, 0)) if mod_row is None else (lambda b, j: (mod_row, 0, 0))
    return pl.pallas_call(
        _ffn_kernel,
        grid=(nb, n // tm),
        in_specs=[pl.BlockSpec((1, tm, D_MODEL), lambda b, j: (b, j, 0)),
                  pl.BlockSpec((1, N_MOD, D_MODEL), mod_map),
                  _const_spec((1, D_MODEL)),
                  _const_spec((D_MODEL, 2 * D_FF)),
                  _const_spec((D_FF, D_MODEL))],
        out_specs=pl.BlockSpec((1, tm, D_MODEL), lambda b, j: (b, j, 0)),
        out_shape=jax.ShapeDtypeStruct(x.shape, F32),
        compiler_params=_params("parallel", "parallel"),
        name="ffn",
    )(x, mod, g, wgu, wd)


W_A_COLS = Q_RANK + KV_RANK + 2 * HEAD_PAD + SSM_WIDTH
QK_WIDTH = MLA_HEADS * HEAD_PAD


def _proj_kernel(x_ref, mod_ref, g_ref, wa_ref, gcq_ref, wq_ref, gckv_ref, wkv_ref,
                 cq_ref, sq_ref, ck_ref, sk_ref, *out_refs, with_q):
    if with_q:
        q_ref, k_ref, v_ref, u_ref = out_refs
    else:
        k_ref, v_ref, u_ref = out_refs
    x = x_ref[0]
    m = mod_ref[0]
    h = (_rms(x, g_ref[...]) * (1.0 + m[4:5]) + m[3:4]).astype(BF16)
    p = _dot(h, wa_ref[...])
    o_kv = Q_RANK
    o_kr = o_kv + KV_RANK
    o_rot = o_kr + HEAD_PAD
    o_u = o_rot + HEAD_PAD
    u_ref[...] = p[:, o_u:o_u + SSM_WIDTH]
    kr = p[:, o_kr:o_kr + HEAD_PAD] * ck_ref[...] + p[:, o_rot:o_rot + HEAD_PAD] * sk_ref[...]
    ckv = _rms(p[:, o_kv:o_kv + KV_RANK], gckv_ref[...]).astype(BF16)
    kv = _dot(ckv, wkv_ref[...])
    for hd in range(MLA_HEADS):
        lo = hd * HEAD_PAD
        k_ref[0, :, lo:lo + HEAD_PAD] = (kv[:, lo:lo + HEAD_PAD] + kr).astype(BF16)
    v_ref[0] = kv[:, QK_WIDTH:QK_WIDTH + MLA_WIDTH].astype(BF16)
    if with_q:
        cq = _rms(p[:, 0:Q_RANK], gcq_ref[...]).astype(BF16)
        qa = _dot(cq, wq_ref[:, 0:QK_WIDTH])
        qb = _dot(cq, wq_ref[:, QK_WIDTH:2 * QK_WIDTH])
        cq_t = cq_ref[...]
        sq_t = sq_ref[...]
        for hd in range(MLA_HEADS):
            lo = hd * HEAD_PAD
            q_ref[0, :, lo:lo + HEAD_PAD] = (qa[:, lo:lo + HEAD_PAD] * cq_t
                                             + qb[:, lo:lo + HEAD_PAD] * sq_t).astype(BF16)


def _proj(x, mod, g_mix, wa, gcq, wq, gckv, wkv, tabs, mod_row, tm, with_q):
    nb, n, _ = x.shape
    mod_map = (lambda b, j: (b, 0, 0)) if mod_row is None else (lambda b, j: (mod_row, 0, 0))
    tab_spec = pl.BlockSpec((tm, HEAD_PAD), lambda b, j: (j, 0))
    out_shape = [jax.ShapeDtypeStruct((nb, n, QK_WIDTH), BF16),
                 jax.ShapeDtypeStruct((nb, n, MLA_WIDTH), BF16),
                 jax.ShapeDtypeStruct((n, nb * SSM_WIDTH), F32)]
    out_specs = [pl.BlockSpec((1, tm, QK_WIDTH), lambda b, j: (b, j, 0)),
                 pl.BlockSpec((1, tm, MLA_WIDTH), lambda b, j: (b, j, 0)),
                 pl.BlockSpec((tm, SSM_WIDTH), lambda b, j: (j, b))]
    if with_q:
        out_shape = [jax.ShapeDtypeStruct((nb, n, QK_WIDTH), BF16)] + out_shape
        out_specs = [pl.BlockSpec((1, tm, QK_WIDTH), lambda b, j: (b, j, 0))] + out_specs
    return pl.pallas_call(
        functools.partial(_proj_kernel, with_q=with_q),
        grid=(nb, n // tm),
        in_specs=[pl.BlockSpec((1, tm, D_MODEL), lambda b, j: (b, j, 0)),
                  pl.BlockSpec((1, N_MOD, D_MODEL), mod_map),
                  _const_spec((1, D_MODEL)),
                  _const_spec(wa.shape),
                  _const_spec((1, Q_RANK)),
                  _const_spec(wq.shape),
                  _const_spec((1, KV_RANK)),
                  _const_spec(wkv.shape),
                  tab_spec, tab_spec, tab_spec, tab_spec],
        out_specs=out_specs,
        out_shape=out_shape,
        compiler_params=_params("parallel", "parallel"),
        name="proj_q" if with_q else "proj_ctx",
    )(x, mod, g_mix, wa, gcq, wq, gckv, wkv, *tabs)


HEADS_PER_STEP = 4
KEY_SPLITS = (0, 1024, 2048, 3072, 4096)


def _attn_kernel(q_ref, kx_ref, kc_ref, vx_ref, vc_ref, o_ref):
    nt = (((1,), (1,)), ((), ()))
    tq = q_ref.shape[1]
    assert KEY_SPLITS[-1] == kx_ref.shape[1]
    chunks = [(kx_ref, vx_ref, lo, hi - lo) for lo, hi in zip(KEY_SPLITS[:-1], KEY_SPLITS[1:])]
    chunks.append((kc_ref, vc_ref, 0, kc_ref.shape[1]))
    outs = []
    for hh in range(HEADS_PER_STEP):
        lo = hh * HEAD_PAD
        q = q_ref[0, :, lo:lo + HEAD_PAD]
        m_b = acc = None
        for k_ref, v_ref, start, size in chunks:
            s = lax.dot_general(q, k_ref[0, start:start + size, lo:lo + HEAD_PAD], nt,
                                preferred_element_type=F32)
            cols = [s[:, i * LANES:(i + 1) * LANES] for i in range(size // LANES)]
            c_vec = functools.reduce(jnp.maximum, cols)
            c_max = jnp.broadcast_to(jnp.max(c_vec, axis=-1, keepdims=True), (tq, LANES))
            m_new = c_max if m_b is None else jnp.maximum(m_b, c_max)
            p = jnp.concatenate([jnp.exp2(col - m_new).astype(BF16) for col in cols], axis=-1)
            v_lo = (hh // 2) * 2 * V_HEAD
            v_ext = jnp.concatenate([v_ref[0, start:start + size, v_lo:v_lo + 2 * V_HEAD],
                                     jnp.ones((size, LANES), BF16)], axis=-1)
            pv = _dot(p, v_ext)
            if m_b is None:
                acc = pv
            else:
                alpha = jnp.exp2(m_b - m_new)
                acc = jnp.concatenate([alpha, alpha], axis=-1) * acc + pv
            m_b = m_new
        outs.append(acc[:, 0:LANES] / acc[:, LANES:2 * LANES])
    lane = lax.broadcasted_iota(jnp.int32, outs[0].shape, 1)
    for pr in range(HEADS_PER_STEP // 2):
        o_ref[0, :, pr * 2 * V_HEAD:(pr + 1) * 2 * V_HEAD] = jnp.where(lane < V_HEAD, outs[2 * pr], outs[2 * pr + 1])


def _attention(q, kx, kc, vx, vc, tq):
    nb, n, _ = q.shape
    nk = kx.shape[1]
    nc = kc.shape[1]
    pairs = MLA_HEADS // HEADS_PER_STEP
    qk_blk = HEADS_PER_STEP * HEAD_PAD
    v_blk = HEADS_PER_STEP * V_HEAD
    return pl.pallas_call(
        _attn_kernel,
        grid=(nb, pairs, n // tq),
        in_specs=[pl.BlockSpec((1, tq, qk_blk), lambda b, p, j: (b, j, p)),
                  pl.BlockSpec((1, nk, qk_blk), lambda b, p, j: (b, 0, p)),
                  pl.BlockSpec((1, nc, qk_blk), lambda b, p, j: (b, 0, p)),
                  pl.BlockSpec((1, nk, v_blk), lambda b, p, j: (b, 0, p)),
                  pl.BlockSpec((1, nc, v_blk), lambda b, p, j: (b, 0, p))],
        out_specs=pl.BlockSpec((1, tq, v_blk), lambda b, p, j: (b, j, p)),
        out_shape=jax.ShapeDtypeStruct((nb, n, MLA_WIDTH), F32),
        compiler_params=_params("parallel", "parallel", "parallel"),
        name="attention",
    )(q, kx, kc, vx, vc)


SCAN_BLOCK = 32
BD_RING = 4
S5_T = 512


def _s5_kernel(uc_ref, ux_ref, lre_ref, lim_ref, ldt_ref, bre_ref, bim_ref, cre_ref, cim_ref,
               y_ref, bb_ref, cm_ref, a_ref, h_ref, bd_ref, *, t_ctx, t_chunk, reverse):
    j = pl.program_id(0)

    @pl.when(j == 0)
    def _discretize():
        dt = jnp.exp(ldt_ref[...])
        lr = jnp.minimum(lre_ref[...], -1e-4)
        li = lim_ref[...]
        mag = jnp.exp(lr * dt)
        ar = mag * jnp.cos(li * dt)
        ai = mag * jnp.sin(li * dt)
        den = lr * lr + li * li
        fr = ((ar - 1.0) * lr + ai * li) / den
        fi = (ai * lr - (ar - 1.0) * li) / den
        same_group = (lax.broadcasted_iota(jnp.int32, (SSM_WIDTH, N_STATE), 0) // SSM_CH
                      == lax.broadcasted_iota(jnp.int32, (SSM_WIDTH, N_STATE), 1) // SSM_STATE)

        def block_diag(ref):
            return jnp.where(same_group, jnp.concatenate([ref[...]] * SSM_GROUPS, axis=-1), 0.0)

        bre = block_diag(bre_ref)
        bim = block_diag(bim_ref)
        bb_ref[:, 0:N_STATE] = (fr * bre - fi * bim).astype(BF16)
        bb_ref[:, N_STATE:2 * N_STATE] = (fr * bim + fi * bre).astype(BF16)
        cm_ref[0:N_STATE, :] = block_diag(cre_ref).T.astype(BF16)
        cm_ref[N_STATE:2 * N_STATE, :] = (-block_diag(cim_ref)).T.astype(BF16)
        a_ref[0:SUBLANES, :] = jnp.broadcast_to(ar, (SUBLANES, N_STATE))
        a_ref[SUBLANES:2 * SUBLANES, :] = jnp.broadcast_to(ai, (SUBLANES, N_STATE))
        h_ref[...] = jnp.zeros_like(h_ref)

    blk_rows = SCAN_BLOCK * SUBLANES
    steps = list(range(SCAN_BLOCK))
    if reverse:
        steps.reverse()

    def run(u_ref, t_len, with_readout):
        order = list(range(t_len // SCAN_BLOCK))
        if reverse:
            order.reverse()
        ar = a_ref[0:SUBLANES, :]
        ai = a_ref[SUBLANES:2 * SUBLANES, :]

        def slot(sb):
            return (sb % BD_RING) * blk_rows

        def drive(sb):
            u = u_ref[sb * SCAN_BLOCK:(sb + 1) * SCAN_BLOCK, :].reshape(blk_rows, SSM_WIDTH)
            bd_ref[slot(sb):slot(sb) + blk_rows, :] = _dot(u.astype(BF16), bb_ref[...])

        def scan(sb, hr, hi):
            for t in steps:
                r = slot(sb) + t * SUBLANES
                nr = ar * hr - ai * hi + bd_ref[r:r + SUBLANES, 0:N_STATE]
                ni = ar * hi + ai * hr + bd_ref[r:r + SUBLANES, N_STATE:2 * N_STATE]
                bd_ref[r:r + SUBLANES, 0:N_STATE] = nr
                bd_ref[r:r + SUBLANES, N_STATE:2 * N_STATE] = ni
                hr, hi = nr, ni
            return hr, hi

        def readout(sb):
            y = _dot(bd_ref[slot(sb):slot(sb) + blk_rows, :].astype(BF16), cm_ref[...])
            y_ref[sb * SCAN_BLOCK:(sb + 1) * SCAN_BLOCK, :] = y.reshape(SCAN_BLOCK, SUBLANES * SSM_WIDTH)

        hr = h_ref[0:SUBLANES, :]
        hi = h_ref[SUBLANES:2 * SUBLANES, :]
        drive(order[0])
        for k, sb in enumerate(order):
            if k + 1 < len(order):
                drive(order[k + 1])
            hr, hi = scan(sb, hr, hi)
            if with_readout and k >= 1:
                readout(order[k - 1])
        if with_readout:
            readout(order[-1])
        h_ref[0:SUBLANES, :] = hr
        h_ref[SUBLANES:2 * SUBLANES, :] = hi

    @pl.when(j == 0)
    def _context():
        run(uc_ref, t_ctx, False)

    @pl.when(j > 0)
    def _latent():
        run(ux_ref, t_chunk, True)


def _s5(u_c, u_x, lre, lim, ldt, bre, bim, cre, cim, t_chunk, reverse):
    t_ctx = u_c.shape[0]
    width = SUBLANES * SSM_WIDTH
    assert u_c.shape[1] == width and u_x.shape[1] == width
    assert t_ctx % SCAN_BLOCK == 0 and t_chunk % SCAN_BLOCK == 0 and u_x.shape[0] % t_chunk == 0
    nch = u_x.shape[0] // t_chunk

    def chunk(j):
        jj = jnp.maximum(j - 1, 0)
        return nch - 1 - jj if reverse else jj

    d = int(reverse)
    vec_spec = pl.BlockSpec((None, 1, N_STATE), lambda j: (d, 0, 0), pipeline_mode=pl.Buffered(1))
    mat_spec = pl.BlockSpec((None, SSM_WIDTH, SSM_STATE), lambda j: (d, 0, 0), pipeline_mode=pl.Buffered(1))
    return pl.pallas_call(
        functools.partial(_s5_kernel, t_ctx=t_ctx, t_chunk=t_chunk, reverse=reverse),
        grid=(nch + 1,),
        in_specs=[pl.BlockSpec((t_ctx, width), lambda j: (0, 0), pipeline_mode=pl.Buffered(1)),
                  pl.BlockSpec((t_chunk, width), lambda j: (chunk(j), 0)),
                  vec_spec, vec_spec, vec_spec, mat_spec, mat_spec, mat_spec, mat_spec],
        out_specs=pl.BlockSpec((t_chunk, width), lambda j: (chunk(j), 0)),
        out_shape=jax.ShapeDtypeStruct((u_x.shape[0], width), F32),
        scratch_shapes=[pltpu.VMEM((SSM_WIDTH, 2 * N_STATE), BF16),
                        pltpu.VMEM((2 * N_STATE, SSM_WIDTH), BF16),
                        pltpu.VMEM((2 * SUBLANES, N_STATE), F32),
                        pltpu.VMEM((2 * SUBLANES, N_STATE), F32),
                        pltpu.VMEM((BD_RING * SCAN_BLOCK * SUBLANES, 2 * N_STATE), F32)],
        compiler_params=_params("arbitrary"),
        name="s5_rev" if reverse else "s5_fwd",
    )(u_c, u_x, lre, lim, ldt, bre, bim, cre, cim)


def _gelu_tanh(x):
    return 0.5 * x * (1.0 + jnp.tanh(math.sqrt(2.0 / math.pi) * (x + 0.044715 * (x * x * x))))


def _final_kernel(x_ref, attn_ref, u_ref, yf_ref, yb_ref, mod_ref, dsk_ref, wglu_ref, gmla_ref, gssm_ref,
                  wo_ref, g2_ref, wgu_ref, wd_ref, gf_ref, o_ref):
    x = x_ref[0]
    m = mod_ref[0]
    y = u_ref[...] * dsk_ref[...] + yf_ref[...] + yb_ref[...]
    z = _gelu_tanh(y).astype(BF16)
    ag = _dot(z, wglu_ref[...])
    ssm = ag[:, 0:SSM_WIDTH] * jax.nn.sigmoid(ag[:, SSM_WIDTH:2 * SSM_WIDTH])
    a_n = _rms(attn_ref[0], gmla_ref[...]).astype(BF16)
    s_n = _rms(ssm, gssm_ref[...]).astype(BF16)
    mix = _dot(a_n, wo_ref[0:MLA_WIDTH, :]) + _dot(s_n, wo_ref[MLA_WIDTH:MLA_WIDTH + SSM_WIDTH, :])
    x = x + m[5:6] * mix
    h = (_rms(x, g2_ref[...]) * (1.0 + m[7:8]) + m[6:7]).astype(BF16)
    x = x + 0.5 * m[8:9] * _swiglu(h, wgu_ref, wd_ref)
    o_ref[0] = _rms(x, gf_ref[...])


def _final(x, attn, u, y_fwd, y_bwd, mod, dsk, wglu, gmla, gssm, wo, g2, wgu, wd, gf, tm):
    nb, n, _ = x.shape
    ssm_spec = pl.BlockSpec((tm, SSM_WIDTH), lambda b, j: (j, b))
    return pl.pallas_call(
        _final_kernel,
        grid=(nb, n // tm),
        in_specs=[pl.BlockSpec((1, tm, D_MODEL), lambda b, j: (b, j, 0)),
                  pl.BlockSpec((1, tm, MLA_WIDTH), lambda b, j: (b, j, 0)),
                  ssm_spec, ssm_spec, ssm_spec,
                  pl.BlockSpec((1, N_MOD, D_MODEL), lambda b, j: (b, 0, 0)),
                  _const_spec((1, SSM_WIDTH)),
                  _const_spec((SSM_WIDTH, 2 * SSM_WIDTH)),
                  _const_spec((1, MLA_WIDTH)),
                  _const_spec((1, SSM_WIDTH)),
                  _const_spec((MLA_WIDTH + SSM_WIDTH, D_MODEL)),
                  _const_spec((1, D_MODEL)),
                  _const_spec((D_MODEL, 2 * D_FF)),
                  _const_spec((D_FF, D_MODEL)),
                  _const_spec((1, D_MODEL))],
        out_specs=pl.BlockSpec((1, tm, D_MODEL), lambda b, j: (b, j, 0)),
        out_shape=jax.ShapeDtypeStruct(x.shape, F32),
        compiler_params=_params("parallel", "parallel"),
        name="final",
    )(x, attn, u, y_fwd, y_bwd, mod, dsk, wglu, gmla, gssm, wo, g2, wgu, wd, gf)


def _rot_cols(w):
    wp = w.reshape(w.shape[0], -1, 2)
    return jnp.stack([-wp[..., 1], wp[..., 0]], axis=-1).reshape(w.shape)


def _rope_tables(n_tokens):
    rows = n_tokens // GRID_W
    row = np.repeat(np.arange(rows), GRID_W).astype(np.float32)
    col = np.tile(np.arange(GRID_W), rows).astype(np.float32)
    per_axis = QK_ROPE // 2
    inv_freq = (ROPE_BASE ** (-np.arange(0, per_axis, 2, dtype=np.float32) / per_axis)).astype(np.float32)
    ang = np.concatenate([row[:, None] * inv_freq, col[:, None] * inv_freq], axis=-1)
    cos = np.repeat(np.cos(ang), 2, axis=-1).astype(np.float32)
    sin = np.repeat(np.sin(ang), 2, axis=-1).astype(np.float32)
    return cos, sin


def _head_tables(cos, sin, nope_fill, scale):
    n = cos.shape[0]
    pad = HEAD_PAD - QK_NOPE - QK_ROPE
    c = np.concatenate([np.full((n, QK_NOPE), nope_fill, np.float32), cos, np.zeros((n, pad), np.float32)], axis=-1)
    s = np.concatenate([np.zeros((n, QK_NOPE), np.float32), sin, np.zeros((n, pad), np.float32)], axis=-1)
    return (c * np.float32(scale)).astype(np.float32), (s * np.float32(scale)).astype(np.float32)


def _pad_head_cols(w_nope, w_rope):
    k, h = w_nope.shape[:2]
    pad = HEAD_PAD - QK_NOPE - QK_ROPE
    return jnp.concatenate([w_nope, w_rope, jnp.zeros((k, h, pad), w_nope.dtype)], axis=-1).reshape(k, h * HEAD_PAD)


def kernel(x, c, ctx, c_ctx, w_mod, b_mod, g_ffn1, w_gu1, w_down1, g_mix, w_in, g_cq, w_uq, g_ckv, w_ukv,
           lam_re, lam_im, log_dt, b_re, b_im, c_re, c_im, d_skip, w_glu, g_mla_out, g_ssm_out, w_out,
           g_ffn2, w_gu2, w_down2, g_final):
    bsz, seq, _ = x.shape
    n_ctx = ctx.shape[1]
    assert w_mod.shape[0] == 1, "single-layer block"

    def layer(a):
        return a.reshape(a.shape[1:])

    rows = 2 * SUBLANES
    cvec = jnp.concatenate([c, c_ctx[None], jnp.zeros((rows - bsz - 1, D_MODEL), F32)], axis=0)
    mod = _adaln(cvec, layer(w_mod), b_mod).reshape(rows, N_MOD, D_MODEL)
    ctx_row = bsz

    wgu1 = layer(w_gu1).astype(BF16)
    wd1 = layer(w_down1).astype(BF16)
    x1 = _ffn(x, mod, g_ffn1, wgu1, wd1, None, FFN_TM)
    ctx1 = _ffn(ctx.reshape(1, bsz * n_ctx, D_MODEL), mod, g_ffn1, wgu1, wd1, ctx_row, CTX_FFN_TM)
    ctx1 = ctx1.reshape(bsz, n_ctx, D_MODEL)

    win = layer(w_in)
    o_kv = Q_RANK
    o_kr = o_kv + KV_RANK
    o_u = o_kr + QK_ROPE
    w_kr = win[:, o_kr:o_kr + QK_ROPE]
    zk = jnp.zeros((D_MODEL, QK_NOPE), F32)
    zp = jnp.zeros((D_MODEL, HEAD_PAD - QK_NOPE - QK_ROPE), F32)
    wa = jnp.concatenate([win[:, 0:o_kr], zk, w_kr, zp, zk, _rot_cols(w_kr), zp, win[:, o_u:]], axis=-1).astype(BF16)
    wuq = w_uq.reshape(Q_RANK, MLA_HEADS, QK_NOPE + QK_ROPE)
    wq_nope, wq_rope = wuq[..., :QK_NOPE], wuq[..., QK_NOPE:]
    wq_rot = _rot_cols(wq_rope.reshape(Q_RANK, -1)).reshape(wq_rope.shape)
    wq = jnp.concatenate([_pad_head_cols(wq_nope, wq_rope),
                          _pad_head_cols(jnp.zeros_like(wq_nope), wq_rot)], axis=-1).astype(BF16)
    wukv = w_ukv.reshape(KV_RANK, MLA_HEADS, QK_NOPE + V_HEAD)
    wk = _pad_head_cols(wukv[..., :QK_NOPE], jnp.zeros((KV_RANK, MLA_HEADS, QK_ROPE), F32))
    wv = wukv[..., QK_NOPE:].reshape(KV_RANK, MLA_WIDTH)
    wkv = jnp.concatenate([wk, wv], axis=-1).astype(BF16)

    cos, sin = _rope_tables(seq)
    cq_t, sq_t = _head_tables(cos, sin, 1.0, ATTN_SCALE * math.log2(math.e))
    ck_t, sk_t = _head_tables(cos, sin, 0.0, 1.0)
    one = np.ones((n_ctx, QK_ROPE), np.float32)
    ckc_t, skc_t = _head_tables(one, np.zeros_like(one), 0.0, 1.0)

    q, kx, vx, ux = _proj(x1, mod, g_mix, wa, g_cq, wq, g_ckv, wkv,
                          (cq_t, sq_t, ck_t, sk_t), None, PROJ_TM, True)
    kc, vc, uc = _proj(ctx1, mod, g_mix, wa, g_cq, wq, g_ckv, wkv,
                       (ckc_t, skc_t, ckc_t, skc_t), ctx_row, n_ctx, False)

    attn = _attention(q, kx, kc, vx, vc, ATTN_TQ)

    lre = lam_re.reshape(2, 1, N_STATE)
    lim = lam_im.reshape(2, 1, N_STATE)
    ldt = jnp.repeat(log_dt.reshape(2, SSM_GROUPS), SSM_STATE, axis=-1).reshape(2, 1, N_STATE)
    bre = jnp.swapaxes(b_re.reshape(2, SSM_GROUPS, SSM_STATE, SSM_CH), -1, -2).reshape(2, SSM_WIDTH, SSM_STATE)
    bim = jnp.swapaxes(b_im.reshape(2, SSM_GROUPS, SSM_STATE, SSM_CH), -1, -2).reshape(2, SSM_WIDTH, SSM_STATE)
    cre = c_re.reshape(2, SSM_WIDTH, SSM_STATE)
    cim = c_im.reshape(2, SSM_WIDTH, SSM_STATE)
    y_fwd, y_bwd = [_s5(uc, ux, lre, lim, ldt, bre, bim, cre, cim, S5_T, rev) for rev in (False, True)]

    return _final(x1, attn, ux, y_fwd, y_bwd, mod, d_skip, layer(w_glu).astype(BF16), g_mla_out, g_ssm_out,
                  layer(w_out).astype(BF16), g_ffn2, layer(w_gu2).astype(BF16), layer(w_down2).astype(BF16),
                  g_final[None], FINAL_TM)
```

```python
import functools
import math

import jax
import jax.numpy as jnp
import numpy as np
from jax import lax
from jax.experimental import pallas as pl
from jax.experimental.pallas import tpu as pltpu

D_MODEL = 1024
GRID_W = 64
N_MOD = 9
D_FF = 2816
MLA_HEADS = 12
QK_NOPE = 64
QK_ROPE = 32
V_HEAD = 64
Q_RANK = 256
KV_RANK = 128
MLA_WIDTH = MLA_HEADS * V_HEAD
SSM_GROUPS = 16
SSM_CH = 16
SSM_WIDTH = SSM_GROUPS * SSM_CH
SSM_STATE = 64
N_STATE = SSM_GROUPS * SSM_STATE
ROPE_BASE = 10000.0
ATTN_SCALE = (QK_NOPE + QK_ROPE) ** -0.5
EPS = 1e-6

LANES = 128
SUBLANES = 8
MXU_DIM = 256
HEAD_PAD = 128
VMEM_LIMIT = 56 * 1024 * 1024

FFN_TM = 1024
CTX_FFN_TM = 512
PROJ_TM = 1024
ATTN_TQ = 1024
FINAL_TM = 512

F32 = jnp.float32
BF16 = jnp.bfloat16


def _rms(x, g):
    ms = jnp.mean(x * x, axis=-1, keepdims=True)
    return x * lax.rsqrt(ms + EPS) * g


def _dot(a, b):
    return jnp.dot(a, b, preferred_element_type=F32)


def _const_spec(shape):
    zeros = (0,) * len(shape)
    return pl.BlockSpec(shape, lambda *_: zeros, pipeline_mode=pl.Buffered(1))


def _params(*sem):
    return pltpu.CompilerParams(dimension_semantics=sem, vmem_limit_bytes=VMEM_LIMIT)


def _adaln_kernel(c_ref, w_ref, b_ref, o_ref):
    c = c_ref[...]
    s = (c * jax.nn.sigmoid(c)).astype(BF16)
    o_ref[...] = _dot(s, w_ref[...].astype(BF16)) + b_ref[...]


def _adaln(cvec, w_mod, b_mod):
    rows = cvec.shape[0]
    n = w_mod.shape[1]
    bn = D_MODEL
    return pl.pallas_call(
        _adaln_kernel,
        grid=(n // bn,),
        in_specs=[pl.BlockSpec((rows, D_MODEL), lambda i: (0, 0)),
                  pl.BlockSpec((D_MODEL, bn), lambda i: (0, i)),
                  pl.BlockSpec((1, bn), lambda i: (0, i))],
        out_specs=pl.BlockSpec((rows, bn), lambda i: (0, i)),
        out_shape=jax.ShapeDtypeStruct((rows, n), F32),
        compiler_params=_params("parallel"),
        name="adaln",
    )(cvec, w_mod, b_mod)


FF_TILES = D_FF // MXU_DIM
FF_SPLITS = (0, (FF_TILES + 1) // 2 * MXU_DIM, D_FF)


def _swiglu(h, wgu_ref, wd_ref):
    acc = None
    for lo, hi in zip(FF_SPLITS[:-1], FF_SPLITS[1:]):
        gate = _dot(h, wgu_ref[:, lo:hi])
        up = _dot(h, wgu_ref[:, D_FF + lo:D_FF + hi])
        act = (gate * jax.nn.sigmoid(gate) * up).astype(BF16)
        part = _dot(act, wd_ref[lo:hi, :])
        acc = part if acc is None else acc + part
    return acc


W_LOAD_CHUNKS = 16


def _weight_copy(w_hbm, stage_ref, sem_ref, c):
    rows = stage_ref.shape[1]
    return pltpu.make_async_copy(w_hbm.at[pl.ds(c * rows, rows), :], stage_ref.at[c % 2], sem_ref.at[c % 2])


def _load_weight_bf16(w_hbm, w_ref, stage_ref, sem_ref):
    rows = stage_ref.shape[1]
    assert w_hbm.shape[0] == W_LOAD_CHUNKS * rows
    _weight_copy(w_hbm, stage_ref, sem_ref, 0).start()
    for c in range(W_LOAD_CHUNKS):
        if c + 1 < W_LOAD_CHUNKS:
            _weight_copy(w_hbm, stage_ref, sem_ref, c + 1).start()
        _weight_copy(w_hbm, stage_ref, sem_ref, c).wait()
        w_ref[c * rows:(c + 1) * rows, :] = stage_ref[c % 2].astype(BF16)


def _ffn_weight_scratch():
    return [pltpu.VMEM((D_MODEL, 2 * D_FF), BF16),
            pltpu.VMEM((D_FF, D_MODEL), BF16),
            pltpu.VMEM((2, D_MODEL // W_LOAD_CHUNKS, 2 * D_FF), F32),
            pltpu.VMEM((2, D_FF // W_LOAD_CHUNKS, D_MODEL), F32),
            pltpu.SemaphoreType.DMA((2,)),
            pltpu.SemaphoreType.DMA((2,))]


def _load_ffn_weights(wgu_hbm, wd_hbm, wgu_ref, wd_ref, sgu_ref, sd_ref, sem_gu, sem_d):
    _load_weight_bf16(wgu_hbm, wgu_ref, sgu_ref, sem_gu)
    _load_weight_bf16(wd_hbm, wd_ref, sd_ref, sem_d)


def _ffn_kernel(x_ref, mod_ref, g_ref, wgu_hbm, wd_hbm, o_ref, wgu_ref, wd_ref, *load_scratch):
    @pl.when((pl.program_id(0) == 0) & (pl.program_id(1) == 0))
    def _first_step():
        _load_ffn_weights(wgu_hbm, wd_hbm, wgu_ref, wd_ref, *load_scratch)

    x = x_ref[0]
    m = mod_ref[0]
    h = (_rms(x, g_ref[...]) * (1.0 + m[1:2]) + m[0:1]).astype(BF16)
    o_ref[0] = x + 0.5 * m[2:3] * _swiglu(h, wgu_ref, wd_ref)


def _ffn(x, mod, g, wgu, wd, mod_row, tm):
    nb, n, _ = x.shape
    mod_map = (lambda b, j: (b, 0, 0)) if mod_row is None else (lambda b, j: (mod_row, 0, 0))
    return pl.pallas_call(
        _ffn_kernel,
        grid=(nb, n // tm),
        in_specs=[pl.BlockSpec((1, tm, D_MODEL), lambda b, j: (b, j, 0)),
                  pl.BlockSpec((1, N_MOD, D_MODEL), mod_map),
                  _const_spec((1, D_MODEL)),
                  pl.BlockSpec(memory_space=pl.ANY),
                  pl.BlockSpec(memory_space=pl.ANY)],
        out_specs=pl.BlockSpec((1, tm, D_MODEL), lambda b, j: (b, j, 0)),
        out_shape=jax.ShapeDtypeStruct(x.shape, F32),
        scratch_shapes=_ffn_weight_scratch(),
        compiler_params=_params("arbitrary", "arbitrary"),
        name="ffn",
    )(x, mod, g, wgu, wd)


W_A_COLS = Q_RANK + KV_RANK + 2 * HEAD_PAD + SSM_WIDTH
QK_WIDTH = MLA_HEADS * HEAD_PAD


def _proj_kernel(x_ref, mod_ref, g_ref, wa_ref, gcq_ref, wq_ref, gckv_ref, wkv_ref,
                 cq_ref, sq_ref, ck_ref, sk_ref, *out_refs, with_q):
    if with_q:
        q_ref, k_ref, v_ref, u_ref = out_refs
    else:
        k_ref, v_ref, u_ref = out_refs
    x = x_ref[0]
    m = mod_ref[0]
    h = (_rms(x, g_ref[...]) * (1.0 + m[4:5]) + m[3:4]).astype(BF16)
    p = _dot(h, wa_ref[...])
    o_kv = Q_RANK
    o_kr = o_kv + KV_RANK
    o_rot = o_kr + HEAD_PAD
    o_u = o_rot + HEAD_PAD
    u_ref[...] = p[:, o_u:o_u + SSM_WIDTH]
    kr = p[:, o_kr:o_kr + HEAD_PAD] * ck_ref[...] + p[:, o_rot:o_rot + HEAD_PAD] * sk_ref[...]
    ckv = _rms(p[:, o_kv:o_kv + KV_RANK], gckv_ref[...]).astype(BF16)
    kv = _dot(ckv, wkv_ref[...])
    for hd in range(MLA_HEADS):
        lo = hd * HEAD_PAD
        k_ref[0, :, lo:lo + HEAD_PAD] = (kv[:, lo:lo + HEAD_PAD] + kr).astype(BF16)
    v_ref[0] = kv[:, QK_WIDTH:QK_WIDTH + MLA_WIDTH].astype(BF16)
    if with_q:
        cq = _rms(p[:, 0:Q_RANK], gcq_ref[...]).astype(BF16)
        qa = _dot(cq, wq_ref[:, 0:QK_WIDTH])
        qb = _dot(cq, wq_ref[:, QK_WIDTH:2 * QK_WIDTH])
        cq_t = cq_ref[...]
        sq_t = sq_ref[...]
        for hd in range(MLA_HEADS):
            lo = hd * HEAD_PAD
            q_ref[0, :, lo:lo + HEAD_PAD] = (qa[:, lo:lo + HEAD_PAD] * cq_t
                                             + qb[:, lo:lo + HEAD_PAD] * sq_t).astype(BF16)


def _proj(x, mod, g_mix, wa, gcq, wq, gckv, wkv, tabs, mod_row, tm, with_q):
    nb, n, _ = x.shape
    mod_map = (lambda b, j: (b, 0, 0)) if mod_row is None else (lambda b, j: (mod_row, 0, 0))
    tab_spec = pl.BlockSpec((tm, HEAD_PAD), lambda b, j: (j, 0))
    out_shape = [jax.ShapeDtypeStruct((nb, n, QK_WIDTH), BF16),
                 jax.ShapeDtypeStruct((nb, n, MLA_WIDTH), BF16),
                 jax.ShapeDtypeStruct((n, nb * SSM_WIDTH), F32)]
    out_specs = [pl.BlockSpec((1, tm, QK_WIDTH), lambda b, j: (b, j, 0)),
                 pl.BlockSpec((1, tm, MLA_WIDTH), lambda b, j: (b, j, 0)),
                 pl.BlockSpec((tm, SSM_WIDTH), lambda b, j: (j, b))]
    if with_q:
        out_shape = [jax.ShapeDtypeStruct((nb, n, QK_WIDTH), BF16)] + out_shape
        out_specs = [pl.BlockSpec((1, tm, QK_WIDTH), lambda b, j: (b, j, 0))] + out_specs
    return pl.pallas_call(
        functools.partial(_proj_kernel, with_q=with_q),
        grid=(nb, n // tm),
        in_specs=[pl.BlockSpec((1, tm, D_MODEL), lambda b, j: (b, j, 0)),
                  pl.BlockSpec((1, N_MOD, D_MODEL), mod_map),
                  _const_spec((1, D_MODEL)),
                  _const_spec(wa.shape),
                  _const_spec((1, Q_RANK)),
                  _const_spec(wq.shape),
                  _const_spec((1, KV_RANK)),
                  _const_spec(wkv.shape),
                  tab_spec, tab_spec, tab_spec, tab_spec],
        out_specs=out_specs,
        out_shape=out_shape,
        compiler_params=_params("parallel", "parallel"),
        name="proj_q" if with_q else "proj_ctx",
    )(x, mod, g_mix, wa, gcq, wq, gckv, wkv, *tabs)


HEADS_PER_STEP = 4
KEY_SPLITS = (0, 1024, 2048, 3072, 4096)


def _attn_kernel(q_ref, kx_ref, kc_ref, vx_ref, vc_ref, o_ref):
    nt = (((1,), (1,)), ((), ()))
    tq = q_ref.shape[1]
    assert KEY_SPLITS[-1] == kx_ref.shape[1]
    chunks = [(kx_ref, vx_ref, lo, hi - lo) for lo, hi in zip(KEY_SPLITS[:-1], KEY_SPLITS[1:])]
    chunks.append((kc_ref, vc_ref, 0, kc_ref.shape[1]))
    outs = []
    for hh in range(HEADS_PER_STEP):
        lo = hh * HEAD_PAD
        q = q_ref[0, :, lo:lo + HEAD_PAD]
        m_b = acc = None
        for k_ref, v_ref, start, size in chunks:
            s = lax.dot_general(q, k_ref[0, start:start + size, lo:lo + HEAD_PAD], nt,
                                preferred_element_type=F32)
            cols = [s[:, i * LANES:(i + 1) * LANES] for i in range(size // LANES)]
            c_vec = functools.reduce(jnp.maximum, cols)
            c_max = jnp.broadcast_to(jnp.max(c_vec, axis=-1, keepdims=True), (tq, LANES))
            m_new = c_max if m_b is None else jnp.maximum(m_b, c_max)
            p = jnp.concatenate([jnp.exp2(col - m_new).astype(BF16) for col in cols], axis=-1)
            v_lo = (hh // 2) * 2 * V_HEAD
            v_ext = jnp.concatenate([v_ref[0, start:start + size, v_lo:v_lo + 2 * V_HEAD],
                                     jnp.ones((size, LANES), BF16)], axis=-1)
            pv = _dot(p, v_ext)
            if m_b is None:
                acc = pv
            else:
                alpha = jnp.exp2(m_b - m_new)
                acc = jnp.concatenate([alpha, alpha], axis=-1) * acc + pv
            m_b = m_new
        outs.append(acc[:, 0:LANES] / acc[:, LANES:2 * LANES])
    lane = lax.broadcasted_iota(jnp.int32, outs[0].shape, 1)
    for pr in range(HEADS_PER_STEP // 2):
        o_ref[0, :, pr * 2 * V_HEAD:(pr + 1) * 2 * V_HEAD] = jnp.where(lane < V_HEAD, outs[2 * pr], outs[2 * pr + 1])


def _attention(q, kx, kc, vx, vc, tq):
    nb, n, _ = q.shape
    nk = kx.shape[1]
    nc = kc.shape[1]
    pairs = MLA_HEADS // HEADS_PER_STEP
    qk_blk = HEADS_PER_STEP * HEAD_PAD
    v_blk = HEADS_PER_STEP * V_HEAD
    return pl.pallas_call(
        _attn_kernel,
        grid=(nb, pairs, n // tq),
        in_specs=[pl.BlockSpec((1, tq, qk_blk), lambda b, p, j: (b, j, p)),
                  pl.BlockSpec((1, nk, qk_blk), lambda b, p, j: (b, 0, p)),
                  pl.BlockSpec((1, nc, qk_blk), lambda b, p, j: (b, 0, p)),
                  pl.BlockSpec((1, nk, v_blk), lambda b, p, j: (b, 0, p)),
                  pl.BlockSpec((1, nc, v_blk), lambda b, p, j: (b, 0, p))],
        out_specs=pl.BlockSpec((1, tq, v_blk), lambda b, p, j: (b, j, p)),
        out_shape=jax.ShapeDtypeStruct((nb, n, MLA_WIDTH), F32),
        compiler_params=_params("parallel", "parallel", "parallel"),
        name="attention",
    )(q, kx, kc, vx, vc)


SCAN_BLOCK = 32
BD_RING = 4
S5_T = 512


def _s5_kernel(uc_ref, ux_ref, lre_ref, lim_ref, ldt_ref, bre_ref, bim_ref, cre_ref, cim_ref,
               y_ref, bb_ref, cm_ref, a_ref, h_ref, bd_ref, *, t_ctx, t_chunk, reverse):
    j = pl.program_id(0)

    @pl.when(j == 0)
    def _discretize():
        dt = jnp.exp(ldt_ref[...])
        lr = jnp.minimum(lre_ref[...], -1e-4)
        li = lim_ref[...]
        mag = jnp.exp(lr * dt)
        ar = mag * jnp.cos(li * dt)
        ai = mag * jnp.sin(li * dt)
        den = lr * lr + li * li
        fr = ((ar - 1.0) * lr + ai * li) / den
        fi = (ai * lr - (ar - 1.0) * li) / den
        same_group = (lax.broadcasted_iota(jnp.int32, (SSM_WIDTH, N_STATE), 0) // SSM_CH
                      == lax.broadcasted_iota(jnp.int32, (SSM_WIDTH, N_STATE), 1) // SSM_STATE)

        def block_diag(ref):
            return jnp.where(same_group, jnp.concatenate([ref[...]] * SSM_GROUPS, axis=-1), 0.0)

        bre = block_diag(bre_ref)
        bim = block_diag(bim_ref)
        bb_ref[:, 0:N_STATE] = (fr * bre - fi * bim).astype(BF16)
        bb_ref[:, N_STATE:2 * N_STATE] = (fr * bim + fi * bre).astype(BF16)
        cm_ref[0:N_STATE, :] = block_diag(cre_ref).T.astype(BF16)
        cm_ref[N_STATE:2 * N_STATE, :] = (-block_diag(cim_ref)).T.astype(BF16)
        a_ref[0:SUBLANES, :] = jnp.broadcast_to(ar, (SUBLANES, N_STATE))
        a_ref[SUBLANES:2 * SUBLANES, :] = jnp.broadcast_to(ai, (SUBLANES, N_STATE))
        h_ref[...] = jnp.zeros_like(h_ref)

    blk_rows = SCAN_BLOCK * SUBLANES
    steps = list(range(SCAN_BLOCK))
    if reverse:
        steps.reverse()

    def run(u_ref, t_len, with_readout):
        order = list(range(t_len // SCAN_BLOCK))
        if reverse:
            order.reverse()
        ar = a_ref[0:SUBLANES, :]
        ai = a_ref[SUBLANES:2 * SUBLANES, :]

        def slot(sb):
            return (sb % BD_RING) * blk_rows

        def drive(sb):
            u = u_ref[sb * SCAN_BLOCK:(sb + 1) * SCAN_BLOCK, :].reshape(blk_rows, SSM_WIDTH)
            bd_ref[slot(sb):slot(sb) + blk_rows, :] = _dot(u.astype(BF16), bb_ref[...])

        def scan(sb, hr, hi):
            for t in steps:
                r = slot(sb) + t * SUBLANES
                nr = ar * hr - ai * hi + bd_ref[r:r + SUBLANES, 0:N_STATE]
                ni = ar * hi + ai * hr + bd_ref[r:r + SUBLANES, N_STATE:2 * N_STATE]
                bd_ref[r:r + SUBLANES, 0:N_STATE] = nr
                bd_ref[r:r + SUBLANES, N_STATE:2 * N_STATE] = ni
                hr, hi = nr, ni
            return hr, hi

        def readout(sb):
            y = _dot(bd_ref[slot(sb):slot(sb) + blk_rows, :].astype(BF16), cm_ref[...])
            y_ref[sb * SCAN_BLOCK:(sb + 1) * SCAN_BLOCK, :] = y.reshape(SCAN_BLOCK, SUBLANES * SSM_WIDTH)

        hr = h_ref[0:SUBLANES, :]
        hi = h_ref[SUBLANES:2 * SUBLANES, :]
        drive(order[0])
        for k, sb in enumerate(order):
            if k + 1 < len(order):
                drive(order[k + 1])
            hr, hi = scan(sb, hr, hi)
            if with_readout and k >= 1:
                readout(order[k - 1])
        if with_readout:
            readout(order[-1])
        h_ref[0:SUBLANES, :] = hr
        h_ref[SUBLANES:2 * SUBLANES, :] = hi

    @pl.when(j == 0)
    def _context():
        run(uc_ref, t_ctx, False)

    @pl.when(j > 0)
    def _latent():
        run(ux_ref, t_chunk, True)


def _s5(u_c, u_x, lre, lim, ldt, bre, bim, cre, cim, t_chunk, reverse):
    t_ctx = u_c.shape[0]
    width = SUBLANES * SSM_WIDTH
    assert u_c.shape[1] == width and u_x.shape[1] == width
    assert t_ctx % SCAN_BLOCK == 0 and t_chunk % SCAN_BLOCK == 0 and u_x.shape[0] % t_chunk == 0
    nch = u_x.shape[0] // t_chunk

    def chunk(j):
        jj = jnp.maximum(j - 1, 0)
        return nch - 1 - jj if reverse else jj

    d = int(reverse)
    vec_spec = pl.BlockSpec((None, 1, N_STATE), lambda j: (d, 0, 0), pipeline_mode=pl.Buffered(1))
    mat_spec = pl.BlockSpec((None, SSM_WIDTH, SSM_STATE), lambda j: (d, 0, 0), pipeline_mode=pl.Buffered(1))
    return pl.pallas_call(
        functools.partial(_s5_kernel, t_ctx=t_ctx, t_chunk=t_chunk, reverse=reverse),
        grid=(nch + 1,),
        in_specs=[pl.BlockSpec((t_ctx, width), lambda j: (0, 0), pipeline_mode=pl.Buffered(1)),
                  pl.BlockSpec((t_chunk, width), lambda j: (chunk(j), 0)),
                  vec_spec, vec_spec, vec_spec, mat_spec, mat_spec, mat_spec, mat_spec],
        out_specs=pl.BlockSpec((t_chunk, width), lambda j: (chunk(j), 0)),
        out_shape=jax.ShapeDtypeStruct((u_x.shape[0], width), F32),
        scratch_shapes=[pltpu.VMEM((SSM_WIDTH, 2 * N_STATE), BF16),
                        pltpu.VMEM((2 * N_STATE, SSM_WIDTH), BF16),
                        pltpu.VMEM((2 * SUBLANES, N_STATE), F32),
                        pltpu.VMEM((2 * SUBLANES, N_STATE), F32),
                        pltpu.VMEM((BD_RING * SCAN_BLOCK * SUBLANES, 2 * N_STATE), F32)],
        compiler_params=_params("arbitrary"),
        name="s5_rev" if reverse else "s5_fwd",
    )(u_c, u_x, lre, lim, ldt, bre, bim, cre, cim)


def _gelu_tanh(x):
    return 0.5 * x * (1.0 + jnp.tanh(math.sqrt(2.0 / math.pi) * (x + 0.044715 * (x * x * x))))


def _final_kernel(x_ref, attn_ref, u_ref, yf_ref, yb_ref, mod_ref, dsk_ref, wglu_ref, gmla_ref, gssm_ref,
                  wo_ref, g2_ref, wgu_hbm, wd_hbm, gf_ref, o_ref, wgu_ref, wd_ref, *load_scratch):
    @pl.when((pl.program_id(0) == 0) & (pl.program_id(1) == 0))
    def _first_step():
        _load_ffn_weights(wgu_hbm, wd_hbm, wgu_ref, wd_ref, *load_scratch)

    x = x_ref[0]
    m = mod_ref[0]
    y = u_ref[...] * dsk_ref[...] + yf_ref[...] + yb_ref[...]
    z = _gelu_tanh(y).astype(BF16)
    ag = _dot(z, wglu_ref[...])
    ssm = ag[:, 0:SSM_WIDTH] * jax.nn.sigmoid(ag[:, SSM_WIDTH:2 * SSM_WIDTH])
    a_n = _rms(attn_ref[0], gmla_ref[...]).astype(BF16)
    s_n = _rms(ssm, gssm_ref[...]).astype(BF16)
    mix = _dot(a_n, wo_ref[0:MLA_WIDTH, :]) + _dot(s_n, wo_ref[MLA_WIDTH:MLA_WIDTH + SSM_WIDTH, :])
    x = x + m[5:6] * mix
    h = (_rms(x, g2_ref[...]) * (1.0 + m[7:8]) + m[6:7]).astype(BF16)
    x = x + 0.5 * m[8:9] * _swiglu(h, wgu_ref, wd_ref)
    o_ref[0] = _rms(x, gf_ref[...])


def _final(x, attn, u, y_fwd, y_bwd, mod, dsk, wglu, gmla, gssm, wo, g2, wgu, wd, gf, tm):
    nb, n, _ = x.shape
    ssm_spec = pl.BlockSpec((tm, SSM_WIDTH), lambda b, j: (j, b))
    return pl.pallas_call(
        _final_kernel,
        grid=(nb, n // tm),
        in_specs=[pl.BlockSpec((1, tm, D_MODEL), lambda b, j: (b, j, 0)),
                  pl.BlockSpec((1, tm, MLA_WIDTH), lambda b, j: (b, j, 0)),
                  ssm_spec, ssm_spec, ssm_spec,
                  pl.BlockSpec((1, N_MOD, D_MODEL), lambda b, j: (b, 0, 0)),
                  _const_spec((1, SSM_WIDTH)),
                  _const_spec((SSM_WIDTH, 2 * SSM_WIDTH)),
                  _const_spec((1, MLA_WIDTH)),
                  _const_spec((1, SSM_WIDTH)),
                  _const_spec((MLA_WIDTH + SSM_WIDTH, D_MODEL)),
                  _const_spec((1, D_MODEL)),
                  pl.BlockSpec(memory_space=pl.ANY),
                  pl.BlockSpec(memory_space=pl.ANY),
                  _const_spec((1, D_MODEL))],
        out_specs=pl.BlockSpec((1, tm, D_MODEL), lambda b, j: (b, j, 0)),
        out_shape=jax.ShapeDtypeStruct(x.shape, F32),
        scratch_shapes=_ffn_weight_scratch(),
        compiler_params=_params("arbitrary", "arbitrary"),
        name="final",
    )(x, attn, u, y_fwd, y_bwd, mod, dsk, wglu, gmla, gssm, wo, g2, wgu, wd, gf)


def _rot_cols(w):
    wp = w.reshape(w.shape[0], -1, 2)
    return jnp.stack([-wp[..., 1], wp[..., 0]], axis=-1).reshape(w.shape)


def _rope_tables(n_tokens):
    rows = n_tokens // GRID_W
    row = np.repeat(np.arange(rows), GRID_W).astype(np.float32)
    col = np.tile(np.arange(GRID_W), rows).astype(np.float32)
    per_axis = QK_ROPE // 2
    inv_freq = (ROPE_BASE ** (-np.arange(0, per_axis, 2, dtype=np.float32) / per_axis)).astype(np.float32)
    ang = np.concatenate([row[:, None] * inv_freq, col[:, None] * inv_freq], axis=-1)
    cos = np.repeat(np.cos(ang), 2, axis=-1).astype(np.float32)
    sin = np.repeat(np.sin(ang), 2, axis=-1).astype(np.float32)
    return cos, sin


def _head_tables(cos, sin, nope_fill, scale):
    n = cos.shape[0]
    pad = HEAD_PAD - QK_NOPE - QK_ROPE
    c = np.concatenate([np.full((n, QK_NOPE), nope_fill, np.float32), cos, np.zeros((n, pad), np.float32)], axis=-1)
    s = np.concatenate([np.zeros((n, QK_NOPE), np.float32), sin, np.zeros((n, pad), np.float32)], axis=-1)
    return (c * np.float32(scale)).astype(np.float32), (s * np.float32(scale)).astype(np.float32)


def _pad_head_cols(w_nope, w_rope):
    k, h = w_nope.shape[:2]
    pad = HEAD_PAD - QK_NOPE - QK_ROPE
    return jnp.concatenate([w_nope, w_rope, jnp.zeros((k, h, pad), w_nope.dtype)], axis=-1).reshape(k, h * HEAD_PAD)


def kernel(x, c, ctx, c_ctx, w_mod, b_mod, g_ffn1, w_gu1, w_down1, g_mix, w_in, g_cq, w_uq, g_ckv, w_ukv,
           lam_re, lam_im, log_dt, b_re, b_im, c_re, c_im, d_skip, w_glu, g_mla_out, g_ssm_out, w_out,
           g_ffn2, w_gu2, w_down2, g_final):
    bsz, seq, _ = x.shape
    n_ctx = ctx.shape[1]
    assert w_mod.shape[0] == 1, "single-layer block"

    def layer(a):
        return a.reshape(a.shape[1:])

    rows = 2 * SUBLANES
    cvec = jnp.concatenate([c, c_ctx[None], jnp.zeros((rows - bsz - 1, D_MODEL), F32)], axis=0)
    mod = _adaln(cvec, layer(w_mod), b_mod).reshape(rows, N_MOD, D_MODEL)
    ctx_row = bsz

    wgu1 = layer(w_gu1)
    wd1 = layer(w_down1)
    x1 = _ffn(x, mod, g_ffn1, wgu1, wd1, None, FFN_TM)
    ctx1 = _ffn(ctx.reshape(1, bsz * n_ctx, D_MODEL), mod, g_ffn1, wgu1, wd1, ctx_row, CTX_FFN_TM)
    ctx1 = ctx1.reshape(bsz, n_ctx, D_MODEL)

    win = layer(w_in)
    o_kv = Q_RANK
    o_kr = o_kv + KV_RANK
    o_u = o_kr + QK_ROPE
    w_kr = win[:, o_kr:o_kr + QK_ROPE]
    zk = jnp.zeros((D_MODEL, QK_NOPE), F32)
    zp = jnp.zeros((D_MODEL, HEAD_PAD - QK_NOPE - QK_ROPE), F32)
    wa = jnp.concatenate([win[:, 0:o_kr], zk, w_kr, zp, zk, _rot_cols(w_kr), zp, win[:, o_u:]], axis=-1).astype(BF16)
    wuq = w_uq.reshape(Q_RANK, MLA_HEADS, QK_NOPE + QK_ROPE)
    wq_nope, wq_rope = wuq[..., :QK_NOPE], wuq[..., QK_NOPE:]
    wq_rot = _rot_cols(wq_rope.reshape(Q_RANK, -1)).reshape(wq_rope.shape)
    wq = jnp.concatenate([_pad_head_cols(wq_nope, wq_rope),
                          _pad_head_cols(jnp.zeros_like(wq_nope), wq_rot)], axis=-1).astype(BF16)
    wukv = w_ukv.reshape(KV_RANK, MLA_HEADS, QK_NOPE + V_HEAD)
    wk = _pad_head_cols(wukv[..., :QK_NOPE], jnp.zeros((KV_RANK, MLA_HEADS, QK_ROPE), F32))
    wv = wukv[..., QK_NOPE:].reshape(KV_RANK, MLA_WIDTH)
    wkv = jnp.concatenate([wk, wv], axis=-1).astype(BF16)

    cos, sin = _rope_tables(seq)
    cq_t, sq_t = _head_tables(cos, sin, 1.0, ATTN_SCALE * math.log2(math.e))
    ck_t, sk_t = _head_tables(cos, sin, 0.0, 1.0)
    one = np.ones((n_ctx, QK_ROPE), np.float32)
    ckc_t, skc_t = _head_tables(one, np.zeros_like(one), 0.0, 1.0)

    q, kx, vx, ux = _proj(x1, mod, g_mix, wa, g_cq, wq, g_ckv, wkv,
                          (cq_t, sq_t, ck_t, sk_t), None, PROJ_TM, True)
    kc, vc, uc = _proj(ctx1, mod, g_mix, wa, g_cq, wq, g_ckv, wkv,
                       (ckc_t, skc_t, ckc_t, skc_t), ctx_row, n_ctx, False)

    attn = _attention(q, kx, kc, vx, vc, ATTN_TQ)

    lre = lam_re.reshape(2, 1, N_STATE)
    lim = lam_im.reshape(2, 1, N_STATE)
    ldt = jnp.repeat(log_dt.reshape(2, SSM_GROUPS), SSM_STATE, axis=-1).reshape(2, 1, N_STATE)
    bre = jnp.swapaxes(b_re.reshape(2, SSM_GROUPS, SSM_STATE, SSM_CH), -1, -2).reshape(2, SSM_WIDTH, SSM_STATE)
    bim = jnp.swapaxes(b_im.reshape(2, SSM_GROUPS, SSM_STATE, SSM_CH), -1, -2).reshape(2, SSM_WIDTH, SSM_STATE)
    cre = c_re.reshape(2, SSM_WIDTH, SSM_STATE)
    cim = c_im.reshape(2, SSM_WIDTH, SSM_STATE)
    y_fwd, y_bwd = [_s5(uc, ux, lre, lim, ldt, bre, bim, cre, cim, S5_T, rev) for rev in (False, True)]

    return _final(x1, attn, ux, y_fwd, y_bwd, mod, d_skip, layer(w_glu).astype(BF16), g_mla_out, g_ssm_out,
                  layer(w_out).astype(BF16), g_ffn2, layer(w_gu2), layer(w_down2),
                  g_final[None], FINAL_TM)
```

```python
import functools
import math

import jax
import jax.numpy as jnp
import numpy as np
from jax import lax
from jax.experimental import pallas as pl
from jax.experimental.pallas import tpu as pltpu

D_MODEL = 1024
GRID_W = 64
N_MOD = 9
D_FF = 2816
MLA_HEADS = 12
QK_NOPE = 64
QK_ROPE = 32
V_HEAD = 64
Q_RANK = 256
KV_RANK = 128
MLA_WIDTH = MLA_HEADS * V_HEAD
SSM_GROUPS = 16
SSM_CH = 16
SSM_WIDTH = SSM_GROUPS * SSM_CH
SSM_STATE = 64
N_STATE = SSM_GROUPS * SSM_STATE
ROPE_BASE = 10000.0
ATTN_SCALE = (QK_NOPE + QK_ROPE) ** -0.5
EPS = 1e-6

LANES = 128
SUBLANES = 8
MXU_DIM = 256
HEAD_PAD = 128
VMEM_LIMIT = 56 * 1024 * 1024

FFN_TM = 1024
CTX_FFN_TM = 512
PROJ_TM = 1024
ATTN_TQ = 1024
FINAL_TM = 512

F32 = jnp.float32
BF16 = jnp.bfloat16


def _rms(x, g):
    ms = jnp.mean(x * x, axis=-1, keepdims=True)
    return x * lax.rsqrt(ms + EPS) * g


def _dot(a, b):
    return jnp.dot(a, b, preferred_element_type=F32)


def _const_spec(shape):
    zeros = (0,) * len(shape)
    return pl.BlockSpec(shape, lambda *_: zeros, pipeline_mode=pl.Buffered(1))


def _params(*sem):
    return pltpu.CompilerParams(dimension_semantics=sem, vmem_limit_bytes=VMEM_LIMIT)


def _adaln_kernel(c_ref, w_ref, b_ref, o_ref):
    c = c_ref[...]
    s = (c * jax.nn.sigmoid(c)).astype(BF16)
    o_ref[...] = _dot(s, w_ref[...].astype(BF16)) + b_ref[...]


def _adaln(cvec, w_mod, b_mod):
    rows = cvec.shape[0]
    n = w_mod.shape[1]
    bn = D_MODEL
    return pl.pallas_call(
        _adaln_kernel,
        grid=(n // bn,),
        in_specs=[pl.BlockSpec((rows, D_MODEL), lambda i: (0, 0)),
                  pl.BlockSpec((D_MODEL, bn), lambda i: (0, i)),
                  pl.BlockSpec((1, bn), lambda i: (0, i))],
        out_specs=pl.BlockSpec((rows, bn), lambda i: (0, i)),
        out_shape=jax.ShapeDtypeStruct((rows, n), F32),
        compiler_params=_params("parallel"),
        name="adaln",
    )(cvec, w_mod, b_mod)


FF_TILES = D_FF // MXU_DIM
FF_SPLITS = (0, (FF_TILES + 1) // 2 * MXU_DIM, D_FF)


def _swiglu(h, wgu_ref, wd_ref):
    acc = None
    for lo, hi in zip(FF_SPLITS[:-1], FF_SPLITS[1:]):
        gate = _dot(h, wgu_ref[:, lo:hi])
        up = _dot(h, wgu_ref[:, D_FF + lo:D_FF + hi])
        act = (gate * jax.nn.sigmoid(gate) * up).astype(BF16)
        part = _dot(act, wd_ref[lo:hi, :])
        acc = part if acc is None else acc + part
    return acc


def _ffn_kernel(x_ref, mod_ref, g_ref, wgu_ref, wd_ref, o_ref):
    x = x_ref[0]
    m = mod_ref[0]
    h = (_rms(x, g_ref[...]) * (1.0 + m[1:2]) + m[0:1]).astype(BF16)
    o_ref[0] = x + 0.5 * m[2:3] * _swiglu(h, wgu_ref, wd_ref)


def _ffn(x, mod, g, wgu, wd, mod_row, tm):
    nb, n, _ = x.shape
    mod_map = (lambda b, j: (b, 0, 0)) if mod_row is None else (lambda b, j: (mod_row, 0, 0))
    return pl.pallas_call(
        _ffn_kernel,
        grid=(nb, n // tm),
        in_specs=[pl.BlockSpec((1, tm, D_MODEL), lambda b, j: (b, j, 0)),
                  pl.BlockSpec((1, N_MOD, D_MODEL), mod_map),
                  _const_spec((1, D_MODEL)),
                  _const_spec((D_MODEL, 2 * D_FF)),
                  _const_spec((D_FF, D_MODEL))],
        out_specs=pl.BlockSpec((1, tm, D_MODEL), lambda b, j: (b, j, 0)),
        out_shape=jax.ShapeDtypeStruct(x.shape, F32),
        compiler_params=_params("parallel", "parallel"),
        name="ffn",
    )(x, mod, g, wgu, wd)


W_A_COLS = Q_RANK + KV_RANK + 2 * HEAD_PAD + SSM_WIDTH
QK_WIDTH = MLA_HEADS * HEAD_PAD


def _proj_kernel(x_ref, mod_ref, g_ref, wa_ref, gcq_ref, wq_ref, gckv_ref, wkv_ref,
                 cq_ref, sq_ref, ck_ref, sk_ref, *out_refs, with_q):
    if with_q:
        q_ref, k_ref, v_ref, u_ref = out_refs
    else:
        k_ref, v_ref, u_ref = out_refs
    x = x_ref[0]
    m = mod_ref[0]
    h = (_rms(x, g_ref[...]) * (1.0 + m[4:5]) + m[3:4]).astype(BF16)
    p = _dot(h, wa_ref[...])
    o_kv = Q_RANK
    o_kr = o_kv + KV_RANK
    o_rot = o_kr + HEAD_PAD
    o_u = o_rot + HEAD_PAD
    u_ref[...] = p[:, o_u:o_u + SSM_WIDTH]
    kr = p[:, o_kr:o_kr + HEAD_PAD] * ck_ref[...] + p[:, o_rot:o_rot + HEAD_PAD] * sk_ref[...]
    ckv = _rms(p[:, o_kv:o_kv + KV_RANK], gckv_ref[...]).astype(BF16)
    kv = _dot(ckv, wkv_ref[...])
    for hd in range(MLA_HEADS):
        lo = hd * HEAD_PAD
        k_ref[0, :, lo:lo + HEAD_PAD] = (kv[:, lo:lo + HEAD_PAD] + kr).astype(BF16)
    v_ref[0] = kv[:, QK_WIDTH:QK_WIDTH + MLA_WIDTH].astype(BF16)
    if with_q:
        cq = _rms(p[:, 0:Q_RANK], gcq_ref[...]).astype(BF16)
        qa = _dot(cq, wq_ref[:, 0:QK_WIDTH])
        qb = _dot(cq, wq_ref[:, QK_WIDTH:2 * QK_WIDTH])
        cq_t = cq_ref[...]
        sq_t = sq_ref[...]
        for hd in range(MLA_HEADS):
            lo = hd * HEAD_PAD
            q_ref[0, :, lo:lo + HEAD_PAD] = (qa[:, lo:lo + HEAD_PAD] * cq_t
                                             + qb[:, lo:lo + HEAD_PAD] * sq_t).astype(BF16)


def _proj(x, mod, g_mix, wa, gcq, wq, gckv, wkv, tabs, mod_row, tm, with_q):
    nb, n, _ = x.shape
    mod_map = (lambda b, j: (b, 0, 0)) if mod_row is None else (lambda b, j: (mod_row, 0, 0))
    tab_spec = pl.BlockSpec((tm, HEAD_PAD), lambda b, j: (j, 0))
    out_shape = [jax.ShapeDtypeStruct((nb, n, QK_WIDTH), BF16),
                 jax.ShapeDtypeStruct((nb, n, MLA_WIDTH), BF16),
                 jax.ShapeDtypeStruct((n, nb * SSM_WIDTH), F32)]
    out_specs = [pl.BlockSpec((1, tm, QK_WIDTH), lambda b, j: (b, j, 0)),
                 pl.BlockSpec((1, tm, MLA_WIDTH), lambda b, j: (b, j, 0)),
                 pl.BlockSpec((tm, SSM_WIDTH), lambda b, j: (j, b))]
    if with_q:
        out_shape = [jax.ShapeDtypeStruct((nb, n, QK_WIDTH), BF16)] + out_shape
        out_specs = [pl.BlockSpec((1, tm, QK_WIDTH), lambda b, j: (b, j, 0))] + out_specs
    return pl.pallas_call(
        functools.partial(_proj_kernel, with_q=with_q),
        grid=(nb, n // tm),
        in_specs=[pl.BlockSpec((1, tm, D_MODEL), lambda b, j: (b, j, 0)),
                  pl.BlockSpec((1, N_MOD, D_MODEL), mod_map),
                  _const_spec((1, D_MODEL)),
                  _const_spec(wa.shape),
                  _const_spec((1, Q_RANK)),
                  _const_spec(wq.shape),
                  _const_spec((1, KV_RANK)),
                  _const_spec(wkv.shape),
                  tab_spec, tab_spec, tab_spec, tab_spec],
        out_specs=out_specs,
        out_shape=out_shape,
        compiler_params=_params("parallel", "parallel"),
        name="proj_q" if with_q else "proj_ctx",
    )(x, mod, g_mix, wa, gcq, wq, gckv, wkv, *tabs)


HEADS_PER_STEP = 4
KEY_SPLITS = (0, 1024, 2048, 3072, 4096)


def _attn_kernel(q_ref, kx_ref, kc_ref, vx_ref, vc_ref, o_ref):
    nt = (((1,), (1,)), ((), ()))
    tq = q_ref.shape[1]
    assert KEY_SPLITS[-1] == kx_ref.shape[1]
    chunks = [(kx_ref, vx_ref, lo, hi - lo) for lo, hi in zip(KEY_SPLITS[:-1], KEY_SPLITS[1:])]
    chunks.append((kc_ref, vc_ref, 0, kc_ref.shape[1]))
    outs = []
    for hh in range(HEADS_PER_STEP):
        lo = hh * HEAD_PAD
        q = q_ref[0, :, lo:lo + HEAD_PAD]
        m_b = acc = None
        for k_ref, v_ref, start, size in chunks:
            s = lax.dot_general(q, k_ref[0, start:start + size, lo:lo + HEAD_PAD], nt,
                                preferred_element_type=F32)
            cols = [s[:, i * LANES:(i + 1) * LANES] for i in range(size // LANES)]
            c_vec = functools.reduce(jnp.maximum, cols)
            c_max = jnp.broadcast_to(jnp.max(c_vec, axis=-1, keepdims=True), (tq, LANES))
            m_new = c_max if m_b is None else jnp.maximum(m_b, c_max)
            p = jnp.concatenate([jnp.exp2(col - m_new).astype(BF16) for col in cols], axis=-1)
            v_lo = (hh // 2) * 2 * V_HEAD
            v_ext = jnp.concatenate([v_ref[0, start:start + size, v_lo:v_lo + 2 * V_HEAD],
                                     jnp.ones((size, LANES), BF16)], axis=-1)
            pv = _dot(p, v_ext)
            if m_b is None:
                acc = pv
            else:
                alpha = jnp.exp2(m_b - m_new)
                acc = jnp.concatenate([alpha, alpha], axis=-1) * acc + pv
            m_b = m_new
        outs.append(acc[:, 0:LANES] / acc[:, LANES:2 * LANES])
    lane = lax.broadcasted_iota(jnp.int32, outs[0].shape, 1)
    for pr in range(HEADS_PER_STEP // 2):
        o_ref[0, :, pr * 2 * V_HEAD:(pr + 1) * 2 * V_HEAD] = jnp.where(lane < V_HEAD, outs[2 * pr], outs[2 * pr + 1])


def _attention(q, kx, kc, vx, vc, tq):
    nb, n, _ = q.shape
    nk = kx.shape[1]
    nc = kc.shape[1]
    pairs = MLA_HEADS // HEADS_PER_STEP
    qk_blk = HEADS_PER_STEP * HEAD_PAD
    v_blk = HEADS_PER_STEP * V_HEAD
    return pl.pallas_call(
        _attn_kernel,
        grid=(nb, pairs, n // tq),
        in_specs=[pl.BlockSpec((1, tq, qk_blk), lambda b, p, j: (b, j, p)),
                  pl.BlockSpec((1, nk, qk_blk), lambda b, p, j: (b, 0, p)),
                  pl.BlockSpec((1, nc, qk_blk), lambda b, p, j: (b, 0, p)),
                  pl.BlockSpec((1, nk, v_blk), lambda b, p, j: (b, 0, p)),
                  pl.BlockSpec((1, nc, v_blk), lambda b, p, j: (b, 0, p))],
        out_specs=pl.BlockSpec((1, tq, v_blk), lambda b, p, j: (b, j, p)),
        out_shape=jax.ShapeDtypeStruct((nb, n, MLA_WIDTH), F32),
        compiler_params=_params("parallel", "parallel", "parallel"),
        name="attention",
    )(q, kx, kc, vx, vc)


SCAN_BLOCK = 32


def _s5_kernel(uc_ref, ux_ref, lre_ref, lim_ref, ldt_ref, bre_ref, bim_ref, cre_ref, cim_ref,
               y_ref, bb_ref, cm_ref, a_ref, h_ref, bd_ref, xb_ref, *, t_chunk, reverse):
    j = pl.program_id(0)

    @pl.when(j == 0)
    def _discretize():
        dt = jnp.exp(ldt_ref[...])
        lr = jnp.minimum(lre_ref[...], -1e-4)
        li = lim_ref[...]
        mag = jnp.exp(lr * dt)
        ar = mag * jnp.cos(li * dt)
        ai = mag * jnp.sin(li * dt)
        den = lr * lr + li * li
        fr = ((ar - 1.0) * lr + ai * li) / den
        fi = (ai * lr - (ar - 1.0) * li) / den
        same_group = (lax.broadcasted_iota(jnp.int32, (SSM_WIDTH, N_STATE), 0) // SSM_CH
                      == lax.broadcasted_iota(jnp.int32, (SSM_WIDTH, N_STATE), 1) // SSM_STATE)

        def block_diag(ref):
            return jnp.where(same_group, jnp.concatenate([ref[...]] * SSM_GROUPS, axis=-1), 0.0)

        bre = block_diag(bre_ref)
        bim = block_diag(bim_ref)
        bb_ref[:, 0:N_STATE] = (fr * bre - fi * bim).astype(BF16)
        bb_ref[:, N_STATE:2 * N_STATE] = (fr * bim + fi * bre).astype(BF16)
        cm_ref[0:N_STATE, :] = block_diag(cre_ref).T.astype(BF16)
        cm_ref[N_STATE:2 * N_STATE, :] = (-block_diag(cim_ref)).T.astype(BF16)
        a_ref[0:SUBLANES, :] = jnp.broadcast_to(ar, (SUBLANES, N_STATE))
        a_ref[SUBLANES:2 * SUBLANES, :] = jnp.broadcast_to(ai, (SUBLANES, N_STATE))
        h_ref[...] = jnp.zeros_like(h_ref)

    blk_rows = SCAN_BLOCK * SUBLANES
    order = list(range(t_chunk // SCAN_BLOCK))
    steps = list(range(SCAN_BLOCK))
    if reverse:
        order.reverse()
        steps.reverse()

    def run(u_ref, with_readout):
        ar = a_ref[0:SUBLANES, :]
        ai = a_ref[SUBLANES:2 * SUBLANES, :]

        def drive(sb):
            u = u_ref[sb * SCAN_BLOCK:(sb + 1) * SCAN_BLOCK, :].reshape(blk_rows, SSM_WIDTH)
            bd_ref[sb * blk_rows:(sb + 1) * blk_rows, :] = _dot(u.astype(BF16), bb_ref[...])

        def scan(sb, hr, hi):
            for k in range(0, SCAN_BLOCK, 2):
                pair = {}
                for t in (steps[k], steps[k + 1]):
                    r = sb * blk_rows + t * SUBLANES
                    nr = ar * hr - ai * hi + bd_ref[r:r + SUBLANES, 0:N_STATE]
                    ni = ar * hi + ai * hr + bd_ref[r:r + SUBLANES, N_STATE:2 * N_STATE]
                    pair[t] = (nr, ni)
                    hr, hi = nr, ni
                if not with_readout:
                    continue
                t0 = min(pair)
                r = sb * blk_rows + t0 * SUBLANES
                xb_ref[r:r + 2 * SUBLANES, 0:N_STATE] = jnp.concatenate(
                    [pair[t0][0], pair[t0 + 1][0]], axis=0).astype(BF16)
                xb_ref[r:r + 2 * SUBLANES, N_STATE:2 * N_STATE] = jnp.concatenate(
                    [pair[t0][1], pair[t0 + 1][1]], axis=0).astype(BF16)
            return hr, hi

        def readout(sb):
            y = _dot(xb_ref[sb * blk_rows:(sb + 1) * blk_rows, :], cm_ref[...])
            y_ref[sb * SCAN_BLOCK:(sb + 1) * SCAN_BLOCK, :] = y.reshape(SCAN_BLOCK, SUBLANES * SSM_WIDTH)

        hr = h_ref[0:SUBLANES, :]
        hi = h_ref[SUBLANES:2 * SUBLANES, :]
        drive(order[0])
        for k, sb in enumerate(order):
            if k + 1 < len(order):
                drive(order[k + 1])
            hr, hi = scan(sb, hr, hi)
            if with_readout and k >= 1:
                readout(order[k - 1])
        if with_readout:
            readout(order[-1])
        h_ref[0:SUBLANES, :] = hr
        h_ref[SUBLANES:2 * SUBLANES, :] = hi

    @pl.when(j == 0)
    def _context():
        run(uc_ref, False)

    @pl.when(j > 0)
    def _latent():
        run(ux_ref, True)


def _s5(u_c, u_x, lre, lim, ldt, bre, bim, cre, cim, t_chunk, reverse):
    rows = t_chunk * SUBLANES
    width = SUBLANES * SSM_WIDTH
    assert u_c.shape == (t_chunk, width) and u_x.shape[1] == width
    nch = u_x.shape[0] // t_chunk

    def chunk(j):
        jj = jnp.maximum(j - 1, 0)
        return nch - 1 - jj if reverse else jj

    d = int(reverse)
    vec_spec = pl.BlockSpec((None, 1, N_STATE), lambda j: (d, 0, 0), pipeline_mode=pl.Buffered(1))
    mat_spec = pl.BlockSpec((None, SSM_WIDTH, SSM_STATE), lambda j: (d, 0, 0), pipeline_mode=pl.Buffered(1))
    return pl.pallas_call(
        functools.partial(_s5_kernel, t_chunk=t_chunk, reverse=reverse),
        grid=(nch + 1,),
        in_specs=[pl.BlockSpec((t_chunk, width), lambda j: (0, 0)),
                  pl.BlockSpec((t_chunk, width), lambda j: (chunk(j), 0)),
                  vec_spec, vec_spec, vec_spec, mat_spec, mat_spec, mat_spec, mat_spec],
        out_specs=pl.BlockSpec((t_chunk, width), lambda j: (chunk(j), 0)),
        out_shape=jax.ShapeDtypeStruct((u_x.shape[0], width), F32),
        scratch_shapes=[pltpu.VMEM((SSM_WIDTH, 2 * N_STATE), BF16),
                        pltpu.VMEM((2 * N_STATE, SSM_WIDTH), BF16),
                        pltpu.VMEM((2 * SUBLANES, N_STATE), F32),
                        pltpu.VMEM((2 * SUBLANES, N_STATE), F32),
                        pltpu.VMEM((rows, 2 * N_STATE), F32),
                        pltpu.VMEM((rows, 2 * N_STATE), BF16)],
        compiler_params=_params("arbitrary"),
        name="s5_rev" if reverse else "s5_fwd",
    )(u_c, u_x, lre, lim, ldt, bre, bim, cre, cim)


def _gelu_tanh(x):
    return 0.5 * x * (1.0 + jnp.tanh(math.sqrt(2.0 / math.pi) * (x + 0.044715 * (x * x * x))))


def _final_kernel(x_ref, attn_ref, u_ref, yf_ref, yb_ref, mod_ref, dsk_ref, wglu_ref, gmla_ref, gssm_ref,
                  wo_ref, g2_ref, wgu_ref, wd_ref, gf_ref, o_ref):
    x = x_ref[0]
    m = mod_ref[0]
    y = u_ref[...] * dsk_ref[...] + yf_ref[...] + yb_ref[...]
    z = _gelu_tanh(y).astype(BF16)
    ag = _dot(z, wglu_ref[...])
    ssm = ag[:, 0:SSM_WIDTH] * jax.nn.sigmoid(ag[:, SSM_WIDTH:2 * SSM_WIDTH])
    a_n = _rms(attn_ref[0], gmla_ref[...]).astype(BF16)
    s_n = _rms(ssm, gssm_ref[...]).astype(BF16)
    mix = _dot(a_n, wo_ref[0:MLA_WIDTH, :]) + _dot(s_n, wo_ref[MLA_WIDTH:MLA_WIDTH + SSM_WIDTH, :])
    x = x + m[5:6] * mix
    h = (_rms(x, g2_ref[...]) * (1.0 + m[7:8]) + m[6:7]).astype(BF16)
    x = x + 0.5 * m[8:9] * _swiglu(h, wgu_ref, wd_ref)
    o_ref[0] = _rms(x, gf_ref[...])


def _final(x, attn, u, y_fwd, y_bwd, mod, dsk, wglu, gmla, gssm, wo, g2, wgu, wd, gf, tm):
    nb, n, _ = x.shape
    ssm_spec = pl.BlockSpec((tm, SSM_WIDTH), lambda b, j: (j, b))
    return pl.pallas_call(
        _final_kernel,
        grid=(nb, n // tm),
        in_specs=[pl.BlockSpec((1, tm, D_MODEL), lambda b, j: (b, j, 0)),
                  pl.BlockSpec((1, tm, MLA_WIDTH), lambda b, j: (b, j, 0)),
                  ssm_spec, ssm_spec, ssm_spec,
                  pl.BlockSpec((1, N_MOD, D_MODEL), lambda b, j: (b, 0, 0)),
                  _const_spec((1, SSM_WIDTH)),
                  _const_spec((SSM_WIDTH, 2 * SSM_WIDTH)),
                  _const_spec((1, MLA_WIDTH)),
                  _const_spec((1, SSM_WIDTH)),
                  _const_spec((MLA_WIDTH + SSM_WIDTH, D_MODEL)),
                  _const_spec((1, D_MODEL)),
                  _const_spec((D_MODEL, 2 * D_FF)),
                  _const_spec((D_FF, D_MODEL)),
                  _const_spec((1, D_MODEL))],
        out_specs=pl.BlockSpec((1, tm, D_MODEL), lambda b, j: (b, j, 0)),
        out_shape=jax.ShapeDtypeStruct(x.shape, F32),
        compiler_params=_params("parallel", "parallel"),
        name="final",
    )(x, attn, u, y_fwd, y_bwd, mod, dsk, wglu, gmla, gssm, wo, g2, wgu, wd, gf)


def _rot_cols(w):
    wp = w.reshape(w.shape[0], -1, 2)
    return jnp.stack([-wp[..., 1], wp[..., 0]], axis=-1).reshape(w.shape)


def _rope_tables(n_tokens):
    rows = n_tokens // GRID_W
    row = np.repeat(np.arange(rows), GRID_W).astype(np.float32)
    col = np.tile(np.arange(GRID_W), rows).astype(np.float32)
    per_axis = QK_ROPE // 2
    inv_freq = (ROPE_BASE ** (-np.arange(0, per_axis, 2, dtype=np.float32) / per_axis)).astype(np.float32)
    ang = np.concatenate([row[:, None] * inv_freq, col[:, None] * inv_freq], axis=-1)
    cos = np.repeat(np.cos(ang), 2, axis=-1).astype(np.float32)
    sin = np.repeat(np.sin(ang), 2, axis=-1).astype(np.float32)
    return cos, sin


def _head_tables(cos, sin, nope_fill, scale):
    n = cos.shape[0]
    pad = HEAD_PAD - QK_NOPE - QK_ROPE
    c = np.concatenate([np.full((n, QK_NOPE), nope_fill, np.float32), cos, np.zeros((n, pad), np.float32)], axis=-1)
    s = np.concatenate([np.zeros((n, QK_NOPE), np.float32), sin, np.zeros((n, pad), np.float32)], axis=-1)
    return (c * np.float32(scale)).astype(np.float32), (s * np.float32(scale)).astype(np.float32)


def _pad_head_cols(w_nope, w_rope):
    k, h = w_nope.shape[:2]
    pad = HEAD_PAD - QK_NOPE - QK_ROPE
    return jnp.concatenate([w_nope, w_rope, jnp.zeros((k, h, pad), w_nope.dtype)], axis=-1).reshape(k, h * HEAD_PAD)


def kernel(x, c, ctx, c_ctx, w_mod, b_mod, g_ffn1, w_gu1, w_down1, g_mix, w_in, g_cq, w_uq, g_ckv, w_ukv,
           lam_re, lam_im, log_dt, b_re, b_im, c_re, c_im, d_skip, w_glu, g_mla_out, g_ssm_out, w_out,
           g_ffn2, w_gu2, w_down2, g_final):
    bsz, seq, _ = x.shape
    n_ctx = ctx.shape[1]
    assert w_mod.shape[0] == 1, "single-layer block"

    def layer(a):
        return a.reshape(a.shape[1:])

    rows = 2 * SUBLANES
    cvec = jnp.concatenate([c, c_ctx[None], jnp.zeros((rows - bsz - 1, D_MODEL), F32)], axis=0)
    mod = _adaln(cvec, layer(w_mod), b_mod).reshape(rows, N_MOD, D_MODEL)
    ctx_row = bsz

    wgu1 = layer(w_gu1).astype(BF16)
    wd1 = layer(w_down1).astype(BF16)
    x1 = _ffn(x, mod, g_ffn1, wgu1, wd1, None, FFN_TM)
    ctx1 = _ffn(ctx.reshape(1, bsz * n_ctx, D_MODEL), mod, g_ffn1, wgu1, wd1, ctx_row, CTX_FFN_TM)
    ctx1 = ctx1.reshape(bsz, n_ctx, D_MODEL)

    win = layer(w_in)
    o_kv = Q_RANK
    o_kr = o_kv + KV_RANK
    o_u = o_kr + QK_ROPE
    w_kr = win[:, o_kr:o_kr + QK_ROPE]
    zk = jnp.zeros((D_MODEL, QK_NOPE), F32)
    zp = jnp.zeros((D_MODEL, HEAD_PAD - QK_NOPE - QK_ROPE), F32)
    wa = jnp.concatenate([win[:, 0:o_kr], zk, w_kr, zp, zk, _rot_cols(w_kr), zp, win[:, o_u:]], axis=-1).astype(BF16)
    wuq = w_uq.reshape(Q_RANK, MLA_HEADS, QK_NOPE + QK_ROPE)
    wq_nope, wq_rope = wuq[..., :QK_NOPE], wuq[..., QK_NOPE:]
    wq_rot = _rot_cols(wq_rope.reshape(Q_RANK, -1)).reshape(wq_rope.shape)
    wq = jnp.concatenate([_pad_head_cols(wq_nope, wq_rope),
                          _pad_head_cols(jnp.zeros_like(wq_nope), wq_rot)], axis=-1).astype(BF16)
    wukv = w_ukv.reshape(KV_RANK, MLA_HEADS, QK_NOPE + V_HEAD)
    wk = _pad_head_cols(wukv[..., :QK_NOPE], jnp.zeros((KV_RANK, MLA_HEADS, QK_ROPE), F32))
    wv = wukv[..., QK_NOPE:].reshape(KV_RANK, MLA_WIDTH)
    wkv = jnp.concatenate([wk, wv], axis=-1).astype(BF16)

    cos, sin = _rope_tables(seq)
    cq_t, sq_t = _head_tables(cos, sin, 1.0, ATTN_SCALE * math.log2(math.e))
    ck_t, sk_t = _head_tables(cos, sin, 0.0, 1.0)
    one = np.ones((n_ctx, QK_ROPE), np.float32)
    ckc_t, skc_t = _head_tables(one, np.zeros_like(one), 0.0, 1.0)

    q, kx, vx, ux = _proj(x1, mod, g_mix, wa, g_cq, wq, g_ckv, wkv,
                          (cq_t, sq_t, ck_t, sk_t), None, PROJ_TM, True)
    kc, vc, uc = _proj(ctx1, mod, g_mix, wa, g_cq, wq, g_ckv, wkv,
                       (ckc_t, skc_t, ckc_t, skc_t), ctx_row, n_ctx, False)

    attn = _attention(q, kx, kc, vx, vc, ATTN_TQ)

    lre = lam_re.reshape(2, 1, N_STATE)
    lim = lam_im.reshape(2, 1, N_STATE)
    ldt = jnp.repeat(log_dt.reshape(2, SSM_GROUPS), SSM_STATE, axis=-1).reshape(2, 1, N_STATE)
    bre = jnp.swapaxes(b_re.reshape(2, SSM_GROUPS, SSM_STATE, SSM_CH), -1, -2).reshape(2, SSM_WIDTH, SSM_STATE)
    bim = jnp.swapaxes(b_im.reshape(2, SSM_GROUPS, SSM_STATE, SSM_CH), -1, -2).reshape(2, SSM_WIDTH, SSM_STATE)
    cre = c_re.reshape(2, SSM_WIDTH, SSM_STATE)
    cim = c_im.reshape(2, SSM_WIDTH, SSM_STATE)
    y_fwd, y_bwd = [_s5(uc, ux, lre, lim, ldt, bre, bim, cre, cim, n_ctx, rev) for rev in (False, True)]

    return _final(x1, attn, ux, y_fwd, y_bwd, mod, d_skip, layer(w_glu).astype(BF16), g_mla_out, g_ssm_out,
                  layer(w_out).astype(BF16), g_ffn2, layer(w_gu2).astype(BF16), layer(w_down2).astype(BF16),
                  g_final[None], FINAL_TM)
```

```python
import functools
import math

import jax
import jax.numpy as jnp
import numpy as np
from jax import lax
from jax.experimental import pallas as pl
from jax.experimental.pallas import tpu as pltpu

D_MODEL = 1024
GRID_W = 64
N_MOD = 9
D_FF = 2816
MLA_HEADS = 12
QK_NOPE = 64
QK_ROPE = 32
V_HEAD = 64
Q_RANK = 256
KV_RANK = 128
MLA_WIDTH = MLA_HEADS * V_HEAD
SSM_GROUPS = 16
SSM_CH = 16
SSM_WIDTH = SSM_GROUPS * SSM_CH
SSM_STATE = 64
N_STATE = SSM_GROUPS * SSM_STATE
ROPE_BASE = 10000.0
ATTN_SCALE = (QK_NOPE + QK_ROPE) ** -0.5
EPS = 1e-6

LANES = 128
SUBLANES = 8
MXU_DIM = 256
HEAD_PAD = 128
VMEM_LIMIT = 56 * 1024 * 1024

FFN_TM = 1024
CTX_FFN_TM = 512
PROJ_TM = 1024
ATTN_TQ = 1024
FINAL_TM = 512

F32 = jnp.float32
BF16 = jnp.bfloat16


def _rms(x, g):
    ms = jnp.mean(x * x, axis=-1, keepdims=True)
    return x * lax.rsqrt(ms + EPS) * g


def _dot(a, b):
    return jnp.dot(a, b, preferred_element_type=F32)


def _const_spec(shape):
    zeros = (0,) * len(shape)
    return pl.BlockSpec(shape, lambda *_: zeros, pipeline_mode=pl.Buffered(1))


def _params(*sem):
    return pltpu.CompilerParams(dimension_semantics=sem, vmem_limit_bytes=VMEM_LIMIT)


def _adaln_kernel(c_ref, w_ref, b_ref, o_ref):
    c = c_ref[...]
    s = (c * jax.nn.sigmoid(c)).astype(BF16)
    o_ref[...] = _dot(s, w_ref[...].astype(BF16)) + b_ref[...]


def _adaln(cvec, w_mod, b_mod):
    rows = cvec.shape[0]
    n = w_mod.shape[1]
    bn = D_MODEL
    return pl.pallas_call(
        _adaln_kernel,
        grid=(n // bn,),
        in_specs=[pl.BlockSpec((rows, D_MODEL), lambda i: (0, 0)),
                  pl.BlockSpec((D_MODEL, bn), lambda i: (0, i)),
                  pl.BlockSpec((1, bn), lambda i: (0, i))],
        out_specs=pl.BlockSpec((rows, bn), lambda i: (0, i)),
        out_shape=jax.ShapeDtypeStruct((rows, n), F32),
        compiler_params=_params("parallel"),
        name="adaln",
    )(cvec, w_mod, b_mod)


FF_TILES = D_FF // MXU_DIM
FF_SPLITS = (0, (FF_TILES + 1) // 2 * MXU_DIM, D_FF)


def _swiglu(h, wgu_ref, wd_ref):
    acc = None
    for lo, hi in zip(FF_SPLITS[:-1], FF_SPLITS[1:]):
        gate = _dot(h, wgu_ref[:, lo:hi])
        up = _dot(h, wgu_ref[:, D_FF + lo:D_FF + hi])
        act = (gate * jax.nn.sigmoid(gate) * up).astype(BF16)
        part = _dot(act, wd_ref[lo:hi, :])
        acc = part if acc is None else acc + part
    return acc


def _ffn_kernel(x_ref, mod_ref, g_ref, wgu_ref, wd_ref, o_ref):
    x = x_ref[0]
    m = mod_ref[0]
    h = (_rms(x, g_ref[...]) * (1.0 + m[1:2]) + m[0:1]).astype(BF16)
    o_ref[0] = x + 0.5 * m[2:3] * _swiglu(h, wgu_ref, wd_ref)


def _ffn(x, mod, g, wgu, wd, mod_row, tm):
    nb, n, _ = x.shape
    mod_map = (lambda b, j: (b, 0, 0)) if mod_row is None else (lambda b, j: (mod_row, 0, 0))
    return pl.pallas_call(
        _ffn_kernel,
        grid=(nb, n // tm),
        in_specs=[pl.BlockSpec((1, tm, D_MODEL), lambda b, j: (b, j, 0)),
                  pl.BlockSpec((1, N_MOD, D_MODEL), mod_map),
                  _const_spec((1, D_MODEL)),
                  _const_spec((D_MODEL, 2 * D_FF)),
                  _const_spec((D_FF, D_MODEL))],
        out_specs=pl.BlockSpec((1, tm, D_MODEL), lambda b, j: (b, j, 0)),
        out_shape=jax.ShapeDtypeStruct(x.shape, F32),
        compiler_params=_params("parallel", "parallel"),
        name="ffn",
    )(x, mod, g, wgu, wd)


W_A_COLS = Q_RANK + KV_RANK + 2 * HEAD_PAD + SSM_WIDTH
QK_WIDTH = MLA_HEADS * HEAD_PAD


def _proj_kernel(x_ref, mod_ref, g_ref, wa_ref, gcq_ref, wq_ref, gckv_ref, wkv_ref,
                 cq_ref, sq_ref, ck_ref, sk_ref, *out_refs, with_q):
    if with_q:
        q_ref, k_ref, v_ref, u_ref = out_refs
    else:
        k_ref, v_ref, u_ref = out_refs
    x = x_ref[0]
    m = mod_ref[0]
    h = (_rms(x, g_ref[...]) * (1.0 + m[4:5]) + m[3:4]).astype(BF16)
    p = _dot(h, wa_ref[...])
    o_kv = Q_RANK
    o_kr = o_kv + KV_RANK
    o_rot = o_kr + HEAD_PAD
    o_u = o_rot + HEAD_PAD
    u_ref[...] = p[:, o_u:o_u + SSM_WIDTH]
    kr = p[:, o_kr:o_kr + HEAD_PAD] * ck_ref[...] + p[:, o_rot:o_rot + HEAD_PAD] * sk_ref[...]
    ckv = _rms(p[:, o_kv:o_kv + KV_RANK], gckv_ref[...]).astype(BF16)
    kv = _dot(ckv, wkv_ref[...])
    for hd in range(MLA_HEADS):
        lo = hd * HEAD_PAD
        k_ref[0, :, lo:lo + HEAD_PAD] = (kv[:, lo:lo + HEAD_PAD] + kr).astype(BF16)
    v_ref[0] = kv[:, QK_WIDTH:QK_WIDTH + MLA_WIDTH].astype(BF16)
    if with_q:
        cq = _rms(p[:, 0:Q_RANK], gcq_ref[...]).astype(BF16)
        qa = _dot(cq, wq_ref[:, 0:QK_WIDTH])
        qb = _dot(cq, wq_ref[:, QK_WIDTH:2 * QK_WIDTH])
        cq_t = cq_ref[...]
        sq_t = sq_ref[...]
        for hd in range(MLA_HEADS):
            lo = hd * HEAD_PAD
            q_ref[0, :, lo:lo + HEAD_PAD] = (qa[:, lo:lo + HEAD_PAD] * cq_t
                                             + qb[:, lo:lo + HEAD_PAD] * sq_t).astype(BF16)


def _proj(x, mod, g_mix, wa, gcq, wq, gckv, wkv, tabs, mod_row, tm, with_q):
    nb, n, _ = x.shape
    mod_map = (lambda b, j: (b, 0, 0)) if mod_row is None else (lambda b, j: (mod_row, 0, 0))
    tab_spec = pl.BlockSpec((tm, HEAD_PAD), lambda b, j: (j, 0))
    out_shape = [jax.ShapeDtypeStruct((nb, n, QK_WIDTH), BF16),
                 jax.ShapeDtypeStruct((nb, n, MLA_WIDTH), BF16),
                 jax.ShapeDtypeStruct((n, nb * SSM_WIDTH), F32)]
    out_specs = [pl.BlockSpec((1, tm, QK_WIDTH), lambda b, j: (b, j, 0)),
                 pl.BlockSpec((1, tm, MLA_WIDTH), lambda b, j: (b, j, 0)),
                 pl.BlockSpec((tm, SSM_WIDTH), lambda b, j: (j, b))]
    if with_q:
        out_shape = [jax.ShapeDtypeStruct((nb, n, QK_WIDTH), BF16)] + out_shape
        out_specs = [pl.BlockSpec((1, tm, QK_WIDTH), lambda b, j: (b, j, 0))] + out_specs
    return pl.pallas_call(
        functools.partial(_proj_kernel, with_q=with_q),
        grid=(nb, n // tm),
        in_specs=[pl.BlockSpec((1, tm, D_MODEL), lambda b, j: (b, j, 0)),
                  pl.BlockSpec((1, N_MOD, D_MODEL), mod_map),
                  _const_spec((1, D_MODEL)),
                  _const_spec(wa.shape),
                  _const_spec((1, Q_RANK)),
                  _const_spec(wq.shape),
                  _const_spec((1, KV_RANK)),
                  _const_spec(wkv.shape),
                  tab_spec, tab_spec, tab_spec, tab_spec],
        out_specs=out_specs,
        out_shape=out_shape,
        compiler_params=_params("parallel", "parallel"),
        name="proj_q" if with_q else "proj_ctx",
    )(x, mod, g_mix, wa, gcq, wq, gckv, wkv, *tabs)


HEADS_PER_STEP = 4
KEY_SPLITS = (0, 1024, 2048, 3072, 4096)


def _attn_kernel(q_ref, kx_ref, kc_ref, vx_ref, vc_ref, o_ref):
    nt = (((1,), (1,)), ((), ()))
    tq = q_ref.shape[1]
    assert KEY_SPLITS[-1] == kx_ref.shape[1]
    chunks = [(kx_ref, vx_ref, lo, hi - lo) for lo, hi in zip(KEY_SPLITS[:-1], KEY_SPLITS[1:])]
    chunks.append((kc_ref, vc_ref, 0, kc_ref.shape[1]))
    outs = []
    for hh in range(HEADS_PER_STEP):
        lo = hh * HEAD_PAD
        q = q_ref[0, :, lo:lo + HEAD_PAD]
        m_b = acc = None
        for k_ref, v_ref, start, size in chunks:
            s = lax.dot_general(q, k_ref[0, start:start + size, lo:lo + HEAD_PAD], nt,
                                preferred_element_type=F32)
            cols = [s[:, i * LANES:(i + 1) * LANES] for i in range(size // LANES)]
            c_vec = functools.reduce(jnp.maximum, cols)
            c_max = jnp.broadcast_to(jnp.max(c_vec, axis=-1, keepdims=True), (tq, LANES))
            m_new = c_max if m_b is None else jnp.maximum(m_b, c_max)
            p = jnp.concatenate([jnp.exp2(col - m_new).astype(BF16) for col in cols], axis=-1)
            v_lo = (hh // 2) * 2 * V_HEAD
            v_ext = jnp.concatenate([v_ref[0, start:start + size, v_lo:v_lo + 2 * V_HEAD],
                                     jnp.ones((size, LANES), BF16)], axis=-1)
            pv = _dot(p, v_ext)
            if m_b is None:
                acc = pv
            else:
                alpha = jnp.exp2(m_b - m_new)
                acc = jnp.concatenate([alpha, alpha], axis=-1) * acc + pv
            m_b = m_new
        outs.append(acc[:, 0:LANES] / acc[:, LANES:2 * LANES])
    lane = lax.broadcasted_iota(jnp.int32, outs[0].shape, 1)
    for pr in range(HEADS_PER_STEP // 2):
        o_ref[0, :, pr * 2 * V_HEAD:(pr + 1) * 2 * V_HEAD] = jnp.where(lane < V_HEAD, outs[2 * pr], outs[2 * pr + 1])


def _attention(q, kx, kc, vx, vc, tq):
    nb, n, _ = q.shape
    nk = kx.shape[1]
    nc = kc.shape[1]
    pairs = MLA_HEADS // HEADS_PER_STEP
    qk_blk = HEADS_PER_STEP * HEAD_PAD
    v_blk = HEADS_PER_STEP * V_HEAD
    return pl.pallas_call(
        _attn_kernel,
        grid=(nb, pairs, n // tq),
        in_specs=[pl.BlockSpec((1, tq, qk_blk), lambda b, p, j: (b, j, p)),
                  pl.BlockSpec((1, nk, qk_blk), lambda b, p, j: (b, 0, p)),
                  pl.BlockSpec((1, nc, qk_blk), lambda b, p, j: (b, 0, p)),
                  pl.BlockSpec((1, nk, v_blk), lambda b, p, j: (b, 0, p)),
                  pl.BlockSpec((1, nc, v_blk), lambda b, p, j: (b, 0, p))],
        out_specs=pl.BlockSpec((1, tq, v_blk), lambda b, p, j: (b, j, p)),
        out_shape=jax.ShapeDtypeStruct((nb, n, MLA_WIDTH), F32),
        compiler_params=_params("parallel", "parallel", "parallel"),
        name="attention",
    )(q, kx, kc, vx, vc)


SCAN_BLOCK = 16


def _s5_kernel(uc_ref, ux_ref, lre_ref, lim_ref, ldt_ref, bre_ref, bim_ref, cre_ref, cim_ref,
               y_ref, bb_ref, cm_ref, a_ref, h_ref, bd_ref, xb_ref, *, t_chunk, reverse):
    j = pl.program_id(0)

    @pl.when(j == 0)
    def _discretize():
        dt = jnp.exp(ldt_ref[...])
        lr = jnp.minimum(lre_ref[...], -1e-4)
        li = lim_ref[...]
        mag = jnp.exp(lr * dt)
        ar = mag * jnp.cos(li * dt)
        ai = mag * jnp.sin(li * dt)
        den = lr * lr + li * li
        fr = ((ar - 1.0) * lr + ai * li) / den
        fi = (ai * lr - (ar - 1.0) * li) / den
        same_group = (lax.broadcasted_iota(jnp.int32, (SSM_WIDTH, N_STATE), 0) // SSM_CH
                      == lax.broadcasted_iota(jnp.int32, (SSM_WIDTH, N_STATE), 1) // SSM_STATE)

        def block_diag(ref):
            return jnp.where(same_group, jnp.concatenate([ref[...]] * SSM_GROUPS, axis=-1), 0.0)

        bre = block_diag(bre_ref)
        bim = block_diag(bim_ref)
        bb_ref[:, 0:N_STATE] = (fr * bre - fi * bim).astype(BF16)
        bb_ref[:, N_STATE:2 * N_STATE] = (fr * bim + fi * bre).astype(BF16)
        cm_ref[0:N_STATE, :] = block_diag(cre_ref).T.astype(BF16)
        cm_ref[N_STATE:2 * N_STATE, :] = (-block_diag(cim_ref)).T.astype(BF16)
        a_ref[0:SUBLANES, :] = jnp.broadcast_to(ar, (SUBLANES, N_STATE))
        a_ref[SUBLANES:2 * SUBLANES, :] = jnp.broadcast_to(ai, (SUBLANES, N_STATE))
        h_ref[...] = jnp.zeros_like(h_ref)

    blk_rows = SCAN_BLOCK * SUBLANES
    order = list(range(t_chunk // SCAN_BLOCK))
    steps = list(range(SCAN_BLOCK))
    if reverse:
        order.reverse()
        steps.reverse()

    def run(u_ref, with_readout):
        ar = a_ref[0:SUBLANES, :]
        ai = a_ref[SUBLANES:2 * SUBLANES, :]

        def drive(sb):
            u = u_ref[sb * SCAN_BLOCK:(sb + 1) * SCAN_BLOCK, :].reshape(blk_rows, SSM_WIDTH)
            bd_ref[sb * blk_rows:(sb + 1) * blk_rows, :] = _dot(u.astype(BF16), bb_ref[...])

        def scan(sb, hr, hi):
            for k in range(0, SCAN_BLOCK, 2):
                pair = {}
                for t in (steps[k], steps[k + 1]):
                    r = sb * blk_rows + t * SUBLANES
                    nr = ar * hr - ai * hi + bd_ref[r:r + SUBLANES, 0:N_STATE]
                    ni = ar * hi + ai * hr + bd_ref[r:r + SUBLANES, N_STATE:2 * N_STATE]
                    pair[t] = (nr, ni)
                    hr, hi = nr, ni
                if not with_readout:
                    continue
                t0 = min(pair)
                r = sb * blk_rows + t0 * SUBLANES
                xb_ref[r:r + 2 * SUBLANES, 0:N_STATE] = jnp.concatenate(
                    [pair[t0][0], pair[t0 + 1][0]], axis=0).astype(BF16)
                xb_ref[r:r + 2 * SUBLANES, N_STATE:2 * N_STATE] = jnp.concatenate(
                    [pair[t0][1], pair[t0 + 1][1]], axis=0).astype(BF16)
            return hr, hi

        def readout(sb):
            y = _dot(xb_ref[sb * blk_rows:(sb + 1) * blk_rows, :], cm_ref[...])
            y_ref[sb * SCAN_BLOCK:(sb + 1) * SCAN_BLOCK, :] = y.reshape(SCAN_BLOCK, SUBLANES * SSM_WIDTH)

        hr = h_ref[0:SUBLANES, :]
        hi = h_ref[SUBLANES:2 * SUBLANES, :]
        drive(order[0])
        for k, sb in enumerate(order):
            if k + 1 < len(order):
                drive(order[k + 1])
            hr, hi = scan(sb, hr, hi)
            if with_readout and k >= 1:
                readout(order[k - 1])
        if with_readout:
            readout(order[-1])
        h_ref[0:SUBLANES, :] = hr
        h_ref[SUBLANES:2 * SUBLANES, :] = hi

    @pl.when(j == 0)
    def _context():
        run(uc_ref, False)

    @pl.when(j > 0)
    def _latent():
        run(ux_ref, True)


def _s5(u_c, u_x, lre, lim, ldt, bre, bim, cre, cim, t_chunk, reverse):
    rows = t_chunk * SUBLANES
    width = SUBLANES * SSM_WIDTH
    assert u_c.shape == (t_chunk, width) and u_x.shape[1] == width
    nch = u_x.shape[0] // t_chunk

    def chunk(j):
        jj = jnp.maximum(j - 1, 0)
        return nch - 1 - jj if reverse else jj

    d = int(reverse)
    vec_spec = pl.BlockSpec((None, 1, N_STATE), lambda j: (d, 0, 0), pipeline_mode=pl.Buffered(1))
    mat_spec = pl.BlockSpec((None, SSM_WIDTH, SSM_STATE), lambda j: (d, 0, 0), pipeline_mode=pl.Buffered(1))
    return pl.pallas_call(
        functools.partial(_s5_kernel, t_chunk=t_chunk, reverse=reverse),
        grid=(nch + 1,),
        in_specs=[pl.BlockSpec((t_chunk, width), lambda j: (0, 0)),
                  pl.BlockSpec((t_chunk, width), lambda j: (chunk(j), 0)),
                  vec_spec, vec_spec, vec_spec, mat_spec, mat_spec, mat_spec, mat_spec],
        out_specs=pl.BlockSpec((t_chunk, width), lambda j: (chunk(j), 0)),
        out_shape=jax.ShapeDtypeStruct((u_x.shape[0], width), F32),
        scratch_shapes=[pltpu.VMEM((SSM_WIDTH, 2 * N_STATE), BF16),
                        pltpu.VMEM((2 * N_STATE, SSM_WIDTH), BF16),
                        pltpu.VMEM((2 * SUBLANES, N_STATE), F32),
                        pltpu.VMEM((2 * SUBLANES, N_STATE), F32),
                        pltpu.VMEM((rows, 2 * N_STATE), F32),
                        pltpu.VMEM((rows, 2 * N_STATE), BF16)],
        compiler_params=_params("arbitrary"),
        name="s5_rev" if reverse else "s5_fwd",
    )(u_c, u_x, lre, lim, ldt, bre, bim, cre, cim)


def _gelu_tanh(x):
    return 0.5 * x * (1.0 + jnp.tanh(math.sqrt(2.0 / math.pi) * (x + 0.044715 * (x * x * x))))


def _final_kernel(x_ref, attn_ref, u_ref, yf_ref, yb_ref, mod_ref, dsk_ref, wglu_ref, gmla_ref, gssm_ref,
                  wo_ref, g2_ref, wgu_ref, wd_ref, gf_ref, o_ref):
    x = x_ref[0]
    m = mod_ref[0]
    y = u_ref[...] * dsk_ref[...] + yf_ref[...] + yb_ref[...]
    z = _gelu_tanh(y).astype(BF16)
    ag = _dot(z, wglu_ref[...])
    ssm = ag[:, 0:SSM_WIDTH] * jax.nn.sigmoid(ag[:, SSM_WIDTH:2 * SSM_WIDTH])
    a_n = _rms(attn_ref[0], gmla_ref[...]).astype(BF16)
    s_n = _rms(ssm, gssm_ref[...]).astype(BF16)
    mix = _dot(a_n, wo_ref[0:MLA_WIDTH, :]) + _dot(s_n, wo_ref[MLA_WIDTH:MLA_WIDTH + SSM_WIDTH, :])
    x = x + m[5:6] * mix
    h = (_rms(x, g2_ref[...]) * (1.0 + m[7:8]) + m[6:7]).astype(BF16)
    x = x + 0.5 * m[8:9] * _swiglu(h, wgu_ref, wd_ref)
    o_ref[0] = _rms(x, gf_ref[...])


def _final(x, attn, u, y_fwd, y_bwd, mod, dsk, wglu, gmla, gssm, wo, g2, wgu, wd, gf, tm):
    nb, n, _ = x.shape
    ssm_spec = pl.BlockSpec((tm, SSM_WIDTH), lambda b, j: (j, b))
    return pl.pallas_call(
        _final_kernel,
        grid=(nb, n // tm),
        in_specs=[pl.BlockSpec((1, tm, D_MODEL), lambda b, j: (b, j, 0)),
                  pl.BlockSpec((1, tm, MLA_WIDTH), lambda b, j: (b, j, 0)),
                  ssm_spec, ssm_spec, ssm_spec,
                  pl.BlockSpec((1, N_MOD, D_MODEL), lambda b, j: (b, 0, 0)),
                  _const_spec((1, SSM_WIDTH)),
                  _const_spec((SSM_WIDTH, 2 * SSM_WIDTH)),
                  _const_spec((1, MLA_WIDTH)),
                  _const_spec((1, SSM_WIDTH)),
                  _const_spec((MLA_WIDTH + SSM_WIDTH, D_MODEL)),
                  _const_spec((1, D_MODEL)),
                  _const_spec((D_MODEL, 2 * D_FF)),
                  _const_spec((D_FF, D_MODEL)),
                  _const_spec((1, D_MODEL))],
        out_specs=pl.BlockSpec((1, tm, D_MODEL), lambda b, j: (b, j, 0)),
        out_shape=jax.ShapeDtypeStruct(x.shape, F32),
        compiler_params=_params("parallel", "parallel"),
        name="final",
    )(x, attn, u, y_fwd, y_bwd, mod, dsk, wglu, gmla, gssm, wo, g2, wgu, wd, gf)


def _rot_cols(w):
    wp = w.reshape(w.shape[0], -1, 2)
    return jnp.stack([-wp[..., 1], wp[..., 0]], axis=-1).reshape(w.shape)


def _rope_tables(n_tokens):
    rows = n_tokens // GRID_W
    row = np.repeat(np.arange(rows), GRID_W).astype(np.float32)
    col = np.tile(np.arange(GRID_W), rows).astype(np.float32)
    per_axis = QK_ROPE // 2
    inv_freq = (ROPE_BASE ** (-np.arange(0, per_axis, 2, dtype=np.float32) / per_axis)).astype(np.float32)
    ang = np.concatenate([row[:, None] * inv_freq, col[:, None] * inv_freq], axis=-1)
    cos = np.repeat(np.cos(ang), 2, axis=-1).astype(np.float32)
    sin = np.repeat(np.sin(ang), 2, axis=-1).astype(np.float32)
    return cos, sin


def _head_tables(cos, sin, nope_fill, scale):
    n = cos.shape[0]
    pad = HEAD_PAD - QK_NOPE - QK_ROPE
    c = np.concatenate([np.full((n, QK_NOPE), nope_fill, np.float32), cos, np.zeros((n, pad), np.float32)], axis=-1)
    s = np.concatenate([np.zeros((n, QK_NOPE), np.float32), sin, np.zeros((n, pad), np.float32)], axis=-1)
    return (c * np.float32(scale)).astype(np.float32), (s * np.float32(scale)).astype(np.float32)


def _pad_head_cols(w_nope, w_rope):
    k, h = w_nope.shape[:2]
    pad = HEAD_PAD - QK_NOPE - QK_ROPE
    return jnp.concatenate([w_nope, w_rope, jnp.zeros((k, h, pad), w_nope.dtype)], axis=-1).reshape(k, h * HEAD_PAD)


def kernel(x, c, ctx, c_ctx, w_mod, b_mod, g_ffn1, w_gu1, w_down1, g_mix, w_in, g_cq, w_uq, g_ckv, w_ukv,
           lam_re, lam_im, log_dt, b_re, b_im, c_re, c_im, d_skip, w_glu, g_mla_out, g_ssm_out, w_out,
           g_ffn2, w_gu2, w_down2, g_final):
    bsz, seq, _ = x.shape
    n_ctx = ctx.shape[1]
    assert w_mod.shape[0] == 1, "single-layer block"

    def layer(a):
        return a.reshape(a.shape[1:])

    rows = 2 * SUBLANES
    cvec = jnp.concatenate([c, c_ctx[None], jnp.zeros((rows - bsz - 1, D_MODEL), F32)], axis=0)
    mod = _adaln(cvec, layer(w_mod), b_mod).reshape(rows, N_MOD, D_MODEL)
    ctx_row = bsz

    wgu1 = layer(w_gu1).astype(BF16)
    wd1 = layer(w_down1).astype(BF16)
    x1 = _ffn(x, mod, g_ffn1, wgu1, wd1, None, FFN_TM)
    ctx1 = _ffn(ctx.reshape(1, bsz * n_ctx, D_MODEL), mod, g_ffn1, wgu1, wd1, ctx_row, CTX_FFN_TM)
    ctx1 = ctx1.reshape(bsz, n_ctx, D_MODEL)

    win = layer(w_in)
    o_kv = Q_RANK
    o_kr = o_kv + KV_RANK
    o_u = o_kr + QK_ROPE
    w_kr = win[:, o_kr:o_kr + QK_ROPE]
    zk = jnp.zeros((D_MODEL, QK_NOPE), F32)
    zp = jnp.zeros((D_MODEL, HEAD_PAD - QK_NOPE - QK_ROPE), F32)
    wa = jnp.concatenate([win[:, 0:o_kr], zk, w_kr, zp, zk, _rot_cols(w_kr), zp, win[:, o_u:]], axis=-1).astype(BF16)
    wuq = w_uq.reshape(Q_RANK, MLA_HEADS, QK_NOPE + QK_ROPE)
    wq_nope, wq_rope = wuq[..., :QK_NOPE], wuq[..., QK_NOPE:]
    wq_rot = _rot_cols(wq_rope.reshape(Q_RANK, -1)).reshape(wq_rope.shape)
    wq = jnp.concatenate([_pad_head_cols(wq_nope, wq_rope),
                          _pad_head_cols(jnp.zeros_like(wq_nope), wq_rot)], axis=-1).astype(BF16)
    wukv = w_ukv.reshape(KV_RANK, MLA_HEADS, QK_NOPE + V_HEAD)
    wk = _pad_head_cols(wukv[..., :QK_NOPE], jnp.zeros((KV_RANK, MLA_HEADS, QK_ROPE), F32))
    wv = wukv[..., QK_NOPE:].reshape(KV_RANK, MLA_WIDTH)
    wkv = jnp.concatenate([wk, wv], axis=-1).astype(BF16)

    cos, sin = _rope_tables(seq)
    cq_t, sq_t = _head_tables(cos, sin, 1.0, ATTN_SCALE * math.log2(math.e))
    ck_t, sk_t = _head_tables(cos, sin, 0.0, 1.0)
    one = np.ones((n_ctx, QK_ROPE), np.float32)
    ckc_t, skc_t = _head_tables(one, np.zeros_like(one), 0.0, 1.0)

    q, kx, vx, ux = _proj(x1, mod, g_mix, wa, g_cq, wq, g_ckv, wkv,
                          (cq_t, sq_t, ck_t, sk_t), None, PROJ_TM, True)
    kc, vc, uc = _proj(ctx1, mod, g_mix, wa, g_cq, wq, g_ckv, wkv,
                       (ckc_t, skc_t, ckc_t, skc_t), ctx_row, n_ctx, False)

    attn = _attention(q, kx, kc, vx, vc, ATTN_TQ)

    lre = lam_re.reshape(2, 1, N_STATE)
    lim = lam_im.reshape(2, 1, N_STATE)
    ldt = jnp.repeat(log_dt.reshape(2, SSM_GROUPS), SSM_STATE, axis=-1).reshape(2, 1, N_STATE)
    bre = jnp.swapaxes(b_re.reshape(2, SSM_GROUPS, SSM_STATE, SSM_CH), -1, -2).reshape(2, SSM_WIDTH, SSM_STATE)
    bim = jnp.swapaxes(b_im.reshape(2, SSM_GROUPS, SSM_STATE, SSM_CH), -1, -2).reshape(2, SSM_WIDTH, SSM_STATE)
    cre = c_re.reshape(2, SSM_WIDTH, SSM_STATE)
    cim = c_im.reshape(2, SSM_WIDTH, SSM_STATE)
    y_fwd, y_bwd = [_s5(uc, ux, lre, lim, ldt, bre, bim, cre, cim, n_ctx, rev) for rev in (False, True)]

    return _final(x1, attn, ux, y_fwd, y_bwd, mod, d_skip, layer(w_glu).astype(BF16), g_mla_out, g_ssm_out,
                  layer(w_out).astype(BF16), g_ffn2, layer(w_gu2).astype(BF16), layer(w_down2).astype(BF16),
                  g_final[None], FINAL_TM)
```

```python
import functools
import math

import jax
import jax.numpy as jnp
import numpy as np
from jax import lax
from jax.experimental import pallas as pl
from jax.experimental.pallas import tpu as pltpu

D_MODEL = 1024
GRID_W = 64
N_MOD = 9
D_FF = 2816
MLA_HEADS = 12
QK_NOPE = 64
QK_ROPE = 32
V_HEAD = 64
Q_RANK = 256
KV_RANK = 128
MLA_WIDTH = MLA_HEADS * V_HEAD
SSM_GROUPS = 16
SSM_CH = 16
SSM_WIDTH = SSM_GROUPS * SSM_CH
SSM_STATE = 64
N_STATE = SSM_GROUPS * SSM_STATE
ROPE_BASE = 10000.0
ATTN_SCALE = (QK_NOPE + QK_ROPE) ** -0.5
EPS = 1e-6

LANES = 128
SUBLANES = 8
MXU_DIM = 256
HEAD_PAD = 128
VMEM_LIMIT = 56 * 1024 * 1024

FFN_TM = 1024
CTX_FFN_TM = 512
PROJ_TM = 1024
ATTN_TQ = 1024
FINAL_TM = 512

F32 = jnp.float32
BF16 = jnp.bfloat16


def _rms(x, g):
    ms = jnp.mean(x * x, axis=-1, keepdims=True)
    return x * lax.rsqrt(ms + EPS) * g


def _dot(a, b):
    return jnp.dot(a, b, preferred_element_type=F32)


def _const_spec(shape):
    zeros = (0,) * len(shape)
    return pl.BlockSpec(shape, lambda *_: zeros, pipeline_mode=pl.Buffered(1))


def _params(*sem):
    return pltpu.CompilerParams(dimension_semantics=sem, vmem_limit_bytes=VMEM_LIMIT)


def _adaln_kernel(c_ref, w_ref, b_ref, o_ref):
    c = c_ref[...]
    s = (c * jax.nn.sigmoid(c)).astype(BF16)
    o_ref[...] = _dot(s, w_ref[...].astype(BF16)) + b_ref[...]


def _adaln(cvec, w_mod, b_mod):
    rows = cvec.shape[0]
    n = w_mod.shape[1]
    bn = D_MODEL
    return pl.pallas_call(
        _adaln_kernel,
        grid=(n // bn,),
        in_specs=[pl.BlockSpec((rows, D_MODEL), lambda i: (0, 0)),
                  pl.BlockSpec((D_MODEL, bn), lambda i: (0, i)),
                  pl.BlockSpec((1, bn), lambda i: (0, i))],
        out_specs=pl.BlockSpec((rows, bn), lambda i: (0, i)),
        out_shape=jax.ShapeDtypeStruct((rows, n), F32),
        compiler_params=_params("parallel"),
        name="adaln",
    )(cvec, w_mod, b_mod)


FF_TILES = D_FF // MXU_DIM
FF_SPLITS = (0, (FF_TILES + 1) // 2 * MXU_DIM, D_FF)


def _swiglu(h, wgu_ref, wd_ref):
    acc = None
    for lo, hi in zip(FF_SPLITS[:-1], FF_SPLITS[1:]):
        gate = _dot(h, wgu_ref[:, lo:hi])
        up = _dot(h, wgu_ref[:, D_FF + lo:D_FF + hi])
        act = (gate * jax.nn.sigmoid(gate) * up).astype(BF16)
        part = _dot(act, wd_ref[lo:hi, :])
        acc = part if acc is None else acc + part
    return acc


def _ffn_kernel(x_ref, mod_ref, g_ref, wgu_ref, wd_ref, o_ref):
    x = x_ref[0]
    m = mod_ref[0]
    h = (_rms(x, g_ref[...]) * (1.0 + m[1:2]) + m[0:1]).astype(BF16)
    o_ref[0] = x + 0.5 * m[2:3] * _swiglu(h, wgu_ref, wd_ref)


def _ffn(x, mod, g, wgu, wd, mod_row, tm):
    nb, n, _ = x.shape
    mod_map = (lambda b, j: (b, 0, 0)) if mod_row is None else (lambda b, j: (mod_row, 0, 0))
    return pl.pallas_call(
        _ffn_kernel,
        grid=(nb, n // tm),
        in_specs=[pl.BlockSpec((1, tm, D_MODEL), lambda b, j: (b, j, 0)),
                  pl.BlockSpec((1, N_MOD, D_MODEL), mod_map),
                  _const_spec((1, D_MODEL)),
                  _const_spec((D_MODEL, 2 * D_FF)),
                  _const_spec((D_FF, D_MODEL))],
        out_specs=pl.BlockSpec((1, tm, D_MODEL), lambda b, j: (b, j, 0)),
        out_shape=jax.ShapeDtypeStruct(x.shape, F32),
        compiler_params=_params("parallel", "parallel"),
        name="ffn",
    )(x, mod, g, wgu, wd)


W_A_COLS = Q_RANK + KV_RANK + 2 * HEAD_PAD + SSM_WIDTH
QK_WIDTH = MLA_HEADS * HEAD_PAD


def _proj_kernel(x_ref, mod_ref, g_ref, wa_ref, gcq_ref, wq_ref, gckv_ref, wkv_ref,
                 cq_ref, sq_ref, ck_ref, sk_ref, *out_refs, with_q):
    if with_q:
        q_ref, k_ref, v_ref, u_ref = out_refs
    else:
        k_ref, v_ref, u_ref = out_refs
    x = x_ref[0]
    m = mod_ref[0]
    h = (_rms(x, g_ref[...]) * (1.0 + m[4:5]) + m[3:4]).astype(BF16)
    p = _dot(h, wa_ref[...])
    o_kv = Q_RANK
    o_kr = o_kv + KV_RANK
    o_rot = o_kr + HEAD_PAD
    o_u = o_rot + HEAD_PAD
    u_ref[...] = p[:, o_u:o_u + SSM_WIDTH]
    kr = p[:, o_kr:o_kr + HEAD_PAD] * ck_ref[...] + p[:, o_rot:o_rot + HEAD_PAD] * sk_ref[...]
    ckv = _rms(p[:, o_kv:o_kv + KV_RANK], gckv_ref[...]).astype(BF16)
    kv = _dot(ckv, wkv_ref[...])
    for hd in range(MLA_HEADS):
        lo = hd * HEAD_PAD
        k_ref[0, :, lo:lo + HEAD_PAD] = (kv[:, lo:lo + HEAD_PAD] + kr).astype(BF16)
    v_ref[0] = kv[:, QK_WIDTH:QK_WIDTH + MLA_WIDTH].astype(BF16)
    if with_q:
        cq = _rms(p[:, 0:Q_RANK], gcq_ref[...]).astype(BF16)
        qa = _dot(cq, wq_ref[:, 0:QK_WIDTH])
        qb = _dot(cq, wq_ref[:, QK_WIDTH:2 * QK_WIDTH])
        cq_t = cq_ref[...]
        sq_t = sq_ref[...]
        for hd in range(MLA_HEADS):
            lo = hd * HEAD_PAD
            q_ref[0, :, lo:lo + HEAD_PAD] = (qa[:, lo:lo + HEAD_PAD] * cq_t
                                             + qb[:, lo:lo + HEAD_PAD] * sq_t).astype(BF16)


def _proj(x, mod, g_mix, wa, gcq, wq, gckv, wkv, tabs, mod_row, tm, with_q):
    nb, n, _ = x.shape
    mod_map = (lambda b, j: (b, 0, 0)) if mod_row is None else (lambda b, j: (mod_row, 0, 0))
    tab_spec = pl.BlockSpec((tm, HEAD_PAD), lambda b, j: (j, 0))
    out_shape = [jax.ShapeDtypeStruct((nb, n, QK_WIDTH), BF16),
                 jax.ShapeDtypeStruct((nb, n, MLA_WIDTH), BF16),
                 jax.ShapeDtypeStruct((n, nb * SSM_WIDTH), F32)]
    out_specs = [pl.BlockSpec((1, tm, QK_WIDTH), lambda b, j: (b, j, 0)),
                 pl.BlockSpec((1, tm, MLA_WIDTH), lambda b, j: (b, j, 0)),
                 pl.BlockSpec((tm, SSM_WIDTH), lambda b, j: (j, b))]
    if with_q:
        out_shape = [jax.ShapeDtypeStruct((nb, n, QK_WIDTH), BF16)] + out_shape
        out_specs = [pl.BlockSpec((1, tm, QK_WIDTH), lambda b, j: (b, j, 0))] + out_specs
    return pl.pallas_call(
        functools.partial(_proj_kernel, with_q=with_q),
        grid=(nb, n // tm),
        in_specs=[pl.BlockSpec((1, tm, D_MODEL), lambda b, j: (b, j, 0)),
                  pl.BlockSpec((1, N_MOD, D_MODEL), mod_map),
                  _const_spec((1, D_MODEL)),
                  _const_spec(wa.shape),
                  _const_spec((1, Q_RANK)),
                  _const_spec(wq.shape),
                  _const_spec((1, KV_RANK)),
                  _const_spec(wkv.shape),
                  tab_spec, tab_spec, tab_spec, tab_spec],
        out_specs=out_specs,
        out_shape=out_shape,
        compiler_params=_params("parallel", "parallel"),
        name="proj_q" if with_q else "proj_ctx",
    )(x, mod, g_mix, wa, gcq, wq, gckv, wkv, *tabs)


HEADS_PER_STEP = 4
KEY_SPLITS = (0, 1024, 2048, 3072, 4096)


def _attn_kernel(q_ref, kx_ref, kc_ref, vx_ref, vc_ref, o_ref):
    nt = (((1,), (1,)), ((), ()))
    tq = q_ref.shape[1]
    assert KEY_SPLITS[-1] == kx_ref.shape[1]
    chunks = [(kx_ref, vx_ref, lo, hi - lo) for lo, hi in zip(KEY_SPLITS[:-1], KEY_SPLITS[1:])]
    chunks.append((kc_ref, vc_ref, 0, kc_ref.shape[1]))
    outs = []
    for hh in range(HEADS_PER_STEP):
        lo = hh * HEAD_PAD
        q = q_ref[0, :, lo:lo + HEAD_PAD]
        m_b = acc = None
        for k_ref, v_ref, start, size in chunks:
            s = lax.dot_general(q, k_ref[0, start:start + size, lo:lo + HEAD_PAD], nt,
                                preferred_element_type=F32)
            cols = [s[:, i * LANES:(i + 1) * LANES] for i in range(size // LANES)]
            c_vec = functools.reduce(jnp.maximum, cols)
            c_max = jnp.broadcast_to(jnp.max(c_vec, axis=-1, keepdims=True), (tq, LANES))
            m_new = c_max if m_b is None else jnp.maximum(m_b, c_max)
            p = jnp.concatenate([jnp.exp2(col - m_new).astype(BF16) for col in cols], axis=-1)
            v_lo = (hh // 2) * 2 * V_HEAD
            v_ext = jnp.concatenate([v_ref[0, start:start + size, v_lo:v_lo + 2 * V_HEAD],
                                     jnp.ones((size, LANES), BF16)], axis=-1)
            pv = _dot(p, v_ext)
            if m_b is None:
                acc = pv
            else:
                alpha = jnp.exp2(m_b - m_new)
                acc = jnp.concatenate([alpha, alpha], axis=-1) * acc + pv
            m_b = m_new
        outs.append(acc[:, 0:LANES] / acc[:, LANES:2 * LANES])
    lane = lax.broadcasted_iota(jnp.int32, outs[0].shape, 1)
    for pr in range(HEADS_PER_STEP // 2):
        o_ref[0, :, pr * 2 * V_HEAD:(pr + 1) * 2 * V_HEAD] = jnp.where(lane < V_HEAD, outs[2 * pr], outs[2 * pr + 1])


def _attention(q, kx, kc, vx, vc, tq):
    nb, n, _ = q.shape
    nk = kx.shape[1]
    nc = kc.shape[1]
    pairs = MLA_HEADS // HEADS_PER_STEP
    qk_blk = HEADS_PER_STEP * HEAD_PAD
    v_blk = HEADS_PER_STEP * V_HEAD
    return pl.pallas_call(
        _attn_kernel,
        grid=(nb, pairs, n // tq),
        in_specs=[pl.BlockSpec((1, tq, qk_blk), lambda b, p, j: (b, j, p)),
                  pl.BlockSpec((1, nk, qk_blk), lambda b, p, j: (b, 0, p)),
                  pl.BlockSpec((1, nc, qk_blk), lambda b, p, j: (b, 0, p)),
                  pl.BlockSpec((1, nk, v_blk), lambda b, p, j: (b, 0, p)),
                  pl.BlockSpec((1, nc, v_blk), lambda b, p, j: (b, 0, p))],
        out_specs=pl.BlockSpec((1, tq, v_blk), lambda b, p, j: (b, j, p)),
        out_shape=jax.ShapeDtypeStruct((nb, n, MLA_WIDTH), F32),
        compiler_params=_params("parallel", "parallel", "parallel"),
        name="attention",
    )(q, kx, kc, vx, vc)


SCAN_BLOCK = 64
BD_RING = 4
S5_T = 512


def _s5_kernel(uc_ref, ux_ref, lre_ref, lim_ref, ldt_ref, bre_ref, bim_ref, cre_ref, cim_ref,
               y_ref, bb_ref, cm_ref, a_ref, h_ref, bd_ref, xb_ref, *, t_ctx, t_chunk, reverse):
    j = pl.program_id(0)

    @pl.when(j == 0)
    def _discretize():
        dt = jnp.exp(ldt_ref[...])
        lr = jnp.minimum(lre_ref[...], -1e-4)
        li = lim_ref[...]
        mag = jnp.exp(lr * dt)
        ar = mag * jnp.cos(li * dt)
        ai = mag * jnp.sin(li * dt)
        den = lr * lr + li * li
        fr = ((ar - 1.0) * lr + ai * li) / den
        fi = (ai * lr - (ar - 1.0) * li) / den
        same_group = (lax.broadcasted_iota(jnp.int32, (SSM_WIDTH, N_STATE), 0) // SSM_CH
                      == lax.broadcasted_iota(jnp.int32, (SSM_WIDTH, N_STATE), 1) // SSM_STATE)

        def block_diag(ref):
            return jnp.where(same_group, jnp.concatenate([ref[...]] * SSM_GROUPS, axis=-1), 0.0)

        bre = block_diag(bre_ref)
        bim = block_diag(bim_ref)
        bb_ref[:, 0:N_STATE] = (fr * bre - fi * bim).astype(BF16)
        bb_ref[:, N_STATE:2 * N_STATE] = (fr * bim + fi * bre).astype(BF16)
        cm_ref[0:N_STATE, :] = block_diag(cre_ref).T.astype(BF16)
        cm_ref[N_STATE:2 * N_STATE, :] = (-block_diag(cim_ref)).T.astype(BF16)
        a_ref[0:SUBLANES, :] = jnp.broadcast_to(ar, (SUBLANES, N_STATE))
        a_ref[SUBLANES:2 * SUBLANES, :] = jnp.broadcast_to(ai, (SUBLANES, N_STATE))
        h_ref[...] = jnp.zeros_like(h_ref)

    blk_rows = SCAN_BLOCK * SUBLANES
    steps = list(range(SCAN_BLOCK))
    if reverse:
        steps.reverse()

    def run(u_ref, t_len, with_readout):
        order = list(range(t_len // SCAN_BLOCK))
        if reverse:
            order.reverse()

        def slot(sb):
            return (sb % BD_RING) * blk_rows

        ar = a_ref[0:SUBLANES, :]
        ai = a_ref[SUBLANES:2 * SUBLANES, :]

        def drive(sb):
            u = u_ref[sb * SCAN_BLOCK:(sb + 1) * SCAN_BLOCK, :].reshape(blk_rows, SSM_WIDTH)
            bd_ref[slot(sb):slot(sb) + blk_rows, :] = _dot(u.astype(BF16), bb_ref[...])

        def scan(sb, hr, hi):
            for k in range(0, SCAN_BLOCK, 2):
                pair = {}
                for t in (steps[k], steps[k + 1]):
                    r = slot(sb) + t * SUBLANES
                    nr = ar * hr - ai * hi + bd_ref[r:r + SUBLANES, 0:N_STATE]
                    ni = ar * hi + ai * hr + bd_ref[r:r + SUBLANES, N_STATE:2 * N_STATE]
                    pair[t] = (nr, ni)
                    hr, hi = nr, ni
                if not with_readout:
                    continue
                t0 = min(pair)
                r = slot(sb) + t0 * SUBLANES
                xb_ref[r:r + 2 * SUBLANES, 0:N_STATE] = jnp.concatenate(
                    [pair[t0][0], pair[t0 + 1][0]], axis=0).astype(BF16)
                xb_ref[r:r + 2 * SUBLANES, N_STATE:2 * N_STATE] = jnp.concatenate(
                    [pair[t0][1], pair[t0 + 1][1]], axis=0).astype(BF16)
            return hr, hi

        def readout(sb):
            y = _dot(xb_ref[slot(sb):slot(sb) + blk_rows, :], cm_ref[...])
            y_ref[sb * SCAN_BLOCK:(sb + 1) * SCAN_BLOCK, :] = y.reshape(SCAN_BLOCK, SUBLANES * SSM_WIDTH)

        hr = h_ref[0:SUBLANES, :]
        hi = h_ref[SUBLANES:2 * SUBLANES, :]
        drive(order[0])
        for k, sb in enumerate(order):
            if k + 1 < len(order):
                drive(order[k + 1])
            hr, hi = scan(sb, hr, hi)
            if with_readout and k >= 1:
                readout(order[k - 1])
        if with_readout:
            readout(order[-1])
        h_ref[0:SUBLANES, :] = hr
        h_ref[SUBLANES:2 * SUBLANES, :] = hi

    @pl.when(j == 0)
    def _context():
        run(uc_ref, t_ctx, False)

    @pl.when(j > 0)
    def _latent():
        run(ux_ref, t_chunk, True)


def _s5(u_c, u_x, lre, lim, ldt, bre, bim, cre, cim, t_chunk, reverse):
    t_ctx = u_c.shape[0]
    rows = BD_RING * SCAN_BLOCK * SUBLANES
    width = SUBLANES * SSM_WIDTH
    assert u_c.shape[1] == width and u_x.shape[1] == width
    assert t_ctx % SCAN_BLOCK == 0 and t_chunk % SCAN_BLOCK == 0 and u_x.shape[0] % t_chunk == 0
    nch = u_x.shape[0] // t_chunk

    def chunk(j):
        jj = jnp.maximum(j - 1, 0)
        return nch - 1 - jj if reverse else jj

    d = int(reverse)
    vec_spec = pl.BlockSpec((None, 1, N_STATE), lambda j: (d, 0, 0), pipeline_mode=pl.Buffered(1))
    mat_spec = pl.BlockSpec((None, SSM_WIDTH, SSM_STATE), lambda j: (d, 0, 0), pipeline_mode=pl.Buffered(1))
    return pl.pallas_call(
        functools.partial(_s5_kernel, t_ctx=t_ctx, t_chunk=t_chunk, reverse=reverse),
        grid=(nch + 1,),
        in_specs=[pl.BlockSpec((t_ctx, width), lambda j: (0, 0), pipeline_mode=pl.Buffered(1)),
                  pl.BlockSpec((t_chunk, width), lambda j: (chunk(j), 0)),
                  vec_spec, vec_spec, vec_spec, mat_spec, mat_spec, mat_spec, mat_spec],
        out_specs=pl.BlockSpec((t_chunk, width), lambda j: (chunk(j), 0)),
        out_shape=jax.ShapeDtypeStruct((u_x.shape[0], width), F32),
        scratch_shapes=[pltpu.VMEM((SSM_WIDTH, 2 * N_STATE), BF16),
                        pltpu.VMEM((2 * N_STATE, SSM_WIDTH), BF16),
                        pltpu.VMEM((2 * SUBLANES, N_STATE), F32),
                        pltpu.VMEM((2 * SUBLANES, N_STATE), F32),
                        pltpu.VMEM((rows, 2 * N_STATE), F32),
                        pltpu.VMEM((rows, 2 * N_STATE), BF16)],
        compiler_params=_params("arbitrary"),
        name="s5_rev" if reverse else "s5_fwd",
    )(u_c, u_x, lre, lim, ldt, bre, bim, cre, cim)


def _gelu_tanh(x):
    return 0.5 * x * (1.0 + jnp.tanh(math.sqrt(2.0 / math.pi) * (x + 0.044715 * (x * x * x))))


def _final_kernel(x_ref, attn_ref, u_ref, yf_ref, yb_ref, mod_ref, dsk_ref, wglu_ref, gmla_ref, gssm_ref,
                  wo_ref, g2_ref, wgu_ref, wd_ref, gf_ref, o_ref):
    x = x_ref[0]
    m = mod_ref[0]
    y = u_ref[...] * dsk_ref[...] + yf_ref[...] + yb_ref[...]
    z = _gelu_tanh(y).astype(BF16)
    ag = _dot(z, wglu_ref[...])
    ssm = ag[:, 0:SSM_WIDTH] * jax.nn.sigmoid(ag[:, SSM_WIDTH:2 * SSM_WIDTH])
    a_n = _rms(attn_ref[0], gmla_ref[...]).astype(BF16)
    s_n = _rms(ssm, gssm_ref[...]).astype(BF16)
    mix = _dot(a_n, wo_ref[0:MLA_WIDTH, :]) + _dot(s_n, wo_ref[MLA_WIDTH:MLA_WIDTH + SSM_WIDTH, :])
    x = x + m[5:6] * mix
    h = (_rms(x, g2_ref[...]) * (1.0 + m[7:8]) + m[6:7]).astype(BF16)
    x = x + 0.5 * m[8:9] * _swiglu(h, wgu_ref, wd_ref)
    o_ref[0] = _rms(x, gf_ref[...])


def _final(x, attn, u, y_fwd, y_bwd, mod, dsk, wglu, gmla, gssm, wo, g2, wgu, wd, gf, tm):
    nb, n, _ = x.shape
    ssm_spec = pl.BlockSpec((tm, SSM_WIDTH), lambda b, j: (j, b))
    return pl.pallas_call(
        _final_kernel,
        grid=(nb, n // tm),
        in_specs=[pl.BlockSpec((1, tm, D_MODEL), lambda b, j: (b, j, 0)),
                  pl.BlockSpec((1, tm, MLA_WIDTH), lambda b, j: (b, j, 0)),
                  ssm_spec, ssm_spec, ssm_spec,
                  pl.BlockSpec((1, N_MOD, D_MODEL), lambda b, j: (b, 0, 0)),
                  _const_spec((1, SSM_WIDTH)),
                  _const_spec((SSM_WIDTH, 2 * SSM_WIDTH)),
                  _const_spec((1, MLA_WIDTH)),
                  _const_spec((1, SSM_WIDTH)),
                  _const_spec((MLA_WIDTH + SSM_WIDTH, D_MODEL)),
                  _const_spec((1, D_MODEL)),
                  _const_spec((D_MODEL, 2 * D_FF)),
                  _const_spec((D_FF, D_MODEL)),
                  _const_spec((1, D_MODEL))],
        out_specs=pl.BlockSpec((1, tm, D_MODEL), lambda b, j: (b, j, 0)),
        out_shape=jax.ShapeDtypeStruct(x.shape, F32),
        compiler_params=_params("parallel", "parallel"),
        name="final",
    )(x, attn, u, y_fwd, y_bwd, mod, dsk, wglu, gmla, gssm, wo, g2, wgu, wd, gf)


def _rot_cols(w):
    wp = w.reshape(w.shape[0], -1, 2)
    return jnp.stack([-wp[..., 1], wp[..., 0]], axis=-1).reshape(w.shape)


def _rope_tables(n_tokens):
    rows = n_tokens // GRID_W
    row = np.repeat(np.arange(rows), GRID_W).astype(np.float32)
    col = np.tile(np.arange(GRID_W), rows).astype(np.float32)
    per_axis = QK_ROPE // 2
    inv_freq = (ROPE_BASE ** (-np.arange(0, per_axis, 2, dtype=np.float32) / per_axis)).astype(np.float32)
    ang = np.concatenate([row[:, None] * inv_freq, col[:, None] * inv_freq], axis=-1)
    cos = np.repeat(np.cos(ang), 2, axis=-1).astype(np.float32)
    sin = np.repeat(np.sin(ang), 2, axis=-1).astype(np.float32)
    return cos, sin


def _head_tables(cos, sin, nope_fill, scale):
    n = cos.shape[0]
    pad = HEAD_PAD - QK_NOPE - QK_ROPE
    c = np.concatenate([np.full((n, QK_NOPE), nope_fill, np.float32), cos, np.zeros((n, pad), np.float32)], axis=-1)
    s = np.concatenate([np.zeros((n, QK_NOPE), np.float32), sin, np.zeros((n, pad), np.float32)], axis=-1)
    return (c * np.float32(scale)).astype(np.float32), (s * np.float32(scale)).astype(np.float32)


def _pad_head_cols(w_nope, w_rope):
    k, h = w_nope.shape[:2]
    pad = HEAD_PAD - QK_NOPE - QK_ROPE
    return jnp.concatenate([w_nope, w_rope, jnp.zeros((k, h, pad), w_nope.dtype)], axis=-1).reshape(k, h * HEAD_PAD)


def kernel(x, c, ctx, c_ctx, w_mod, b_mod, g_ffn1, w_gu1, w_down1, g_mix, w_in, g_cq, w_uq, g_ckv, w_ukv,
           lam_re, lam_im, log_dt, b_re, b_im, c_re, c_im, d_skip, w_glu, g_mla_out, g_ssm_out, w_out,
           g_ffn2, w_gu2, w_down2, g_final):
    bsz, seq, _ = x.shape
    n_ctx = ctx.shape[1]
    assert w_mod.shape[0] == 1, "single-layer block"

    def layer(a):
        return a.reshape(a.shape[1:])

    rows = 2 * SUBLANES
    cvec = jnp.concatenate([c, c_ctx[None], jnp.zeros((rows - bsz - 1, D_MODEL), F32)], axis=0)
    mod = _adaln(cvec, layer(w_mod), b_mod).reshape(rows, N_MOD, D_MODEL)
    ctx_row = bsz

    wgu1 = layer(w_gu1).astype(BF16)
    wd1 = layer(w_down1).astype(BF16)
    x1 = _ffn(x, mod, g_ffn1, wgu1, wd1, None, FFN_TM)
    ctx1 = _ffn(ctx.reshape(1, bsz * n_ctx, D_MODEL), mod, g_ffn1, wgu1, wd1, ctx_row, CTX_FFN_TM)
    ctx1 = ctx1.reshape(bsz, n_ctx, D_MODEL)

    win = layer(w_in)
    o_kv = Q_RANK
    o_kr = o_kv + KV_RANK
    o_u = o_kr + QK_ROPE
    w_kr = win[:, o_kr:o_kr + QK_ROPE]
    zk = jnp.zeros((D_MODEL, QK_NOPE), F32)
    zp = jnp.zeros((D_MODEL, HEAD_PAD - QK_NOPE - QK_ROPE), F32)
    wa = jnp.concatenate([win[:, 0:o_kr], zk, w_kr, zp, zk, _rot_cols(w_kr), zp, win[:, o_u:]], axis=-1).astype(BF16)
    wuq = w_uq.reshape(Q_RANK, MLA_HEADS, QK_NOPE + QK_ROPE)
    wq_nope, wq_rope = wuq[..., :QK_NOPE], wuq[..., QK_NOPE:]
    wq_rot = _rot_cols(wq_rope.reshape(Q_RANK, -1)).reshape(wq_rope.shape)
    wq = jnp.concatenate([_pad_head_cols(wq_nope, wq_rope),
                          _pad_head_cols(jnp.zeros_like(wq_nope), wq_rot)], axis=-1).astype(BF16)
    wukv = w_ukv.reshape(KV_RANK, MLA_HEADS, QK_NOPE + V_HEAD)
    wk = _pad_head_cols(wukv[..., :QK_NOPE], jnp.zeros((KV_RANK, MLA_HEADS, QK_ROPE), F32))
    wv = wukv[..., QK_NOPE:].reshape(KV_RANK, MLA_WIDTH)
    wkv = jnp.concatenate([wk, wv], axis=-1).astype(BF16)

    cos, sin = _rope_tables(seq)
    cq_t, sq_t = _head_tables(cos, sin, 1.0, ATTN_SCALE * math.log2(math.e))
    ck_t, sk_t = _head_tables(cos, sin, 0.0, 1.0)
    one = np.ones((n_ctx, QK_ROPE), np.float32)
    ckc_t, skc_t = _head_tables(one, np.zeros_like(one), 0.0, 1.0)

    q, kx, vx, ux = _proj(x1, mod, g_mix, wa, g_cq, wq, g_ckv, wkv,
                          (cq_t, sq_t, ck_t, sk_t), None, PROJ_TM, True)
    kc, vc, uc = _proj(ctx1, mod, g_mix, wa, g_cq, wq, g_ckv, wkv,
                       (ckc_t, skc_t, ckc_t, skc_t), ctx_row, n_ctx, False)

    attn = _attention(q, kx, kc, vx, vc, ATTN_TQ)

    lre = lam_re.reshape(2, 1, N_STATE)
    lim = lam_im.reshape(2, 1, N_STATE)
    ldt = jnp.repeat(log_dt.reshape(2, SSM_GROUPS), SSM_STATE, axis=-1).reshape(2, 1, N_STATE)
    bre = jnp.swapaxes(b_re.reshape(2, SSM_GROUPS, SSM_STATE, SSM_CH), -1, -2).reshape(2, SSM_WIDTH, SSM_STATE)
    bim = jnp.swapaxes(b_im.reshape(2, SSM_GROUPS, SSM_STATE, SSM_CH), -1, -2).reshape(2, SSM_WIDTH, SSM_STATE)
    cre = c_re.reshape(2, SSM_WIDTH, SSM_STATE)
    cim = c_im.reshape(2, SSM_WIDTH, SSM_STATE)
    y_fwd, y_bwd = [_s5(uc, ux, lre, lim, ldt, bre, bim, cre, cim, S5_T, rev) for rev in (False, True)]

    return _final(x1, attn, ux, y_fwd, y_bwd, mod, d_skip, layer(w_glu).astype(BF16), g_mla_out, g_ssm_out,
                  layer(w_out).astype(BF16), g_ffn2, layer(w_gu2).astype(BF16), layer(w_down2).astype(BF16),
                  g_final[None], FINAL_TM)
```

```python
import functools
import math

import jax
import jax.numpy as jnp
import numpy as np
from jax import lax
from jax.experimental import pallas as pl
from jax.experimental.pallas import tpu as pltpu

D_MODEL = 1024
GRID_W = 64
N_MOD = 9
D_FF = 2816
MLA_HEADS = 12
QK_NOPE = 64
QK_ROPE = 32
V_HEAD = 64
Q_RANK = 256
KV_RANK = 128
MLA_WIDTH = MLA_HEADS * V_HEAD
SSM_GROUPS = 16
SSM_CH = 16
SSM_WIDTH = SSM_GROUPS * SSM_CH
SSM_STATE = 64
N_STATE = SSM_GROUPS * SSM_STATE
ROPE_BASE = 10000.0
ATTN_SCALE = (QK_NOPE + QK_ROPE) ** -0.5
EPS = 1e-6

LANES = 128
SUBLANES = 8
MXU_DIM = 256
HEAD_PAD = 128
VMEM_LIMIT = 56 * 1024 * 1024

FFN_TM = 1024
CTX_FFN_TM = 512
PROJ_TM = 1024
ATTN_TQ = 1024
FINAL_TM = 512

F32 = jnp.float32
BF16 = jnp.bfloat16


def _rms(x, g):
    ms = jnp.mean(x * x, axis=-1, keepdims=True)
    return x * lax.rsqrt(ms + EPS) * g


def _dot(a, b):
    return jnp.dot(a, b, preferred_element_type=F32)


def _const_spec(shape):
    zeros = (0,) * len(shape)
    return pl.BlockSpec(shape, lambda *_: zeros, pipeline_mode=pl.Buffered(1))


def _params(*sem):
    return pltpu.CompilerParams(dimension_semantics=sem, vmem_limit_bytes=VMEM_LIMIT)


def _adaln_kernel(c_ref, w_ref, b_ref, o_ref):
    c = c_ref[...]
    s = (c * jax.nn.sigmoid(c)).astype(BF16)
    o_ref[...] = _dot(s, w_ref[...].astype(BF16)) + b_ref[...]


def _adaln(cvec, w_mod, b_mod):
    rows = cvec.shape[0]
    n = w_mod.shape[1]
    bn = D_MODEL
    return pl.pallas_call(
        _adaln_kernel,
        grid=(n // bn,),
        in_specs=[pl.BlockSpec((rows, D_MODEL), lambda i: (0, 0)),
                  pl.BlockSpec((D_MODEL, bn), lambda i: (0, i)),
                  pl.BlockSpec((1, bn), lambda i: (0, i))],
        out_specs=pl.BlockSpec((rows, bn), lambda i: (0, i)),
        out_shape=jax.ShapeDtypeStruct((rows, n), F32),
        compiler_params=_params("parallel"),
        name="adaln",
    )(cvec, w_mod, b_mod)


FF_TILES = D_FF // MXU_DIM
FF_SPLITS = (0, (FF_TILES + 1) // 2 * MXU_DIM, D_FF)


def _swiglu(h, wgu_ref, wd_ref):
    acc = None
    for lo, hi in zip(FF_SPLITS[:-1], FF_SPLITS[1:]):
        gate = _dot(h, wgu_ref[:, lo:hi])
        up = _dot(h, wgu_ref[:, D_FF + lo:D_FF + hi])
        act = (gate * jax.nn.sigmoid(gate) * up).astype(BF16)
        part = _dot(act, wd_ref[lo:hi, :])
        acc = part if acc is None else acc + part
    return acc


def _ffn_kernel(x_ref, mod_ref, g_ref, wgu_ref, wd_ref, o_ref):
    x = x_ref[0]
    m = mod_ref[0]
    h = (_rms(x, g_ref[...]) * (1.0 + m[1:2]) + m[0:1]).astype(BF16)
    o_ref[0] = x + 0.5 * m[2:3] * _swiglu(h, wgu_ref, wd_ref)


def _ffn(x, mod, g, wgu, wd, mod_row, tm):
    nb, n, _ = x.shape
    mod_map = (lambda b, j: (b, 0, 0)) if mod_row is None else (lambda b, j: (mod_row, 0, 0))
    return pl.pallas_call(
        _ffn_kernel,
        grid=(nb, n // tm),
        in_specs=[pl.BlockSpec((1, tm, D_MODEL), lambda b, j: (b, j, 0)),
                  pl.BlockSpec((1, N_MOD, D_MODEL), mod_map),
                  _const_spec((1, D_MODEL)),
                  _const_spec((D_MODEL, 2 * D_FF)),
                  _const_spec((D_FF, D_MODEL))],
        out_specs=pl.BlockSpec((1, tm, D_MODEL), lambda b, j: (b, j, 0)),
        out_shape=jax.ShapeDtypeStruct(x.shape, F32),
        compiler_params=_params("parallel", "parallel"),
        name="ffn",
    )(x, mod, g, wgu, wd)


W_A_COLS = Q_RANK + KV_RANK + 2 * HEAD_PAD + SSM_WIDTH
QK_WIDTH = MLA_HEADS * HEAD_PAD


def _proj_kernel(x_ref, mod_ref, g_ref, wa_ref, gcq_ref, wq_ref, gckv_ref, wkv_ref,
                 cq_ref, sq_ref, ck_ref, sk_ref, *out_refs, with_q):
    if with_q:
        q_ref, k_ref, v_ref, u_ref = out_refs
    else:
        k_ref, v_ref, u_ref = out_refs
    x = x_ref[0]
    m = mod_ref[0]
    h = (_rms(x, g_ref[...]) * (1.0 + m[4:5]) + m[3:4]).astype(BF16)
    p = _dot(h, wa_ref[...])
    o_kv = Q_RANK
    o_kr = o_kv + KV_RANK
    o_rot = o_kr + HEAD_PAD
    o_u = o_rot + HEAD_PAD
    u_ref[...] = p[:, o_u:o_u + SSM_WIDTH]
    kr = p[:, o_kr:o_kr + HEAD_PAD] * ck_ref[...] + p[:, o_rot:o_rot + HEAD_PAD] * sk_ref[...]
    ckv = _rms(p[:, o_kv:o_kv + KV_RANK], gckv_ref[...]).astype(BF16)
    kv = _dot(ckv, wkv_ref[...])
    for hd in range(MLA_HEADS):
        lo = hd * HEAD_PAD
        k_ref[0, :, lo:lo + HEAD_PAD] = (kv[:, lo:lo + HEAD_PAD] + kr).astype(BF16)
    v_ref[0] = kv[:, QK_WIDTH:QK_WIDTH + MLA_WIDTH].astype(BF16)
    if with_q:
        cq = _rms(p[:, 0:Q_RANK], gcq_ref[...]).astype(BF16)
        qa = _dot(cq, wq_ref[:, 0:QK_WIDTH])
        qb = _dot(cq, wq_ref[:, QK_WIDTH:2 * QK_WIDTH])
        cq_t = cq_ref[...]
        sq_t = sq_ref[...]
        for hd in range(MLA_HEADS):
            lo = hd * HEAD_PAD
            q_ref[0, :, lo:lo + HEAD_PAD] = (qa[:, lo:lo + HEAD_PAD] * cq_t
                                             + qb[:, lo:lo + HEAD_PAD] * sq_t).astype(BF16)


def _proj(x, mod, g_mix, wa, gcq, wq, gckv, wkv, tabs, mod_row, tm, with_q):
    nb, n, _ = x.shape
    mod_map = (lambda b, j: (b, 0, 0)) if mod_row is None else (lambda b, j: (mod_row, 0, 0))
    tab_spec = pl.BlockSpec((tm, HEAD_PAD), lambda b, j: (j, 0))
    out_shape = [jax.ShapeDtypeStruct((nb, n, QK_WIDTH), BF16),
                 jax.ShapeDtypeStruct((nb, n, MLA_WIDTH), BF16),
                 jax.ShapeDtypeStruct((n, nb * SSM_WIDTH), F32)]
    out_specs = [pl.BlockSpec((1, tm, QK_WIDTH), lambda b, j: (b, j, 0)),
                 pl.BlockSpec((1, tm, MLA_WIDTH), lambda b, j: (b, j, 0)),
                 pl.BlockSpec((tm, SSM_WIDTH), lambda b, j: (j, b))]
    if with_q:
        out_shape = [jax.ShapeDtypeStruct((nb, n, QK_WIDTH), BF16)] + out_shape
        out_specs = [pl.BlockSpec((1, tm, QK_WIDTH), lambda b, j: (b, j, 0))] + out_specs
    return pl.pallas_call(
        functools.partial(_proj_kernel, with_q=with_q),
        grid=(nb, n // tm),
        in_specs=[pl.BlockSpec((1, tm, D_MODEL), lambda b, j: (b, j, 0)),
                  pl.BlockSpec((1, N_MOD, D_MODEL), mod_map),
                  _const_spec((1, D_MODEL)),
                  _const_spec(wa.shape),
                  _const_spec((1, Q_RANK)),
                  _const_spec(wq.shape),
                  _const_spec((1, KV_RANK)),
                  _const_spec(wkv.shape),
                  tab_spec, tab_spec, tab_spec, tab_spec],
        out_specs=out_specs,
        out_shape=out_shape,
        compiler_params=_params("parallel", "parallel"),
        name="proj_q" if with_q else "proj_ctx",
    )(x, mod, g_mix, wa, gcq, wq, gckv, wkv, *tabs)


HEADS_PER_STEP = 4
KEY_SPLITS = tuple(range(0, 4097, 512))


def _attn_kernel(q_ref, kx_ref, kc_ref, vx_ref, vc_ref, o_ref):
    nt = (((1,), (1,)), ((), ()))
    tq = q_ref.shape[1]
    assert KEY_SPLITS[-1] == kx_ref.shape[1]
    chunks = [(kx_ref, vx_ref, lo, hi - lo) for lo, hi in zip(KEY_SPLITS[:-1], KEY_SPLITS[1:])]
    chunks.append((kc_ref, vc_ref, 0, kc_ref.shape[1]))
    outs = []
    for hh in range(HEADS_PER_STEP):
        lo = hh * HEAD_PAD
        q = q_ref[0, :, lo:lo + HEAD_PAD]
        m_b = acc = None
        for k_ref, v_ref, start, size in chunks:
            s = lax.dot_general(q, k_ref[0, start:start + size, lo:lo + HEAD_PAD], nt,
                                preferred_element_type=F32)
            cols = [s[:, i * LANES:(i + 1) * LANES] for i in range(size // LANES)]
            c_vec = functools.reduce(jnp.maximum, cols)
            c_max = jnp.broadcast_to(jnp.max(c_vec, axis=-1, keepdims=True), (tq, LANES))
            m_new = c_max if m_b is None else jnp.maximum(m_b, c_max)
            p = jnp.concatenate([jnp.exp2(col - m_new).astype(BF16) for col in cols], axis=-1)
            v_lo = (hh // 2) * 2 * V_HEAD
            v_ext = jnp.concatenate([v_ref[0, start:start + size, v_lo:v_lo + 2 * V_HEAD],
                                     jnp.ones((size, LANES), BF16)], axis=-1)
            pv = _dot(p, v_ext)
            if m_b is None:
                acc = pv
            else:
                alpha = jnp.exp2(m_b - m_new)
                acc = jnp.concatenate([alpha, alpha], axis=-1) * acc + pv
            m_b = m_new
        outs.append(acc[:, 0:LANES] / acc[:, LANES:2 * LANES])
    lane = lax.broadcasted_iota(jnp.int32, outs[0].shape, 1)
    for pr in range(HEADS_PER_STEP // 2):
        o_ref[0, :, pr * 2 * V_HEAD:(pr + 1) * 2 * V_HEAD] = jnp.where(lane < V_HEAD, outs[2 * pr], outs[2 * pr + 1])


def _attention(q, kx, kc, vx, vc, tq):
    nb, n, _ = q.shape
    nk = kx.shape[1]
    nc = kc.shape[1]
    pairs = MLA_HEADS // HEADS_PER_STEP
    qk_blk = HEADS_PER_STEP * HEAD_PAD
    v_blk = HEADS_PER_STEP * V_HEAD
    return pl.pallas_call(
        _attn_kernel,
        grid=(nb, pairs, n // tq),
        in_specs=[pl.BlockSpec((1, tq, qk_blk), lambda b, p, j: (b, j, p)),
                  pl.BlockSpec((1, nk, qk_blk), lambda b, p, j: (b, 0, p)),
                  pl.BlockSpec((1, nc, qk_blk), lambda b, p, j: (b, 0, p)),
                  pl.BlockSpec((1, nk, v_blk), lambda b, p, j: (b, 0, p)),
                  pl.BlockSpec((1, nc, v_blk), lambda b, p, j: (b, 0, p))],
        out_specs=pl.BlockSpec((1, tq, v_blk), lambda b, p, j: (b, j, p)),
        out_shape=jax.ShapeDtypeStruct((nb, n, MLA_WIDTH), F32),
        compiler_params=_params("parallel", "parallel", "parallel"),
        name="attention",
    )(q, kx, kc, vx, vc)


SCAN_BLOCK = 32


def _s5_kernel(uc_ref, ux_ref, lre_ref, lim_ref, ldt_ref, bre_ref, bim_ref, cre_ref, cim_ref,
               y_ref, bb_ref, cm_ref, a_ref, h_ref, bd_ref, xb_ref, *, t_chunk, reverse):
    j = pl.program_id(0)

    @pl.when(j == 0)
    def _discretize():
        dt = jnp.exp(ldt_ref[...])
        lr = jnp.minimum(lre_ref[...], -1e-4)
        li = lim_ref[...]
        mag = jnp.exp(lr * dt)
        ar = mag * jnp.cos(li * dt)
        ai = mag * jnp.sin(li * dt)
        den = lr * lr + li * li
        fr = ((ar - 1.0) * lr + ai * li) / den
        fi = (ai * lr - (ar - 1.0) * li) / den
        same_group = (lax.broadcasted_iota(jnp.int32, (SSM_WIDTH, N_STATE), 0) // SSM_CH
                      == lax.broadcasted_iota(jnp.int32, (SSM_WIDTH, N_STATE), 1) // SSM_STATE)

        def block_diag(ref):
            return jnp.where(same_group, jnp.concatenate([ref[...]] * SSM_GROUPS, axis=-1), 0.0)

        bre = block_diag(bre_ref)
        bim = block_diag(bim_ref)
        bb_ref[:, 0:N_STATE] = (fr * bre - fi * bim).astype(BF16)
        bb_ref[:, N_STATE:2 * N_STATE] = (fr * bim + fi * bre).astype(BF16)
        cm_ref[0:N_STATE, :] = block_diag(cre_ref).T.astype(BF16)
        cm_ref[N_STATE:2 * N_STATE, :] = (-block_diag(cim_ref)).T.astype(BF16)
        a_ref[0:SUBLANES, :] = jnp.broadcast_to(ar, (SUBLANES, N_STATE))
        a_ref[SUBLANES:2 * SUBLANES, :] = jnp.broadcast_to(ai, (SUBLANES, N_STATE))
        h_ref[...] = jnp.zeros_like(h_ref)

    blk_rows = SCAN_BLOCK * SUBLANES
    order = list(range(t_chunk // SCAN_BLOCK))
    steps = list(range(SCAN_BLOCK))
    if reverse:
        order.reverse()
        steps.reverse()

    def run(u_ref, with_readout):
        ar = a_ref[0:SUBLANES, :]
        ai = a_ref[SUBLANES:2 * SUBLANES, :]

        def drive(sb):
            u = u_ref[sb * SCAN_BLOCK:(sb + 1) * SCAN_BLOCK, :].reshape(blk_rows, SSM_WIDTH)
            bd_ref[sb * blk_rows:(sb + 1) * blk_rows, :] = _dot(u.astype(BF16), bb_ref[...])

        def scan(sb, hr, hi):
            for k in range(0, SCAN_BLOCK, 2):
                pair = {}
                for t in (steps[k], steps[k + 1]):
                    r = sb * blk_rows + t * SUBLANES
                    nr = ar * hr - ai * hi + bd_ref[r:r + SUBLANES, 0:N_STATE]
                    ni = ar * hi + ai * hr + bd_ref[r:r + SUBLANES, N_STATE:2 * N_STATE]
                    pair[t] = (nr, ni)
                    hr, hi = nr, ni
                if not with_readout:
                    continue
                t0 = min(pair)
                r = sb * blk_rows + t0 * SUBLANES
                xb_ref[r:r + 2 * SUBLANES, 0:N_STATE] = jnp.concatenate(
                    [pair[t0][0], pair[t0 + 1][0]], axis=0).astype(BF16)
                xb_ref[r:r + 2 * SUBLANES, N_STATE:2 * N_STATE] = jnp.concatenate(
                    [pair[t0][1], pair[t0 + 1][1]], axis=0).astype(BF16)
            return hr, hi

        def readout(sb):
            y = _dot(xb_ref[sb * blk_rows:(sb + 1) * blk_rows, :], cm_ref[...])
            y_ref[sb * SCAN_BLOCK:(sb + 1) * SCAN_BLOCK, :] = y.reshape(SCAN_BLOCK, SUBLANES * SSM_WIDTH)

        hr = h_ref[0:SUBLANES, :]
        hi = h_ref[SUBLANES:2 * SUBLANES, :]
        drive(order[0])
        for k, sb in enumerate(order):
            if k + 1 < len(order):
                drive(order[k + 1])
            hr, hi = scan(sb, hr, hi)
            if with_readout and k >= 1:
                readout(order[k - 1])
        if with_readout:
            readout(order[-1])
        h_ref[0:SUBLANES, :] = hr
        h_ref[SUBLANES:2 * SUBLANES, :] = hi

    @pl.when(j == 0)
    def _context():
        run(uc_ref, False)

    @pl.when(j > 0)
    def _latent():
        run(ux_ref, True)


def _s5(u_c, u_x, lre, lim, ldt, bre, bim, cre, cim, t_chunk, reverse):
    rows = t_chunk * SUBLANES
    width = SUBLANES * SSM_WIDTH
    assert u_c.shape == (t_chunk, width) and u_x.shape[1] == width
    nch = u_x.shape[0] // t_chunk

    def chunk(j):
        jj = jnp.maximum(j - 1, 0)
        return nch - 1 - jj if reverse else jj

    d = int(reverse)
    vec_spec = pl.BlockSpec((None, 1, N_STATE), lambda j: (d, 0, 0), pipeline_mode=pl.Buffered(1))
    mat_spec = pl.BlockSpec((None, SSM_WIDTH, SSM_STATE), lambda j: (d, 0, 0), pipeline_mode=pl.Buffered(1))
    return pl.pallas_call(
        functools.partial(_s5_kernel, t_chunk=t_chunk, reverse=reverse),
        grid=(nch + 1,),
        in_specs=[pl.BlockSpec((t_chunk, width), lambda j: (0, 0)),
                  pl.BlockSpec((t_chunk, width), lambda j: (chunk(j), 0)),
                  vec_spec, vec_spec, vec_spec, mat_spec, mat_spec, mat_spec, mat_spec],
        out_specs=pl.BlockSpec((t_chunk, width), lambda j: (chunk(j), 0)),
        out_shape=jax.ShapeDtypeStruct((u_x.shape[0], width), F32),
        scratch_shapes=[pltpu.VMEM((SSM_WIDTH, 2 * N_STATE), BF16),
                        pltpu.VMEM((2 * N_STATE, SSM_WIDTH), BF16),
                        pltpu.VMEM((2 * SUBLANES, N_STATE), F32),
                        pltpu.VMEM((2 * SUBLANES, N_STATE), F32),
                        pltpu.VMEM((rows, 2 * N_STATE), F32),
                        pltpu.VMEM((rows, 2 * N_STATE), BF16)],
        compiler_params=_params("arbitrary"),
        name="s5_rev" if reverse else "s5_fwd",
    )(u_c, u_x, lre, lim, ldt, bre, bim, cre, cim)


def _gelu_tanh(x):
    return 0.5 * x * (1.0 + jnp.tanh(math.sqrt(2.0 / math.pi) * (x + 0.044715 * (x * x * x))))


def _final_kernel(x_ref, attn_ref, u_ref, yf_ref, yb_ref, mod_ref, dsk_ref, wglu_ref, gmla_ref, gssm_ref,
                  wo_ref, g2_ref, wgu_ref, wd_ref, gf_ref, o_ref):
    x = x_ref[0]
    m = mod_ref[0]
    y = u_ref[...] * dsk_ref[...] + yf_ref[...] + yb_ref[...]
    z = _gelu_tanh(y).astype(BF16)
    ag = _dot(z, wglu_ref[...])
    ssm = ag[:, 0:SSM_WIDTH] * jax.nn.sigmoid(ag[:, SSM_WIDTH:2 * SSM_WIDTH])
    a_n = _rms(attn_ref[0], gmla_ref[...]).astype(BF16)
    s_n = _rms(ssm, gssm_ref[...]).astype(BF16)
    mix = _dot(a_n, wo_ref[0:MLA_WIDTH, :]) + _dot(s_n, wo_ref[MLA_WIDTH:MLA_WIDTH + SSM_WIDTH, :])
    x = x + m[5:6] * mix
    h = (_rms(x, g2_ref[...]) * (1.0 + m[7:8]) + m[6:7]).astype(BF16)
    x = x + 0.5 * m[8:9] * _swiglu(h, wgu_ref, wd_ref)
    o_ref[0] = _rms(x, gf_ref[...])


def _final(x, attn, u, y_fwd, y_bwd, mod, dsk, wglu, gmla, gssm, wo, g2, wgu, wd, gf, tm):
    nb, n, _ = x.shape
    ssm_spec = pl.BlockSpec((tm, SSM_WIDTH), lambda b, j: (j, b))
    return pl.pallas_call(
        _final_kernel,
        grid=(nb, n // tm),
        in_specs=[pl.BlockSpec((1, tm, D_MODEL), lambda b, j: (b, j, 0)),
                  pl.BlockSpec((1, tm, MLA_WIDTH), lambda b, j: (b, j, 0)),
                  ssm_spec, ssm_spec, ssm_spec,
                  pl.BlockSpec((1, N_MOD, D_MODEL), lambda b, j: (b, 0, 0)),
                  _const_spec((1, SSM_WIDTH)),
                  _const_spec((SSM_WIDTH, 2 * SSM_WIDTH)),
                  _const_spec((1, MLA_WIDTH)),
                  _const_spec((1, SSM_WIDTH)),
                  _const_spec((MLA_WIDTH + SSM_WIDTH, D_MODEL)),
                  _const_spec((1, D_MODEL)),
                  _const_spec((D_MODEL, 2 * D_FF)),
                  _const_spec((D_FF, D_MODEL)),
                  _const_spec((1, D_MODEL))],
        out_specs=pl.BlockSpec((1, tm, D_MODEL), lambda b, j: (b, j, 0)),
        out_shape=jax.ShapeDtypeStruct(x.shape, F32),
        compiler_params=_params("parallel", "parallel"),
        name="final",
    )(x, attn, u, y_fwd, y_bwd, mod, dsk, wglu, gmla, gssm, wo, g2, wgu, wd, gf)


def _rot_cols(w):
    wp = w.reshape(w.shape[0], -1, 2)
    return jnp.stack([-wp[..., 1], wp[..., 0]], axis=-1).reshape(w.shape)


def _rope_tables(n_tokens):
    rows = n_tokens // GRID_W
    row = np.repeat(np.arange(rows), GRID_W).astype(np.float32)
    col = np.tile(np.arange(GRID_W), rows).astype(np.float32)
    per_axis = QK_ROPE // 2
    inv_freq = (ROPE_BASE ** (-np.arange(0, per_axis, 2, dtype=np.float32) / per_axis)).astype(np.float32)
    ang = np.concatenate([row[:, None] * inv_freq, col[:, None] * inv_freq], axis=-1)
    cos = np.repeat(np.cos(ang), 2, axis=-1).astype(np.float32)
    sin = np.repeat(np.sin(ang), 2, axis=-1).astype(np.float32)
    return cos, sin


def _head_tables(cos, sin, nope_fill, scale):
    n = cos.shape[0]
    pad = HEAD_PAD - QK_NOPE - QK_ROPE
    c = np.concatenate([np.full((n, QK_NOPE), nope_fill, np.float32), cos, np.zeros((n, pad), np.float32)], axis=-1)
    s = np.concatenate([np.zeros((n, QK_NOPE), np.float32), sin, np.zeros((n, pad), np.float32)], axis=-1)
    return (c * np.float32(scale)).astype(np.float32), (s * np.float32(scale)).astype(np.float32)


def _pad_head_cols(w_nope, w_rope):
    k, h = w_nope.shape[:2]
    pad = HEAD_PAD - QK_NOPE - QK_ROPE
    return jnp.concatenate([w_nope, w_rope, jnp.zeros((k, h, pad), w_nope.dtype)], axis=-1).reshape(k, h * HEAD_PAD)


def kernel(x, c, ctx, c_ctx, w_mod, b_mod, g_ffn1, w_gu1, w_down1, g_mix, w_in, g_cq, w_uq, g_ckv, w_ukv,
           lam_re, lam_im, log_dt, b_re, b_im, c_re, c_im, d_skip, w_glu, g_mla_out, g_ssm_out, w_out,
           g_ffn2, w_gu2, w_down2, g_final):
    bsz, seq, _ = x.shape
    n_ctx = ctx.shape[1]
    assert w_mod.shape[0] == 1, "single-layer block"

    def layer(a):
        return a.reshape(a.shape[1:])

    rows = 2 * SUBLANES
    cvec = jnp.concatenate([c, c_ctx[None], jnp.zeros((rows - bsz - 1, D_MODEL), F32)], axis=0)
    mod = _adaln(cvec, layer(w_mod), b_mod).reshape(rows, N_MOD, D_MODEL)
    ctx_row = bsz

    wgu1 = layer(w_gu1).astype(BF16)
    wd1 = layer(w_down1).astype(BF16)
    x1 = _ffn(x, mod, g_ffn1, wgu1, wd1, None, FFN_TM)
    ctx1 = _ffn(ctx.reshape(1, bsz * n_ctx, D_MODEL), mod, g_ffn1, wgu1, wd1, ctx_row, CTX_FFN_TM)
    ctx1 = ctx1.reshape(bsz, n_ctx, D_MODEL)

    win = layer(w_in)
    o_kv = Q_RANK
    o_kr = o_kv + KV_RANK
    o_u = o_kr + QK_ROPE
    w_kr = win[:, o_kr:o_kr + QK_ROPE]
    zk = jnp.zeros((D_MODEL, QK_NOPE), F32)
    zp = jnp.zeros((D_MODEL, HEAD_PAD - QK_NOPE - QK_ROPE), F32)
    wa = jnp.concatenate([win[:, 0:o_kr], zk, w_kr, zp, zk, _rot_cols(w_kr), zp, win[:, o_u:]], axis=-1).astype(BF16)
    wuq = w_uq.reshape(Q_RANK, MLA_HEADS, QK_NOPE + QK_ROPE)
    wq_nope, wq_rope = wuq[..., :QK_NOPE], wuq[..., QK_NOPE:]
    wq_rot = _rot_cols(wq_rope.reshape(Q_RANK, -1)).reshape(wq_rope.shape)
    wq = jnp.concatenate([_pad_head_cols(wq_nope, wq_rope),
                          _pad_head_cols(jnp.zeros_like(wq_nope), wq_rot)], axis=-1).astype(BF16)
    wukv = w_ukv.reshape(KV_RANK, MLA_HEADS, QK_NOPE + V_HEAD)
    wk = _pad_head_cols(wukv[..., :QK_NOPE], jnp.zeros((KV_RANK, MLA_HEADS, QK_ROPE), F32))
    wv = wukv[..., QK_NOPE:].reshape(KV_RANK, MLA_WIDTH)
    wkv = jnp.concatenate([wk, wv], axis=-1).astype(BF16)

    cos, sin = _rope_tables(seq)
    cq_t, sq_t = _head_tables(cos, sin, 1.0, ATTN_SCALE * math.log2(math.e))
    ck_t, sk_t = _head_tables(cos, sin, 0.0, 1.0)
    one = np.ones((n_ctx, QK_ROPE), np.float32)
    ckc_t, skc_t = _head_tables(one, np.zeros_like(one), 0.0, 1.0)

    q, kx, vx, ux = _proj(x1, mod, g_mix, wa, g_cq, wq, g_ckv, wkv,
                          (cq_t, sq_t, ck_t, sk_t), None, PROJ_TM, True)
    kc, vc, uc = _proj(ctx1, mod, g_mix, wa, g_cq, wq, g_ckv, wkv,
                       (ckc_t, skc_t, ckc_t, skc_t), ctx_row, n_ctx, False)

    attn = _attention(q, kx, kc, vx, vc, ATTN_TQ)

    lre = lam_re.reshape(2, 1, N_STATE)
    lim = lam_im.reshape(2, 1, N_STATE)
    ldt = jnp.repeat(log_dt.reshape(2, SSM_GROUPS), SSM_STATE, axis=-1).reshape(2, 1, N_STATE)
    bre = jnp.swapaxes(b_re.reshape(2, SSM_GROUPS, SSM_STATE, SSM_CH), -1, -2).reshape(2, SSM_WIDTH, SSM_STATE)
    bim = jnp.swapaxes(b_im.reshape(2, SSM_GROUPS, SSM_STATE, SSM_CH), -1, -2).reshape(2, SSM_WIDTH, SSM_STATE)
    cre = c_re.reshape(2, SSM_WIDTH, SSM_STATE)
    cim = c_im.reshape(2, SSM_WIDTH, SSM_STATE)
    y_fwd, y_bwd = [_s5(uc, ux, lre, lim, ldt, bre, bim, cre, cim, n_ctx, rev) for rev in (False, True)]

    return _final(x1, attn, ux, y_fwd, y_bwd, mod, d_skip, layer(w_glu).astype(BF16), g_mla_out, g_ssm_out,
                  layer(w_out).astype(BF16), g_ffn2, layer(w_gu2).astype(BF16), layer(w_down2).astype(BF16),
                  g_final[None], FINAL_TM)
```

```python
import functools
import math

import jax
import jax.numpy as jnp
import numpy as np
from jax import lax
from jax.experimental import pallas as pl
from jax.experimental.pallas import tpu as pltpu

D_MODEL = 1024
GRID_W = 64
N_MOD = 9
D_FF = 2816
MLA_HEADS = 12
QK_NOPE = 64
QK_ROPE = 32
V_HEAD = 64
Q_RANK = 256
KV_RANK = 128
MLA_WIDTH = MLA_HEADS * V_HEAD
SSM_GROUPS = 16
SSM_CH = 16
SSM_WIDTH = SSM_GROUPS * SSM_CH
SSM_STATE = 64
N_STATE = SSM_GROUPS * SSM_STATE
ROPE_BASE = 10000.0
ATTN_SCALE = (QK_NOPE + QK_ROPE) ** -0.5
EPS = 1e-6

LANES = 128
SUBLANES = 8
MXU_DIM = 256
HEAD_PAD = 128
VMEM_LIMIT = 56 * 1024 * 1024

FFN_TM = 1024
CTX_FFN_TM = 512
PROJ_TM = 1024
ATTN_TQ = 1024
FINAL_TM = 512

F32 = jnp.float32
BF16 = jnp.bfloat16


def _rms(x, g):
    ms = jnp.mean(x * x, axis=-1, keepdims=True)
    return x * lax.rsqrt(ms + EPS) * g


def _dot(a, b):
    return jnp.dot(a, b, preferred_element_type=F32)


def _const_spec(shape):
    zeros = (0,) * len(shape)
    return pl.BlockSpec(shape, lambda *_: zeros, pipeline_mode=pl.Buffered(1))


def _params(*sem):
    return pltpu.CompilerParams(dimension_semantics=sem, vmem_limit_bytes=VMEM_LIMIT)


def _adaln_kernel(c_ref, w_ref, b_ref, o_ref):
    c = c_ref[...]
    s = (c * jax.nn.sigmoid(c)).astype(BF16)
    o_ref[...] = _dot(s, w_ref[...].astype(BF16)) + b_ref[...]


def _adaln(cvec, w_mod, b_mod):
    rows = cvec.shape[0]
    n = w_mod.shape[1]
    bn = D_MODEL
    return pl.pallas_call(
        _adaln_kernel,
        grid=(n // bn,),
        in_specs=[pl.BlockSpec((rows, D_MODEL), lambda i: (0, 0)),
                  pl.BlockSpec((D_MODEL, bn), lambda i: (0, i)),
                  pl.BlockSpec((1, bn), lambda i: (0, i))],
        out_specs=pl.BlockSpec((rows, bn), lambda i: (0, i)),
        out_shape=jax.ShapeDtypeStruct((rows, n), F32),
        compiler_params=_params("parallel"),
        name="adaln",
    )(cvec, w_mod, b_mod)


FF_TILES = D_FF // MXU_DIM
FF_SPLITS = (0, (FF_TILES + 1) // 2 * MXU_DIM, D_FF)


def _swiglu(h, wgu_ref, wd_ref):
    acc = None
    for lo, hi in zip(FF_SPLITS[:-1], FF_SPLITS[1:]):
        gate = _dot(h, wgu_ref[:, lo:hi])
        up = _dot(h, wgu_ref[:, D_FF + lo:D_FF + hi])
        act = (gate * jax.nn.sigmoid(gate) * up).astype(BF16)
        part = _dot(act, wd_ref[lo:hi, :])
        acc = part if acc is None else acc + part
    return acc


def _ffn_kernel(x_ref, mod_ref, g_ref, wgu_ref, wd_ref, o_ref):
    x = x_ref[0]
    m = mod_ref[0]
    h = (_rms(x, g_ref[...]) * (1.0 + m[1:2]) + m[0:1]).astype(BF16)
    o_ref[0] = x + 0.5 * m[2:3] * _swiglu(h, wgu_ref, wd_ref)


def _ffn(x, mod, g, wgu, wd, mod_row, tm):
    nb, n, _ = x.shape
    mod_map = (lambda b, j: (b, 0, 0)) if mod_row is None else (lambda b, j: (mod_row, 0, 0))
    return pl.pallas_call(
        _ffn_kernel,
        grid=(nb, n // tm),
        in_specs=[pl.BlockSpec((1, tm, D_MODEL), lambda b, j: (b, j, 0)),
                  pl.BlockSpec((1, N_MOD, D_MODEL), mod_map),
                  _const_spec((1, D_MODEL)),
                  _const_spec((D_MODEL, 2 * D_FF)),
                  _const_spec((D_FF, D_MODEL))],
        out_specs=pl.BlockSpec((1, tm, D_MODEL), lambda b, j: (b, j, 0)),
        out_shape=jax.ShapeDtypeStruct(x.shape, F32),
        compiler_params=_params("parallel", "parallel"),
        name="ffn",
    )(x, mod, g, wgu, wd)


W_A_COLS = Q_RANK + KV_RANK + 2 * HEAD_PAD + SSM_WIDTH
QK_WIDTH = MLA_HEADS * HEAD_PAD


def _proj_kernel(x_ref, mod_ref, g_ref, wa_ref, gcq_ref, wq_ref, gckv_ref, wkv_ref,
                 cq_ref, sq_ref, ck_ref, sk_ref, *out_refs, with_q):
    if with_q:
        q_ref, k_ref, v_ref, u_ref = out_refs
    else:
        k_ref, v_ref, u_ref = out_refs
    x = x_ref[0]
    m = mod_ref[0]
    h = (_rms(x, g_ref[...]) * (1.0 + m[4:5]) + m[3:4]).astype(BF16)
    p = _dot(h, wa_ref[...])
    o_kv = Q_RANK
    o_kr = o_kv + KV_RANK
    o_rot = o_kr + HEAD_PAD
    o_u = o_rot + HEAD_PAD
    u_ref[...] = p[:, o_u:o_u + SSM_WIDTH]
    kr = p[:, o_kr:o_kr + HEAD_PAD] * ck_ref[...] + p[:, o_rot:o_rot + HEAD_PAD] * sk_ref[...]
    ckv = _rms(p[:, o_kv:o_kv + KV_RANK], gckv_ref[...]).astype(BF16)
    kv = _dot(ckv, wkv_ref[...])
    for hd in range(MLA_HEADS):
        lo = hd * HEAD_PAD
        k_ref[0, :, lo:lo + HEAD_PAD] = (kv[:, lo:lo + HEAD_PAD] + kr).astype(BF16)
    v_ref[0] = kv[:, QK_WIDTH:QK_WIDTH + MLA_WIDTH].astype(BF16)
    if with_q:
        cq = _rms(p[:, 0:Q_RANK], gcq_ref[...]).astype(BF16)
        qa = _dot(cq, wq_ref[:, 0:QK_WIDTH])
        qb = _dot(cq, wq_ref[:, QK_WIDTH:2 * QK_WIDTH])
        cq_t = cq_ref[...]
        sq_t = sq_ref[...]
        for hd in range(MLA_HEADS):
            lo = hd * HEAD_PAD
            q_ref[0, :, lo:lo + HEAD_PAD] = (qa[:, lo:lo + HEAD_PAD] * cq_t
                                             + qb[:, lo:lo + HEAD_PAD] * sq_t).astype(BF16)


def _proj(x, mod, g_mix, wa, gcq, wq, gckv, wkv, tabs, mod_row, tm, with_q):
    nb, n, _ = x.shape
    mod_map = (lambda b, j: (b, 0, 0)) if mod_row is None else (lambda b, j: (mod_row, 0, 0))
    tab_spec = pl.BlockSpec((tm, HEAD_PAD), lambda b, j: (j, 0))
    out_shape = [jax.ShapeDtypeStruct((nb, n, QK_WIDTH), BF16),
                 jax.ShapeDtypeStruct((nb, n, MLA_WIDTH), BF16),
                 jax.ShapeDtypeStruct((n, nb * SSM_WIDTH), F32)]
    out_specs = [pl.BlockSpec((1, tm, QK_WIDTH), lambda b, j: (b, j, 0)),
                 pl.BlockSpec((1, tm, MLA_WIDTH), lambda b, j: (b, j, 0)),
                 pl.BlockSpec((tm, SSM_WIDTH), lambda b, j: (j, b))]
    if with_q:
        out_shape = [jax.ShapeDtypeStruct((nb, n, QK_WIDTH), BF16)] + out_shape
        out_specs = [pl.BlockSpec((1, tm, QK_WIDTH), lambda b, j: (b, j, 0))] + out_specs
    return pl.pallas_call(
        functools.partial(_proj_kernel, with_q=with_q),
        grid=(nb, n // tm),
        in_specs=[pl.BlockSpec((1, tm, D_MODEL), lambda b, j: (b, j, 0)),
                  pl.BlockSpec((1, N_MOD, D_MODEL), mod_map),
                  _const_spec((1, D_MODEL)),
                  _const_spec(wa.shape),
                  _const_spec((1, Q_RANK)),
                  _const_spec(wq.shape),
                  _const_spec((1, KV_RANK)),
                  _const_spec(wkv.shape),
                  tab_spec, tab_spec, tab_spec, tab_spec],
        out_specs=out_specs,
        out_shape=out_shape,
        compiler_params=_params("parallel", "parallel"),
        name="proj_q" if with_q else "proj_ctx",
    )(x, mod, g_mix, wa, gcq, wq, gckv, wkv, *tabs)


HEADS_PER_STEP = 4
KEY_SPLITS = (0, 1024, 2048, 3072, 4096)


def _attn_kernel(q_ref, kx_ref, kc_ref, vx_ref, vc_ref, o_ref):
    nt = (((1,), (1,)), ((), ()))
    tq = q_ref.shape[1]
    assert KEY_SPLITS[-1] == kx_ref.shape[1]
    chunks = [(kx_ref, vx_ref, lo, hi - lo) for lo, hi in zip(KEY_SPLITS[:-1], KEY_SPLITS[1:])]
    chunks.append((kc_ref, vc_ref, 0, kc_ref.shape[1]))
    outs = []
    for hh in range(HEADS_PER_STEP):
        lo = hh * HEAD_PAD
        q = q_ref[0, :, lo:lo + HEAD_PAD]
        m_b = acc = None
        for k_ref, v_ref, start, size in chunks:
            s = lax.dot_general(q, k_ref[0, start:start + size, lo:lo + HEAD_PAD], nt,
                                preferred_element_type=F32)
            cols = [s[:, i * LANES:(i + 1) * LANES] for i in range(size // LANES)]
            c_vec = functools.reduce(jnp.maximum, cols)
            c_max = jnp.broadcast_to(jnp.max(c_vec, axis=-1, keepdims=True), (tq, LANES))
            m_new = c_max if m_b is None else jnp.maximum(m_b, c_max)
            p = jnp.concatenate([jnp.exp2(col - m_new).astype(BF16) for col in cols], axis=-1)
            v_lo = (hh // 2) * 2 * V_HEAD
            v_ext = jnp.concatenate([v_ref[0, start:start + size, v_lo:v_lo + 2 * V_HEAD],
                                     jnp.ones((size, LANES), BF16)], axis=-1)
            pv = _dot(p, v_ext)
            if m_b is None:
                acc = pv
            else:
                alpha = jnp.exp2(m_b - m_new)
                acc = jnp.concatenate([alpha, alpha], axis=-1) * acc + pv
            m_b = m_new
        outs.append(acc[:, 0:LANES] / acc[:, LANES:2 * LANES])
    lane = lax.broadcasted_iota(jnp.int32, outs[0].shape, 1)
    for pr in range(HEADS_PER_STEP // 2):
        o_ref[0, :, pr * 2 * V_HEAD:(pr + 1) * 2 * V_HEAD] = jnp.where(
            lane < V_HEAD, outs[2 * pr], outs[2 * pr + 1]).astype(o_ref.dtype)


def _attention(q, kx, kc, vx, vc, tq):
    nb, n, _ = q.shape
    nk = kx.shape[1]
    nc = kc.shape[1]
    pairs = MLA_HEADS // HEADS_PER_STEP
    qk_blk = HEADS_PER_STEP * HEAD_PAD
    v_blk = HEADS_PER_STEP * V_HEAD
    return pl.pallas_call(
        _attn_kernel,
        grid=(nb, pairs, n // tq),
        in_specs=[pl.BlockSpec((1, tq, qk_blk), lambda b, p, j: (b, j, p)),
                  pl.BlockSpec((1, nk, qk_blk), lambda b, p, j: (b, 0, p)),
                  pl.BlockSpec((1, nc, qk_blk), lambda b, p, j: (b, 0, p)),
                  pl.BlockSpec((1, nk, v_blk), lambda b, p, j: (b, 0, p)),
                  pl.BlockSpec((1, nc, v_blk), lambda b, p, j: (b, 0, p))],
        out_specs=pl.BlockSpec((1, tq, v_blk), lambda b, p, j: (b, j, p)),
        out_shape=jax.ShapeDtypeStruct((nb, n, MLA_WIDTH), BF16),
        compiler_params=_params("parallel", "parallel", "parallel"),
        name="attention",
    )(q, kx, kc, vx, vc)


SCAN_BLOCK = 32


def _s5_kernel(uc_ref, ux_ref, lre_ref, lim_ref, ldt_ref, bre_ref, bim_ref, cre_ref, cim_ref,
               y_ref, bb_ref, cm_ref, a_ref, h_ref, bd_ref, xb_ref, *, t_chunk, reverse):
    j = pl.program_id(0)

    @pl.when(j == 0)
    def _discretize():
        dt = jnp.exp(ldt_ref[...])
        lr = jnp.minimum(lre_ref[...], -1e-4)
        li = lim_ref[...]
        mag = jnp.exp(lr * dt)
        ar = mag * jnp.cos(li * dt)
        ai = mag * jnp.sin(li * dt)
        den = lr * lr + li * li
        fr = ((ar - 1.0) * lr + ai * li) / den
        fi = (ai * lr - (ar - 1.0) * li) / den
        same_group = (lax.broadcasted_iota(jnp.int32, (SSM_WIDTH, N_STATE), 0) // SSM_CH
                      == lax.broadcasted_iota(jnp.int32, (SSM_WIDTH, N_STATE), 1) // SSM_STATE)

        def block_diag(ref):
            return jnp.where(same_group, jnp.concatenate([ref[...]] * SSM_GROUPS, axis=-1), 0.0)

        bre = block_diag(bre_ref)
        bim = block_diag(bim_ref)
        bb_ref[:, 0:N_STATE] = (fr * bre - fi * bim).astype(BF16)
        bb_ref[:, N_STATE:2 * N_STATE] = (fr * bim + fi * bre).astype(BF16)
        cm_ref[0:N_STATE, :] = block_diag(cre_ref).T.astype(BF16)
        cm_ref[N_STATE:2 * N_STATE, :] = (-block_diag(cim_ref)).T.astype(BF16)
        a_ref[0:SUBLANES, :] = jnp.broadcast_to(ar, (SUBLANES, N_STATE))
        a_ref[SUBLANES:2 * SUBLANES, :] = jnp.broadcast_to(ai, (SUBLANES, N_STATE))
        h_ref[...] = jnp.zeros_like(h_ref)

    blk_rows = SCAN_BLOCK * SUBLANES
    order = list(range(t_chunk // SCAN_BLOCK))
    steps = list(range(SCAN_BLOCK))
    if reverse:
        order.reverse()
        steps.reverse()

    def run(u_ref, with_readout):
        ar = a_ref[0:SUBLANES, :]
        ai = a_ref[SUBLANES:2 * SUBLANES, :]

        def drive(sb):
            u = u_ref[sb * SCAN_BLOCK:(sb + 1) * SCAN_BLOCK, :].reshape(blk_rows, SSM_WIDTH)
            bd_ref[sb * blk_rows:(sb + 1) * blk_rows, :] = _dot(u.astype(BF16), bb_ref[...])

        def scan(sb, hr, hi):
            for k in range(0, SCAN_BLOCK, 2):
                pair = {}
                for t in (steps[k], steps[k + 1]):
                    r = sb * blk_rows + t * SUBLANES
                    nr = ar * hr - ai * hi + bd_ref[r:r + SUBLANES, 0:N_STATE]
                    ni = ar * hi + ai * hr + bd_ref[r:r + SUBLANES, N_STATE:2 * N_STATE]
                    pair[t] = (nr, ni)
                    hr, hi = nr, ni
                if not with_readout:
                    continue
                t0 = min(pair)
                r = sb * blk_rows + t0 * SUBLANES
                xb_ref[r:r + 2 * SUBLANES, 0:N_STATE] = jnp.concatenate(
                    [pair[t0][0], pair[t0 + 1][0]], axis=0).astype(BF16)
                xb_ref[r:r + 2 * SUBLANES, N_STATE:2 * N_STATE] = jnp.concatenate(
                    [pair[t0][1], pair[t0 + 1][1]], axis=0).astype(BF16)
            return hr, hi

        def readout(sb):
            y = _dot(xb_ref[sb * blk_rows:(sb + 1) * blk_rows, :], cm_ref[...])
            y_ref[sb * SCAN_BLOCK:(sb + 1) * SCAN_BLOCK, :] = y.reshape(SCAN_BLOCK, SUBLANES * SSM_WIDTH)

        hr = h_ref[0:SUBLANES, :]
        hi = h_ref[SUBLANES:2 * SUBLANES, :]
        drive(order[0])
        for k, sb in enumerate(order):
            if k + 1 < len(order):
                drive(order[k + 1])
            hr, hi = scan(sb, hr, hi)
            if with_readout and k >= 1:
                readout(order[k - 1])
        if with_readout:
            readout(order[-1])
        h_ref[0:SUBLANES, :] = hr
        h_ref[SUBLANES:2 * SUBLANES, :] = hi

    @pl.when(j == 0)
    def _context():
        run(uc_ref, False)

    @pl.when(j > 0)
    def _latent():
        run(ux_ref, True)


def _s5(u_c, u_x, lre, lim, ldt, bre, bim, cre, cim, t_chunk, reverse):
    rows = t_chunk * SUBLANES
    width = SUBLANES * SSM_WIDTH
    assert u_c.shape == (t_chunk, width) and u_x.shape[1] == width
    nch = u_x.shape[0] // t_chunk

    def chunk(j):
        jj = jnp.maximum(j - 1, 0)
        return nch - 1 - jj if reverse else jj

    d = int(reverse)
    vec_spec = pl.BlockSpec((None, 1, N_STATE), lambda j: (d, 0, 0), pipeline_mode=pl.Buffered(1))
    mat_spec = pl.BlockSpec((None, SSM_WIDTH, SSM_STATE), lambda j: (d, 0, 0), pipeline_mode=pl.Buffered(1))
    return pl.pallas_call(
        functools.partial(_s5_kernel, t_chunk=t_chunk, reverse=reverse),
        grid=(nch + 1,),
        in_specs=[pl.BlockSpec((t_chunk, width), lambda j: (0, 0)),
                  pl.BlockSpec((t_chunk, width), lambda j: (chunk(j), 0)),
                  vec_spec, vec_spec, vec_spec, mat_spec, mat_spec, mat_spec, mat_spec],
        out_specs=pl.BlockSpec((t_chunk, width), lambda j: (chunk(j), 0)),
        out_shape=jax.ShapeDtypeStruct((u_x.shape[0], width), F32),
        scratch_shapes=[pltpu.VMEM((SSM_WIDTH, 2 * N_STATE), BF16),
                        pltpu.VMEM((2 * N_STATE, SSM_WIDTH), BF16),
                        pltpu.VMEM((2 * SUBLANES, N_STATE), F32),
                        pltpu.VMEM((2 * SUBLANES, N_STATE), F32),
                        pltpu.VMEM((rows, 2 * N_STATE), F32),
                        pltpu.VMEM((rows, 2 * N_STATE), BF16)],
        compiler_params=_params("arbitrary"),
        name="s5_rev" if reverse else "s5_fwd",
    )(u_c, u_x, lre, lim, ldt, bre, bim, cre, cim)


def _gelu_tanh(x):
    return 0.5 * x * (1.0 + jnp.tanh(math.sqrt(2.0 / math.pi) * (x + 0.044715 * (x * x * x))))


def _final_kernel(x_ref, attn_ref, u_ref, yf_ref, yb_ref, mod_ref, dsk_ref, wglu_ref, gmla_ref, gssm_ref,
                  wo_ref, g2_ref, wgu_ref, wd_ref, gf_ref, o_ref):
    x = x_ref[0]
    m = mod_ref[0]
    y = u_ref[...] * dsk_ref[...] + yf_ref[...] + yb_ref[...]
    z = _gelu_tanh(y).astype(BF16)
    ag = _dot(z, wglu_ref[...])
    ssm = ag[:, 0:SSM_WIDTH] * jax.nn.sigmoid(ag[:, SSM_WIDTH:2 * SSM_WIDTH])
    a_n = _rms(attn_ref[0].astype(F32), gmla_ref[...]).astype(BF16)
    s_n = _rms(ssm, gssm_ref[...]).astype(BF16)
    mix = _dot(a_n, wo_ref[0:MLA_WIDTH, :]) + _dot(s_n, wo_ref[MLA_WIDTH:MLA_WIDTH + SSM_WIDTH, :])
    x = x + m[5:6] * mix
    h = (_rms(x, g2_ref[...]) * (1.0 + m[7:8]) + m[6:7]).astype(BF16)
    x = x + 0.5 * m[8:9] * _swiglu(h, wgu_ref, wd_ref)
    o_ref[0] = _rms(x, gf_ref[...])


def _final(x, attn, u, y_fwd, y_bwd, mod, dsk, wglu, gmla, gssm, wo, g2, wgu, wd, gf, tm):
    nb, n, _ = x.shape
    ssm_spec = pl.BlockSpec((tm, SSM_WIDTH), lambda b, j: (j, b))
    return pl.pallas_call(
        _final_kernel,
        grid=(nb, n // tm),
        in_specs=[pl.BlockSpec((1, tm, D_MODEL), lambda b, j: (b, j, 0)),
                  pl.BlockSpec((1, tm, MLA_WIDTH), lambda b, j: (b, j, 0)),
                  ssm_spec, ssm_spec, ssm_spec,
                  pl.BlockSpec((1, N_MOD, D_MODEL), lambda b, j: (b, 0, 0)),
                  _const_spec((1, SSM_WIDTH)),
                  _const_spec((SSM_WIDTH, 2 * SSM_WIDTH)),
                  _const_spec((1, MLA_WIDTH)),
                  _const_spec((1, SSM_WIDTH)),
                  _const_spec((MLA_WIDTH + SSM_WIDTH, D_MODEL)),
                  _const_spec((1, D_MODEL)),
                  _const_spec((D_MODEL, 2 * D_FF)),
                  _const_spec((D_FF, D_MODEL)),
                  _const_spec((1, D_MODEL))],
        out_specs=pl.BlockSpec((1, tm, D_MODEL), lambda b, j: (b, j, 0)),
        out_shape=jax.ShapeDtypeStruct(x.shape, F32),
        compiler_params=_params("parallel", "parallel"),
        name="final",
    )(x, attn, u, y_fwd, y_bwd, mod, dsk, wglu, gmla, gssm, wo, g2, wgu, wd, gf)


def _rot_cols(w):
    wp = w.reshape(w.shape[0], -1, 2)
    return jnp.stack([-wp[..., 1], wp[..., 0]], axis=-1).reshape(w.shape)


def _rope_tables(n_tokens):
    rows = n_tokens // GRID_W
    row = np.repeat(np.arange(rows), GRID_W).astype(np.float32)
    col = np.tile(np.arange(GRID_W), rows).astype(np.float32)
    per_axis = QK_ROPE // 2
    inv_freq = (ROPE_BASE ** (-np.arange(0, per_axis, 2, dtype=np.float32) / per_axis)).astype(np.float32)
    ang = np.concatenate([row[:, None] * inv_freq, col[:, None] * inv_freq], axis=-1)
    cos = np.repeat(np.cos(ang), 2, axis=-1).astype(np.float32)
    sin = np.repeat(np.sin(ang), 2, axis=-1).astype(np.float32)
    return cos, sin


def _head_tables(cos, sin, nope_fill, scale):
    n = cos.shape[0]
    pad = HEAD_PAD - QK_NOPE - QK_ROPE
    c = np.concatenate([np.full((n, QK_NOPE), nope_fill, np.float32), cos, np.zeros((n, pad), np.float32)], axis=-1)
    s = np.concatenate([np.zeros((n, QK_NOPE), np.float32), sin, np.zeros((n, pad), np.float32)], axis=-1)
    return (c * np.float32(scale)).astype(np.float32), (s * np.float32(scale)).astype(np.float32)


def _pad_head_cols(w_nope, w_rope):
    k, h = w_nope.shape[:2]
    pad = HEAD_PAD - QK_NOPE - QK_ROPE
    return jnp.concatenate([w_nope, w_rope, jnp.zeros((k, h, pad), w_nope.dtype)], axis=-1).reshape(k, h * HEAD_PAD)


def kernel(x, c, ctx, c_ctx, w_mod, b_mod, g_ffn1, w_gu1, w_down1, g_mix, w_in, g_cq, w_uq, g_ckv, w_ukv,
           lam_re, lam_im, log_dt, b_re, b_im, c_re, c_im, d_skip, w_glu, g_mla_out, g_ssm_out, w_out,
           g_ffn2, w_gu2, w_down2, g_final):
    bsz, seq, _ = x.shape
    n_ctx = ctx.shape[1]
    assert w_mod.shape[0] == 1, "single-layer block"

    def layer(a):
        return a.reshape(a.shape[1:])

    rows = 2 * SUBLANES
    cvec = jnp.concatenate([c, c_ctx[None], jnp.zeros((rows - bsz - 1, D_MODEL), F32)], axis=0)
    mod = _adaln(cvec, layer(w_mod), b_mod).reshape(rows, N_MOD, D_MODEL)
    ctx_row = bsz

    wgu1 = layer(w_gu1).astype(BF16)
    wd1 = layer(w_down1).astype(BF16)
    x1 = _ffn(x, mod, g_ffn1, wgu1, wd1, None, FFN_TM)
    ctx1 = _ffn(ctx.reshape(1, bsz * n_ctx, D_MODEL), mod, g_ffn1, wgu1, wd1, ctx_row, CTX_FFN_TM)
    ctx1 = ctx1.reshape(bsz, n_ctx, D_MODEL)

    win = layer(w_in)
    o_kv = Q_RANK
    o_kr = o_kv + KV_RANK
    o_u = o_kr + QK_ROPE
    w_kr = win[:, o_kr:o_kr + QK_ROPE]
    zk = jnp.zeros((D_MODEL, QK_NOPE), F32)
    zp = jnp.zeros((D_MODEL, HEAD_PAD - QK_NOPE - QK_ROPE), F32)
    wa = jnp.concatenate([win[:, 0:o_kr], zk, w_kr, zp, zk, _rot_cols(w_kr), zp, win[:, o_u:]], axis=-1).astype(BF16)
    wuq = w_uq.reshape(Q_RANK, MLA_HEADS, QK_NOPE + QK_ROPE)
    wq_nope, wq_rope = wuq[..., :QK_NOPE], wuq[..., QK_NOPE:]
    wq_rot = _rot_cols(wq_rope.reshape(Q_RANK, -1)).reshape(wq_rope.shape)
    wq = jnp.concatenate([_pad_head_cols(wq_nope, wq_rope),
                          _pad_head_cols(jnp.zeros_like(wq_nope), wq_rot)], axis=-1).astype(BF16)
    wukv = w_ukv.reshape(KV_RANK, MLA_HEADS, QK_NOPE + V_HEAD)
    wk = _pad_head_cols(wukv[..., :QK_NOPE], jnp.zeros((KV_RANK, MLA_HEADS, QK_ROPE), F32))
    wv = wukv[..., QK_NOPE:].reshape(KV_RANK, MLA_WIDTH)
    wkv = jnp.concatenate([wk, wv], axis=-1).astype(BF16)

    cos, sin = _rope_tables(seq)
    cq_t, sq_t = _head_tables(cos, sin, 1.0, ATTN_SCALE * math.log2(math.e))
    ck_t, sk_t = _head_tables(cos, sin, 0.0, 1.0)
    one = np.ones((n_ctx, QK_ROPE), np.float32)
    ckc_t, skc_t = _head_tables(one, np.zeros_like(one), 0.0, 1.0)

    q, kx, vx, ux = _proj(x1, mod, g_mix, wa, g_cq, wq, g_ckv, wkv,
                          (cq_t, sq_t, ck_t, sk_t), None, PROJ_TM, True)
    kc, vc, uc = _proj(ctx1, mod, g_mix, wa, g_cq, wq, g_ckv, wkv,
                       (ckc_t, skc_t, ckc_t, skc_t), ctx_row, n_ctx, False)

    attn = _attention(q, kx, kc, vx, vc, ATTN_TQ)

    lre = lam_re.reshape(2, 1, N_STATE)
    lim = lam_im.reshape(2, 1, N_STATE)
    ldt = jnp.repeat(log_dt.reshape(2, SSM_GROUPS), SSM_STATE, axis=-1).reshape(2, 1, N_STATE)
    bre = jnp.swapaxes(b_re.reshape(2, SSM_GROUPS, SSM_STATE, SSM_CH), -1, -2).reshape(2, SSM_WIDTH, SSM_STATE)
    bim = jnp.swapaxes(b_im.reshape(2, SSM_GROUPS, SSM_STATE, SSM_CH), -1, -2).reshape(2, SSM_WIDTH, SSM_STATE)
    cre = c_re.reshape(2, SSM_WIDTH, SSM_STATE)
    cim = c_im.reshape(2, SSM_WIDTH, SSM_STATE)
    y_fwd, y_bwd = [_s5(uc, ux, lre, lim, ldt, bre, bim, cre, cim, n_ctx, rev) for rev in (False, True)]

    return _final(x1, attn, ux, y_fwd, y_bwd, mod, d_skip, layer(w_glu).astype(BF16), g_mla_out, g_ssm_out,
                  layer(w_out).astype(BF16), g_ffn2, layer(w_gu2).astype(BF16), layer(w_down2).astype(BF16),
                  g_final[None], FINAL_TM)
```
